```python
import jax
import jax.numpy as jnp
from jax import lax
import numpy as np

D_MODEL = 1024
BATCH = 32
SEQ = 256
DEPTH = 2
DEC_BATCH = 2
DEC_SEQ = 4096
PAST_LEN = 512

GRID_W = 64
HEAD_DIM = 64
AXIS_DIM = HEAD_DIM // 2
ROPE_THETA = 10000.0
BLK = 128
WINDOW = 128
RET_HEADS = D_MODEL // 128
RET_DK = HEAD_DIM
RET_DV = HEAD_DIM
WIN_HEADS = D_MODEL // 128
WIN_KV_HEADS = WIN_HEADS // 4
ATT_HEADS = D_MODEL // HEAD_DIM
ATT_KV_HEADS = ATT_HEADS // 4
RET_QW = RET_HEADS * RET_DK
RET_VW = RET_HEADS * RET_DV
WIN_QW = WIN_HEADS * HEAD_DIM
WIN_KW = WIN_KV_HEADS * HEAD_DIM
EVEN_IN = 2 * RET_QW + 2 * RET_VW + WIN_QW + 2 * WIN_KW
EVEN_MIX = RET_VW + WIN_QW
ATT_QW = ATT_HEADS * HEAD_DIM
ATT_KW = ATT_KV_HEADS * HEAD_DIM
ODD_IN = ATT_QW + 2 * ATT_KW
N_EXPERTS = 16
CAPACITY_FACTOR = 2
EXPERT_FF = ((8 * D_MODEL // 3 + 127) // 128) * 128
N_EVEN = (DEPTH + 1) // 2
N_ODD = DEPTH // 2
EPS = 1e-6
NEG_INF = -1e30
F32 = jnp.float32

kernel_name = 'hybrid_flow_trunk_step'


def rmsnorm(x, g):
    xf = x.astype(F32)
    y = xf * lax.rsqrt(jnp.mean(xf * xf, axis=-1, keepdims=True) + EPS)
    return (y * g.astype(F32)).astype(x.dtype)


def split_heads(x, n):
    b, l, _ = x.shape
    return x.reshape(b, l, n, -1).transpose(0, 2, 1, 3)


def merge_heads(x):
    b, h, l, d = x.shape
    return x.transpose(0, 2, 1, 3).reshape(b, l, h * d)


def modulation(cvec, w_ada, b_ada):
    m = jax.nn.silu(cvec) @ w_ada + b_ada
    return jnp.split(m[:, None, :], 6, axis=-1)


def pre_norm(x, g, shift, scale):
    return rmsnorm(x, g) * (1 + scale) + shift


def post_norm(x, y, g, gate):
    return x + gate * rmsnorm(y, g)


def rope_tables(rows):
    t = jnp.arange(rows * GRID_W)
    row = (t // GRID_W).astype(F32)
    col = (t % GRID_W).astype(F32)
    inv = ROPE_THETA ** (-jnp.arange(0, AXIS_DIM, 2, dtype=F32) / AXIS_DIM)
    ar = row[:, None] * inv[None]
    ac = col[:, None] * inv[None]
    ang = jnp.concatenate([ar, ar, ac, ac], axis=-1)
    return jnp.cos(ang), jnp.sin(ang)


def apply_rope(x, cos, sin):
    xf = x.astype(F32)
    xr = xf.reshape(xf.shape[:-1] + (2, 2, AXIS_DIM // 2))
    rot = jnp.stack([-xr[..., 1, :], xr[..., 0, :]], axis=-2).reshape(xf.shape)
    return (xf * cos + rot * sin).astype(x.dtype)


def retention_scan(q, k, v, log_gamma, s0):
    b, h, l, _ = q.shape
    dv = v.shape[-1]
    nc = l // BLK

    def chunks(t):
        return t.astype(F32).reshape(b, h, nc, BLK, t.shape[-1]).transpose(2, 0, 1, 3, 4)

    lg = log_gamma.astype(F32)
    idx = jnp.arange(BLK, dtype=F32)
    diff = idx[:, None] - idx[None, :]
    dmask = jnp.where(diff >= 0, jnp.exp(lg[:, None, None] * jnp.maximum(diff, 0.0)), 0.0)
    q_dec = jnp.exp(lg[:, None] * (idx + 1.0))[:, :, None]
    k_dec = jnp.exp(lg[:, None] * (BLK - 1.0 - idx))[:, :, None]
    c_dec = jnp.exp(lg * BLK)[:, None, None]

    def step(s, qkv):
        qc, kc, vc = qkv
        a = jnp.einsum('bhqd,bhkd->bhqk', qc, kc) * dmask
        o = jnp.einsum('bhqk,bhkv->bhqv', a, vc) + jnp.einsum('bhqd,bhdv->bhqv', qc * q_dec, s)
        s = s * c_dec + jnp.einsum('bhkd,bhkv->bhdv', kc * k_dec, vc)
        return s, o

    s, o = lax.scan(step, s0.astype(F32), (chunks(q), chunks(k), chunks(v)))
    return o.transpose(1, 2, 0, 3, 4).reshape(b, h, l, dv), s


def bidir_retention(q, k, v, lg_f, lg_b, s0f, s0b):
    k = k * (RET_DK ** -0.5)
    of, sf = retention_scan(q, k, v, lg_f, s0f)
    flip = lambda t: jnp.flip(t, axis=2)
    ob, sb = retention_scan(flip(q), flip(k), flip(v), lg_b, s0b)
    return of + flip(ob), sf, sb


def retention_readout(o, g):
    mu = jnp.mean(o, axis=-1, keepdims=True)
    var = jnp.mean(jnp.square(o - mu), axis=-1, keepdims=True)
    y = (o - mu) * lax.rsqrt(var + EPS)
    return merge_heads(y).astype(g.dtype) * jax.nn.silu(g)


def block_attention(q, k, v, sink=None):
    b, hq, l, hd = q.shape
    hkv = k.shape[1]
    g = hq // hkv
    nb = l // BLK
    scale = hd ** -0.5
    qb = q.reshape(b, hkv, g, nb, BLK, hd).transpose(3, 0, 1, 2, 4, 5)

    def one(qi):
        s = jnp.einsum('bhgqd,bhkd->bhgqk', qi, k).astype(F32) * scale
        if sink is not None:
            sc = jnp.broadcast_to(sink.astype(F32).reshape(hkv, g, 1, 1), s.shape[:-1] + (1,))
            p = jax.nn.softmax(jnp.concatenate([s, sc], axis=-1), axis=-1)[..., :-1]
        else:
            p = jax.nn.softmax(s, axis=-1)
        return jnp.einsum('bhgqk,bhkd->bhgqd', p.astype(v.dtype), v)

    o = lax.map(one, qb)
    return o.transpose(1, 2, 3, 0, 4, 5).reshape(b, hq, l, hd)


def window_attention(q, k, v, kc, vc, sink):
    b, hq, l, hd = q.shape
    hkv = k.shape[1]
    g = hq // hkv
    nb = l // BLK
    scale = hd ** -0.5
    qb = q.reshape(b, hkv, g, nb, BLK, hd)
    pad = ((0, 0), (0, 0), (BLK, BLK), (0, 0))
    kp = jnp.pad(k, pad).reshape(b, hkv, nb + 2, BLK, hd)
    vp = jnp.pad(v, pad).reshape(b, hkv, nb + 2, BLK, hd)
    kblk = jnp.concatenate([kp[:, :, 0:nb], kp[:, :, 1:nb + 1], kp[:, :, 2:nb + 2]], axis=3)
    vblk = jnp.concatenate([vp[:, :, 0:nb], vp[:, :, 1:nb + 1], vp[:, :, 2:nb + 2]], axis=3)
    r = jnp.arange(3 * BLK)
    i = jnp.arange(BLK)
    j = jnp.arange(nb)
    band = jnp.abs(r[None, :] - BLK - i[:, None]) <= WINDOW
    kpos = j[:, None] * BLK + r[None, :] - BLK
    valid = (kpos >= 0) & (kpos < l)
    mask = band[None] & valid[:, None, :]
    kc = kc.astype(k.dtype)
    vc = vc.astype(v.dtype)
    s_loc = jnp.einsum('bhgnqd,bhnkd->bhgnqk', qb, kblk).astype(F32) * scale
    s_loc = jnp.where(mask, s_loc, NEG_INF)
    s_ctx = jnp.einsum('bhgnqd,bhcd->bhgnqc', qb, kc).astype(F32) * scale
    s_snk = jnp.broadcast_to(sink.astype(F32).reshape(hkv, g, 1, 1, 1), s_loc.shape[:-1] + (1,))
    p = jax.nn.softmax(jnp.concatenate([s_loc, s_ctx, s_snk], axis=-1), axis=-1)
    w = 3 * BLK
    lc = kc.shape[2]
    o = (jnp.einsum('bhgnqk,bhnkd->bhgnqd', p[..., :w].astype(v.dtype), vblk)
         + jnp.einsum('bhgnqc,bhcd->bhgnqd', p[..., w:w + lc].astype(v.dtype), vc))
    return o.reshape(b, hq, l, hd)


def even_proj(h, w_in):
    p = h @ w_in
    o1 = RET_QW
    o2 = o1 + RET_QW
    o3 = o2 + RET_VW
    o4 = o3 + RET_VW
    o5 = o4 + WIN_QW
    o6 = o5 + WIN_KW
    qa, ka, va, ga, qb, kb, vb = jnp.split(p, [o1, o2, o3, o4, o5, o6], axis=-1)
    return (split_heads(qa, RET_HEADS), split_heads(ka, RET_HEADS), split_heads(va, RET_HEADS), ga,
            split_heads(qb, WIN_HEADS), split_heads(kb, WIN_KV_HEADS), split_heads(vb, WIN_KV_HEADS))


def even_mixer_context(h, w_in, w_out, lg_f, lg_b, sink):
    qa, ka, va, ga, qb, kb, vb = even_proj(h, w_in)
    zero = jnp.zeros((h.shape[0], RET_HEADS, RET_DK, RET_DV), F32)
    oa, sf, sb = bidir_retention(qa, ka, va, lg_f, lg_b, zero, zero)
    ob = block_attention(qb, kb, vb, sink)
    y = jnp.concatenate([retention_readout(oa, ga), merge_heads(ob)], axis=-1) @ w_out
    return y, sf, sb, kb, vb


def even_mixer_latent(h, w_in, w_out, lg_f, lg_b, sink, sf, sb, kc, vc, cos, sin):
    qa, ka, va, ga, qb, kb, vb = even_proj(h, w_in)
    qa, ka = apply_rope(qa, cos, sin), apply_rope(ka, cos, sin)
    qb, kb = apply_rope(qb, cos, sin), apply_rope(kb, cos, sin)
    oa, _, _ = bidir_retention(qa, ka, va, lg_f, lg_b, sf, sb)
    ob = window_attention(qb, kb, vb, kc, vc, sink)
    return jnp.concatenate([retention_readout(oa, ga), merge_heads(ob)], axis=-1) @ w_out


def odd_proj(h, w_in, qn, kn):
    q, k, v = jnp.split(h @ w_in, [ATT_QW, ATT_QW + ATT_KW], axis=-1)
    return (rmsnorm(split_heads(q, ATT_HEADS), qn), rmsnorm(split_heads(k, ATT_KV_HEADS), kn),
            split_heads(v, ATT_KV_HEADS))


def odd_mixer_context(h, w_in, w_out, qn, kn):
    q, k, v = odd_proj(h, w_in, qn, kn)
    return merge_heads(block_attention(q, k, v)) @ w_out, k, v


def odd_mixer_latent(h, w_in, w_out, qn, kn, kc, vc, cos, sin):
    q, k, v = odd_proj(h, w_in, qn, kn)
    q, k = apply_rope(q, cos, sin), apply_rope(k, cos, sin)
    k_all = jnp.concatenate([k, kc.astype(k.dtype)], axis=2)
    v_all = jnp.concatenate([v, vc.astype(v.dtype)], axis=2)
    return merge_heads(block_attention(q, k_all, v_all)) @ w_out


def expert_choice_ffn(h, w_router, w_gate, w_up, w_down):
    b, n, d = h.shape
    cap = CAPACITY_FACTOR * n // N_EXPERTS
    aff = jax.nn.softmax(jnp.einsum('bnd,de->ben', h, w_router).astype(F32), axis=1)
    gw, idx = lax.top_k(aff, cap)
    xg = jax.vmap(lambda xb, ib: xb[ib])(h, idx)
    hid = jax.nn.silu(jnp.einsum('becd,edf->becf', xg, w_gate)) * jnp.einsum('becd,edf->becf', xg, w_up)
    y = jnp.einsum('becf,efd->becd', hid, w_down) * gw[..., None].astype(h.dtype)
    return jax.vmap(lambda yb, ib: jnp.zeros((n, d), yb.dtype).at[ib.reshape(-1)].add(yb.reshape(-1, d)))(y, idx)


def setup_inputs(seed: int = 0) -> dict:
    key = jax.random.key(seed)
    ks = jax.random.split(key, 26)
    nrm = lambda k, shape, s: jax.random.normal(k, shape, F32) * s
    base_decay = jnp.log(2.0 ** (5.0 + jnp.arange(RET_HEADS, dtype=F32)) - 1.0)
    return {
        'x_prompt': nrm(ks[0], (BATCH, SEQ, D_MODEL), 1.0),
        'x_sample': nrm(ks[1], (DEC_BATCH, DEC_SEQ, D_MODEL), 1.0),
        'state_ret_fwd': nrm(ks[2], (DEC_BATCH, N_EVEN, RET_HEADS, RET_DK, RET_DV), 1.0),
        'state_ret_bwd': nrm(ks[3], (DEC_BATCH, N_EVEN, RET_HEADS, RET_DK, RET_DV), 1.0),
        'cache_win_k': nrm(ks[4], (DEC_BATCH, N_EVEN, WIN_KV_HEADS, PAST_LEN, HEAD_DIM), 1.0),
        'cache_win_v': nrm(ks[5], (DEC_BATCH, N_EVEN, WIN_KV_HEADS, PAST_LEN, HEAD_DIM), 1.0),
        'cache_attn_k': nrm(ks[6], (DEC_BATCH, N_ODD, ATT_KV_HEADS, PAST_LEN, HEAD_DIM), 1.0),
        'cache_attn_v': nrm(ks[7], (DEC_BATCH, N_ODD, ATT_KV_HEADS, PAST_LEN, HEAD_DIM), 1.0),
        'c': nrm(ks[8], (DEC_BATCH, D_MODEL), 1.0),
        'c_ctx': nrm(ks[9], (D_MODEL,), 1.0),
        'w_ada': nrm(ks[10], (DEPTH, D_MODEL, 6 * D_MODEL), 0.5 * D_MODEL ** -0.5),
        'b_ada': nrm(ks[11], (DEPTH, 6 * D_MODEL), 0.02),
        'norm_gains': 1.0 + nrm(ks[12], (DEPTH, 4, D_MODEL), 0.05),
        'w_in_even': nrm(ks[13], (N_EVEN, D_MODEL, EVEN_IN), D_MODEL ** -0.5),
        'w_out_even': nrm(ks[14], (N_EVEN, EVEN_MIX, D_MODEL), EVEN_MIX ** -0.5),
        'ret_decay_fwd': base_decay + nrm(ks[15], (N_EVEN, RET_HEADS), 0.1),
        'ret_decay_bwd': base_decay + nrm(ks[16], (N_EVEN, RET_HEADS), 0.1),
        'win_sink': nrm(ks[17], (N_EVEN, WIN_HEADS), 0.5),
        'w_in_odd': nrm(ks[18], (N_ODD, D_MODEL, ODD_IN), D_MODEL ** -0.5),
        'w_out_odd': nrm(ks[19], (N_ODD, ATT_QW, D_MODEL), ATT_QW ** -0.5),
        'q_norm': 1.0 + nrm(ks[20], (N_ODD, HEAD_DIM), 0.05),
        'k_norm': 1.0 + nrm(ks[21], (N_ODD, HEAD_DIM), 0.05),
        'w_router': nrm(ks[22], (DEPTH, D_MODEL, N_EXPERTS), D_MODEL ** -0.5),
        'w_gate': nrm(ks[23], (DEPTH, N_EXPERTS, D_MODEL, EXPERT_FF), D_MODEL ** -0.5),
        'w_up': nrm(ks[24], (DEPTH, N_EXPERTS, D_MODEL, EXPERT_FF), D_MODEL ** -0.5),
        'w_down': nrm(ks[25], (DEPTH, N_EXPERTS, EXPERT_FF, D_MODEL), EXPERT_FF ** -0.5),
    }


def reference(x_prompt, x_sample, state_ret_fwd, state_ret_bwd, cache_win_k, cache_win_v, cache_attn_k, cache_attn_v,
              c, c_ctx, w_ada, b_ada, norm_gains, w_in_even, w_out_even, ret_decay_fwd, ret_decay_bwd, win_sink,
              w_in_odd, w_out_odd, q_norm, k_norm, w_router, w_gate, w_up, w_down):
    rows = x_sample.shape[1] // GRID_W
    cos, sin = rope_tables(rows)
    xp, xs = x_prompt, x_sample
    rf_l, rb_l, wk_l, wv_l, ak_l, av_l = [], [], [], [], [], []
    for l in range(DEPTH):
        j = l // 2
        gns = norm_gains[l]
        mp = modulation(c_ctx[None, :], w_ada[l], b_ada[l])
        ms = modulation(c, w_ada[l], b_ada[l])
        hp = pre_norm(xp, gns[0], mp[0], mp[1])
        hs = pre_norm(xs, gns[0], ms[0], ms[1])
        if l % 2 == 0:
            lg_f = jax.nn.log_sigmoid(ret_decay_fwd[j].astype(F32))
            lg_b = jax.nn.log_sigmoid(ret_decay_bwd[j].astype(F32))
            yp, sf, sb, kb, vb = even_mixer_context(hp, w_in_even[j], w_out_even[j], lg_f, lg_b, win_sink[j])
            ys = even_mixer_latent(hs, w_in_even[j], w_out_even[j], lg_f, lg_b, win_sink[j],
                                   state_ret_fwd[:, j], state_ret_bwd[:, j], cache_win_k[:, j], cache_win_v[:, j],
                                   cos, sin)
            rf_l.append(sf.astype(x_prompt.dtype))
            rb_l.append(sb.astype(x_prompt.dtype))
            wk_l.append(kb)
            wv_l.append(vb)
        else:
            yp, ka, va = odd_mixer_context(hp, w_in_odd[j], w_out_odd[j], q_norm[j], k_norm[j])
            ys = odd_mixer_latent(hs, w_in_odd[j], w_out_odd[j], q_norm[j], k_norm[j],
                                  cache_attn_k[:, j], cache_attn_v[:, j], cos, sin)
            ak_l.append(ka)
            av_l.append(va)
        xp = post_norm(xp, yp, gns[1], mp[2])
        xs = post_norm(xs, ys, gns[1], ms[2])
        hp = pre_norm(xp, gns[2], mp[3], mp[4])
        hs = pre_norm(xs, gns[2], ms[3], ms[4])
        xp = post_norm(xp, expert_choice_ffn(hp, w_router[l], w_gate[l], w_up[l], w_down[l]), gns[3], mp[5])
        xs = post_norm(xs, expert_choice_ffn(hs, w_router[l], w_gate[l], w_up[l], w_down[l]), gns[3], ms[5])
    y_prompt = xp
    y_sample = xs
    new_ret_fwd = jnp.stack(rf_l, axis=1)
    new_ret_bwd = jnp.stack(rb_l, axis=1)
    new_win_k = jnp.stack(wk_l, axis=1)
    new_win_v = jnp.stack(wv_l, axis=1)
    new_attn_k = jnp.stack(ak_l, axis=1)
    new_attn_v = jnp.stack(av_l, axis=1)
    return (y_prompt, y_sample, new_ret_fwd, new_ret_bwd, new_win_k, new_win_v, new_attn_k, new_attn_v)
```

```python
import functools

import jax
import jax.numpy as jnp
from jax import lax
from jax.experimental import pallas as pl
from jax.experimental.pallas import tpu as pltpu

D_MODEL = 1024
BATCH = 32
SEQ = 256
DEPTH = 2
DEC_BATCH = 2
DEC_SEQ = 4096
PAST_LEN = 512
GRID_W = 64
HEAD_DIM = 64
AXIS_DIM = HEAD_DIM // 2
ROPE_THETA = 10000.0
BLK = 128
WINDOW = 128
RET_HEADS = D_MODEL // 128
WIN_HEADS = D_MODEL // 128
WIN_KV_HEADS = WIN_HEADS // 4
ATT_HEADS = D_MODEL // HEAD_DIM
ATT_KV_HEADS = ATT_HEADS // 4
RET_QW = RET_HEADS * HEAD_DIM
WIN_QW = WIN_HEADS * HEAD_DIM
WIN_KW = WIN_KV_HEADS * HEAD_DIM
EVEN_IN = 4 * RET_QW + WIN_QW + 2 * WIN_KW
ATT_QW = ATT_HEADS * HEAD_DIM
ATT_KW = ATT_KV_HEADS * HEAD_DIM
ODD_IN = ATT_QW + 2 * ATT_KW
N_EXPERTS = 16
CAPACITY_FACTOR = 2
EXPERT_FF = ((8 * D_MODEL // 3 + 127) // 128) * 128
EPS = 1e-6
NEG_INF = -1e30
F32 = jnp.float32
BF16 = jnp.bfloat16

LANES = 128
T_PROMPT = BATCH * SEQ
T_SAMPLE = DEC_BATCH * DEC_SEQ
T_ALL = T_PROMPT + T_SAMPLE
TM = 512
CAP_P = CAPACITY_FACTOR * SEQ // N_EXPERTS
CAP_S = CAPACITY_FACTOR * DEC_SEQ // N_EXPERTS
ROWS_P = BATCH * CAP_P
ROWS_S = DEC_BATCH * CAP_S
TF = 256
VMEM_LIMIT = 56 * 1024 * 1024
QK_SCALE = HEAD_DIM ** -0.5


def _cparams(sem):
    return pltpu.CompilerParams(dimension_semantics=sem, vmem_limit_bytes=VMEM_LIMIT)


def _silu(x):
    return x * (1.0 / (1.0 + jnp.exp(-x)))


def _dot(a, b):
    return jnp.dot(a, b, preferred_element_type=F32)


def _dot_nt(a, b):
    return lax.dot_general(a, b, (((1,), (1,)), ((), ())), preferred_element_type=F32)


def _rms(x, g):
    return x * lax.rsqrt(jnp.mean(x * x, axis=-1, keepdims=True) + EPS) * g


def _pair_block_diag(shape):
    r = lax.broadcasted_iota(jnp.int32, shape, 0)
    c = lax.broadcasted_iota(jnp.int32, shape, 1)
    return (r // HEAD_DIM) == (c // HEAD_DIM)


def _head_sum(x, bd):
    hi = x.astype(BF16)
    lo = (x - hi.astype(F32)).astype(BF16)
    return _dot(hi, bd) + _dot(lo, bd)


def _rope(xb, cos, sin):
    lane = lax.broadcasted_iota(jnp.int32, xb.shape, 1)
    half = AXIS_DIM // 2
    rot = jnp.where((lane % AXIS_DIM) < half,
                    -pltpu.roll(xb, LANES - half, 1), pltpu.roll(xb, half, 1))
    return xb * cos + rot * sin


def _mod_kernel(c_ref, w_ref, b_ref, o_ref):
    o_ref[...] = _dot(_silu(c_ref[...]), w_ref[...]) + b_ref[...]


def _modulation(cond, w_ada, b_ada):
    tn = 1536
    n = 6 * D_MODEL
    return pl.pallas_call(
        _mod_kernel,
        grid=(DEPTH, n // tn),
        in_specs=[
            pl.BlockSpec((8, D_MODEL), lambda l, j: (0, 0)),
            pl.BlockSpec((None, D_MODEL, tn), lambda l, j: (l, 0, j)),
            pl.BlockSpec((None, 1, tn), lambda l, j: (l, 0, j)),
        ],
        out_specs=pl.BlockSpec((None, 8, tn), lambda l, j: (l, 0, j)),
        out_shape=jax.ShapeDtypeStruct((DEPTH, 8, n), F32),
        compiler_params=_cparams(("arbitrary", "arbitrary")),
        name="modulation",
    )(cond, w_ada, b_ada.reshape(DEPTH, 1, n))


def _mod_index(i):
    npt = T_PROMPT // TM
    return jnp.where(i < npt, 0, 1 + (i - npt) // (DEC_SEQ // TM))


def _rope_index(i):
    npt = T_PROMPT // TM
    return jnp.where(i < npt, 0, 1 + (i - npt) % (DEC_SEQ // TM))


def _inproj_kernel(x_ref, mod_ref, g_ref, w_ref, cos_ref, sin_ref, *rest, even):
    if even:
        (o_ref,) = rest
    else:
        qn_ref, kn_ref, o_ref = rest
    h = _rms(x_ref[...], g_ref[...]) * (1.0 + mod_ref[1:2, :]) + mod_ref[0:1, :]
    p = _dot(h.astype(BF16), w_ref[...])
    cos = cos_ref[...]
    sin = sin_ref[...]
    nblk = p.shape[1] // LANES
    if even:
        nq = RET_QW // LANES
        rope_blocks = set(range(0, 2 * nq)) | set(range(4 * nq, 5 * nq + 1))
        scaled = set(range(nq, 2 * nq)) | set(range(4 * nq, 5 * nq))
        normed = {}
    else:
        nq = ATT_QW // LANES
        nk = ATT_KW // LANES
        rope_blocks = set(range(0, nq + nk))
        scaled = set(range(0, nq))
        normed = {b: (qn_ref if b < nq else kn_ref) for b in range(nq + nk)}
        bd = _pair_block_diag((LANES, LANES)).astype(BF16)
    for b in range(nblk):
        blk = p[:, b * LANES:(b + 1) * LANES]
        if b in normed:
            ms = _head_sum(blk * blk, bd) * (1.0 / HEAD_DIM)
            blk = blk * lax.rsqrt(ms + EPS) * normed[b][...]
        if b in rope_blocks:
            blk = _rope(blk, cos, sin)
        if b in scaled:
            blk = blk * QK_SCALE
        o_ref[:, b * LANES:(b + 1) * LANES] = blk


def _inproj(x, mod, layer, gain, w, cos_t, sin_t, qn=None, kn=None):
    even = qn is None
    n = w.shape[1]
    in_specs = [
        pl.BlockSpec((TM, D_MODEL), lambda i: (i, 0)),
        pl.BlockSpec((None, None, 6, D_MODEL), lambda i: (layer, _mod_index(i), 0, 0)),
        pl.BlockSpec((1, D_MODEL), lambda i: (0, 0)),
        pl.BlockSpec((D_MODEL, n), lambda i: (0, 0)),
        pl.BlockSpec((TM, LANES), lambda i: (_rope_index(i), 0)),
        pl.BlockSpec((TM, LANES), lambda i: (_rope_index(i), 0)),
    ]
    args = [x, mod, gain.reshape(1, D_MODEL), w, cos_t, sin_t]
    if not even:
        in_specs += [pl.BlockSpec((1, LANES), lambda i: (0, 0))] * 2
        args += [jnp.tile(qn, 2).reshape(1, LANES), jnp.tile(kn, 2).reshape(1, LANES)]
    return pl.pallas_call(
        functools.partial(_inproj_kernel, even=even),
        grid=(T_ALL // TM,),
        in_specs=in_specs,
        out_specs=pl.BlockSpec((TM, n), lambda i: (i, 0)),
        out_shape=jax.ShapeDtypeStruct((T_ALL, n), F32),
        compiler_params=_cparams(("arbitrary",)),
        name="inproj_even" if even else "inproj_odd",
    )(*args)


def _ret_kernel(q_ref, k_ref, v_ref, g_ref, dm_ref, qd_ref, kd_ref, cm_ref, s0f_ref, s0b_ref,
                o_ref, sf_ref, sb_ref, of_scr, ob_scr, *, nc):
    lane = lax.broadcasted_iota(jnp.int32, (BLK, LANES), 1)
    lo = lane < HEAD_DIM
    bd = _pair_block_diag((LANES, LANES))
    bd16 = bd.astype(BF16)

    def chunk(c, s, d):
        r = pl.ds(pl.multiple_of(c * BLK, BLK), BLK)
        qc = q_ref[r, :]
        kc = k_ref[r, :]
        vc = v_ref[r, :]
        a0 = _dot_nt(qc, jnp.where(lo, kc, 0.0)) * dm_ref[d, 0]
        a1 = _dot_nt(qc, jnp.where(lo, 0.0, kc)) * dm_ref[d, 1]
        o = (_dot(a0, jnp.where(lo, vc, 0.0)) + _dot(a1, jnp.where(lo, 0.0, vc))
             + _dot(qc * qd_ref[d], s))
        kv = _dot((kc * kd_ref[d]).T, vc)
        s = s * cm_ref[d] + jnp.where(bd, kv, 0.0)
        return r, o, s

    def body(c, carry):
        s_f, s_b = carry
        r, o, s_f = chunk(c, s_f, 0)
        of_scr[r, :] = o
        r, o, s_b = chunk(nc - 1 - c, s_b, 1)
        ob_scr[r, :] = o
        return s_f, s_b

    s_f, s_b = lax.fori_loop(0, nc, body, (s0f_ref[...], s0b_ref[...]))
    sf_ref[...] = s_f
    sb_ref[...] = s_b

    def readout(c, carry):
        r = pl.ds(pl.multiple_of(c * BLK, BLK), BLK)
        o = of_scr[r, :] + ob_scr[r, :]
        mu = _head_sum(o, bd16) * (1.0 / HEAD_DIM)
        dlt = o - mu
        var = _head_sum(dlt * dlt, bd16) * (1.0 / HEAD_DIM)
        o_ref[r, :] = dlt * lax.rsqrt(var + EPS) * _silu(g_ref[r, :])
        return carry

    lax.fori_loop(0, nc, readout, 0)


def _retention(p, tabs, s0f, s0b, n_seq, seq_len, row0):
    npair = RET_HEADS // 2
    rb0 = row0 // seq_len
    dm, qd, kd, cm = tabs

    def col(c0):
        return pl.BlockSpec((seq_len, LANES), lambda b, j: (rb0 + b, c0 + j))

    tab4 = pl.BlockSpec((None, 2, LANES, LANES), lambda b, j: (j, 0, 0, 0))
    st = pl.BlockSpec((None, None, LANES, LANES), lambda b, j: (b, j, 0, 0))
    return pl.pallas_call(
        functools.partial(_ret_kernel, nc=seq_len // BLK),
        grid=(n_seq, npair),
        in_specs=[col(0), col(npair), col(2 * npair), col(3 * npair),
                  pl.BlockSpec((None, 2, 2, BLK, BLK), lambda b, j: (j, 0, 0, 0, 0)),
                  tab4, tab4, tab4, st, st],
        out_specs=[pl.BlockSpec((seq_len, LANES), lambda b, j: (b, j)), st, st],
        out_shape=[jax.ShapeDtypeStruct((n_seq * seq_len, RET_QW), F32),
                   jax.ShapeDtypeStruct((n_seq, npair, LANES, LANES), F32),
                   jax.ShapeDtypeStruct((n_seq, npair, LANES, LANES), F32)],
        scratch_shapes=[pltpu.VMEM((seq_len, LANES), F32), pltpu.VMEM((seq_len, LANES), F32)],
        compiler_params=_cparams(("arbitrary", "arbitrary")),
        name="retention",
    )(p, p, p, p, dm, qd, kd, cm, s0f, s0b)


def _retention_tables(decay_f, decay_b):
    npair = RET_HEADS // 2
    idx = jnp.arange(BLK, dtype=F32)
    diff = idx[:, None] - idx[None, :]

    def one(decay, backward):
        lg = jax.nn.log_sigmoid(decay.astype(F32))
        dmask = jnp.where(diff >= 0, jnp.exp(lg[:, None, None] * jnp.maximum(diff, 0.0)), 0.0)
        q_dec = jnp.exp(lg[:, None] * (idx + 1.0))
        k_dec = jnp.exp(lg[:, None] * (BLK - 1.0 - idx))
        c_dec = jnp.exp(lg * BLK)
        if backward:
            dmask = jnp.swapaxes(dmask, 1, 2)
            q_dec = q_dec[:, ::-1]
            k_dec = k_dec[:, ::-1]
        return dmask, q_dec, k_dec, c_dec

    def lanes(t):
        t = t.reshape(npair, 2, BLK)
        return jnp.repeat(jnp.swapaxes(t, 1, 2), HEAD_DIM, axis=2)

    parts = [one(decay_f, False), one(decay_b, True)]
    dm = jnp.stack([p[0].reshape(npair, 2, BLK, BLK) for p in parts], axis=1)
    qd = jnp.stack([lanes(p[1]) for p in parts], axis=1)
    kd = jnp.stack([lanes(p[2]) for p in parts], axis=1)
    bd = _pair_block_diag((LANES, LANES))
    cm = jnp.stack([jnp.where(bd[None], jnp.repeat(p[3].reshape(npair, 2), HEAD_DIM, axis=1)[:, :, None], 0.0)
                    for p in parts], axis=1)
    return dm, qd, kd, cm


def _pair_states(s):
    b = s.shape[0]
    s = s.astype(F32).reshape(b, RET_HEADS // 2, 2, HEAD_DIM, HEAD_DIM)
    z = jnp.zeros_like(s[:, :, 0])
    top = jnp.concatenate([s[:, :, 0], z], axis=-1)
    bot = jnp.concatenate([z, s[:, :, 1]], axis=-1)
    return jnp.concatenate([top, bot], axis=-2)


def _unpair_states(s):
    b = s.shape[0]
    h0 = s[:, :, :HEAD_DIM, :HEAD_DIM]
    h1 = s[:, :, HEAD_DIM:, HEAD_DIM:]
    return jnp.stack([h0, h1], axis=2).reshape(b, RET_HEADS, HEAD_DIM, HEAD_DIM)


def _attn_core(q, k, v, sink_ref, o_ref, mask):
    lane = lax.broadcasted_iota(jnp.int32, k.shape, 1)
    for half in (0, 1):
        sel = (lane < HEAD_DIM) if half == 0 else (lane >= HEAD_DIM)
        k_sel = jnp.where(sel, k, 0.0)
        v_sel = jnp.where(sel, v, 0.0)
        k_oth = pltpu.roll(k_sel, HEAD_DIM, 1)
        v_oth = pltpu.roll(v_sel, HEAD_DIM, 1)
        kv = ((k_sel, v_sel), (k_oth, v_oth)) if half == 0 else ((k_oth, v_oth), (k_sel, v_sel))
        for pp in (0, 1):
            pair = 2 * half + pp
            qp = q[:, pair * LANES:(pair + 1) * LANES]
            acc = None
            for hh, (kk, vv) in enumerate(kv):
                s = _dot_nt(qp, kk)
                if mask is not None:
                    s = jnp.where(mask, s, NEG_INF)
                m = jnp.max(s, axis=-1, keepdims=True)
                if sink_ref is not None:
                    snk = sink_ref[pair, hh:hh + 1, 0:1]
                    m = jnp.maximum(m, snk)
                e = jnp.exp(s - m)
                den = jnp.sum(e, axis=-1, keepdims=True)
                if sink_ref is not None:
                    den = den + jnp.exp(snk - m)
                o = _dot(e, vv) * (1.0 / den)
                acc = o if acc is None else acc + o
            o_ref[:, pair * LANES:(pair + 1) * LANES] = acc


def _full_attn_kernel(q_ref, k_ref, v_ref, *rest, has_sink):
    if has_sink:
        sink_ref, o_ref = rest
    else:
        sink_ref, (o_ref,) = None, rest
    _attn_core(q_ref[...], k_ref[...], v_ref[...], sink_ref, o_ref, None)


def _full_attention(q_arr, q_rb0, q_cb0, k_arr, k_rb0, k_cb0, v_arr, v_rb0, v_cb0,
                    n_seq, lq, lk, tq, n_groups, sink=None):
    nq = lq // tq
    qw = 4 * LANES
    in_specs = [
        pl.BlockSpec((tq, qw), lambda b, g, i: (q_rb0 + b * nq + i, q_cb0 + g)),
        pl.BlockSpec((lk, LANES), lambda b, g, i: (k_rb0 + b, k_cb0 + g)),
        pl.BlockSpec((lk, LANES), lambda b, g, i: (v_rb0 + b, v_cb0 + g)),
    ]
    args = [q_arr, k_arr, v_arr]
    if sink is not None:
        in_specs.append(pl.BlockSpec((None, 4, 2, LANES), lambda b, g, i: (g, 0, 0, 0)))
        args.append(sink)
    return pl.pallas_call(
        functools.partial(_full_attn_kernel, has_sink=sink is not None),
        grid=(n_seq, n_groups, nq),
        in_specs=in_specs,
        out_specs=pl.BlockSpec((tq, qw), lambda b, g, i: (b * nq + i, g)),
        out_shape=jax.ShapeDtypeStruct((n_seq * lq, n_groups * qw), F32),
        compiler_params=_cparams(("arbitrary", "arbitrary", "arbitrary")),
        name="full_attention",
    )(*args)


def _window_attn_kernel(q_ref, kp_ref, kc_ref, kn_ref, vp_ref, vc_ref, vn_ref, kx_ref, vx_ref,
                        sink_ref, o_ref):
    n = pl.program_id(1)
    k = jnp.concatenate([kp_ref[...], kc_ref[...], kn_ref[...], kx_ref[...]], axis=0)
    v = jnp.concatenate([vp_ref[...], vc_ref[...], vn_ref[...], vx_ref[...]], axis=0)
    shape = (BLK, 3 * BLK + PAST_LEN)
    i = lax.broadcasted_iota(jnp.int32, shape, 0)
    r = lax.broadcasted_iota(jnp.int32, shape, 1)
    kpos = (n - 1) * BLK + r
    local = (jnp.abs(r - BLK - i) <= WINDOW) & (kpos >= 0) & (kpos < DEC_SEQ)
    mask = local | (r >= 3 * BLK)
    _attn_core(q_ref[...], k, v, sink_ref, o_ref, mask)


def _window_attention(p, kx, vx, sink):
    nb = DEC_SEQ // BLK
    rb0 = T_PROMPT // BLK
    qcb = (4 * RET_QW) // (4 * LANES)
    kcb = (4 * RET_QW + WIN_QW) // LANES
    vcb = kcb + 1

    def kv(cb, off):
        return pl.BlockSpec((BLK, LANES),
                            lambda b, n: (rb0 + b * nb + jnp.clip(n + off, 0, nb - 1), cb))

    ctx = pl.BlockSpec((None, PAST_LEN, LANES), lambda b, n: (b, 0, 0))
    return pl.pallas_call(
        _window_attn_kernel,
        grid=(DEC_BATCH, nb),
        in_specs=[pl.BlockSpec((BLK, 4 * LANES), lambda b, n: (rb0 + b * nb + n, qcb)),
                  kv(kcb, -1), kv(kcb, 0), kv(kcb, 1), kv(vcb, -1), kv(vcb, 0), kv(vcb, 1),
                  ctx, ctx,
                  pl.BlockSpec((4, 2, LANES), lambda b, n: (0, 0, 0))],
        out_specs=pl.BlockSpec((BLK, 4 * LANES), lambda b, n: (b * nb + n, 0)),
        out_shape=jax.ShapeDtypeStruct((T_SAMPLE, WIN_QW), F32),
        compiler_params=_cparams(("arbitrary", "arbitrary")),
        name="window_attention",
    )(p, p, p, p, p, p, p, kx, vx, sink)


def _sink_table(sink, n_groups):
    s = sink.astype(F32).reshape(n_groups, 4, 2, 1)
    return jnp.broadcast_to(s, (n_groups, 4, 2, LANES))


def _cache_rows(cache):
    b, h, l, d = cache.shape
    return cache.astype(F32).transpose(0, 2, 1, 3).reshape(b, l, h * d)


def _outproj_kernel(*refs, n_in):
    a_refs = refs[:n_in]
    w_refs = refs[n_in:2 * n_in]
    x_ref, mod_ref, g_ref, wr_ref, xn_ref, hf_ref, lg_ref = refs[2 * n_in:]
    y = None
    for a_ref, w_ref in zip(a_refs, w_refs):
        t = _dot(a_ref[...].astype(BF16), w_ref[...])
        y = t if y is None else y + t
    xn = x_ref[...] + mod_ref[2:3, :] * _rms(y, g_ref[1:2, :])
    hf = _rms(xn, g_ref[2:3, :]) * (1.0 + mod_ref[4:5, :]) + mod_ref[3:4, :]
    xn_ref[...] = xn
    hf_ref[...] = hf
    lg_ref[...] = _dot_nt(wr_ref[...], hf.astype(BF16))


def _outproj(parts, w_parts, x, mod, layer, gains, w_router_t):
    n_in = len(parts)
    in_specs = [pl.BlockSpec((TM, a.shape[1]), lambda i: (i, 0)) for a in parts]
    in_specs += [pl.BlockSpec(w.shape, lambda i: (0, 0)) for w in w_parts]
    in_specs += [
        pl.BlockSpec((TM, D_MODEL), lambda i: (i, 0)),
        pl.BlockSpec((None, None, 6, D_MODEL), lambda i: (layer, _mod_index(i), 0, 0)),
        pl.BlockSpec((4, D_MODEL), lambda i: (0, 0)),
        pl.BlockSpec((N_EXPERTS, D_MODEL), lambda i: (0, 0)),
    ]
    row = pl.BlockSpec((TM, D_MODEL), lambda i: (i, 0))
    return pl.pallas_call(
        functools.partial(_outproj_kernel, n_in=n_in),
        grid=(T_ALL // TM,),
        in_specs=in_specs,
        out_specs=[row, row, pl.BlockSpec((N_EXPERTS, TM), lambda i: (0, i))],
        out_shape=[jax.ShapeDtypeStruct((T_ALL, D_MODEL), F32),
                   jax.ShapeDtypeStruct((T_ALL, D_MODEL), F32),
                   jax.ShapeDtypeStruct((N_EXPERTS, T_ALL), F32)],
        compiler_params=_cparams(("arbitrary",)),
        name="outproj",
    )(*parts, *w_parts, x, mod, gains, w_router_t)


def _gather_kernel(idx_ref, h_ref, o_ref, buf, *, eg, cap):
    b = pl.program_id(0)
    g = pl.program_id(1)
    for e in range(eg):
        base = (g * eg + e) * cap

        def body(c, carry):
            t = idx_ref[b, base + c]
            buf[pl.ds(c, 1), :] = h_ref[pl.ds(t, 1), :]
            return carry

        lax.fori_loop(0, cap, body, 0, unroll=8)
        o_ref[e] = buf[...].astype(BF16)


def _gather(h, idx, n_seq, seq_len, cap, row0, eg):
    rb0 = row0 // seq_len
    return pl.pallas_call(
        functools.partial(_gather_kernel, eg=eg, cap=cap),
        grid_spec=pltpu.PrefetchScalarGridSpec(
            num_scalar_prefetch=1,
            grid=(n_seq, N_EXPERTS // eg),
            in_specs=[pl.BlockSpec((seq_len, D_MODEL), lambda b, g, idx: (rb0 + b, 0))],
            out_specs=pl.BlockSpec((eg, cap, D_MODEL), lambda b, g, idx: (g, b, 0)),
            scratch_shapes=[pltpu.VMEM((cap, D_MODEL), F32)],
        ),
        out_shape=jax.ShapeDtypeStruct((N_EXPERTS, n_seq * cap, D_MODEL), BF16),
        compiler_params=_cparams(("arbitrary", "arbitrary")),
        name="moe_gather",
    )(idx, h)


def _moe_kernel(xp_ref, xs_ref, wg_ref, wu_ref, wd_ref, gwp_ref, gws_ref, yp_ref, ys_ref):
    f = pl.program_id(1)
    wg = wg_ref[...].astype(BF16)
    wu = wu_ref[...].astype(BF16)
    wd = wd_ref[...].astype(BF16)
    for x_ref, gw_ref, y_ref in ((xp_ref, gwp_ref, yp_ref), (xs_ref, gws_ref, ys_ref)):
        x = x_ref[...]
        hid = (_silu(_dot(x, wg)) * _dot(x, wu)).astype(BF16)
        part = _dot(hid, wd)

        @pl.when(f == 0)
        def _():
            y_ref[...] = part

        @pl.when(f > 0)
        def _():
            y_ref[...] += part

        @pl.when(f == pl.num_programs(1) - 1)
        def _():
            y_ref[...] = y_ref[...] * gw_ref[...]


def _moe_ffn(xg_p, xg_s, w_gate, w_up, w_down, layer, gw_p, gw_s):
    nf = EXPERT_FF // TF
    xspec_p = pl.BlockSpec((None, ROWS_P, D_MODEL), lambda e, f: (e, 0, 0))
    xspec_s = pl.BlockSpec((None, ROWS_S, D_MODEL), lambda e, f: (e, 0, 0))
    return pl.pallas_call(
        _moe_kernel,
        grid=(N_EXPERTS, nf),
        in_specs=[xspec_p, xspec_s,
                  pl.BlockSpec((None, None, D_MODEL, TF), lambda e, f: (layer, e, 0, f)),
                  pl.BlockSpec((None, None, D_MODEL, TF), lambda e, f: (layer, e, 0, f)),
                  pl.BlockSpec((None, None, TF, D_MODEL), lambda e, f: (layer, e, f, 0)),
                  pl.BlockSpec((None, ROWS_P, 1), lambda e, f: (e, 0, 0)),
                  pl.BlockSpec((None, ROWS_S, 1), lambda e, f: (e, 0, 0))],
        out_specs=[xspec_p, xspec_s],
        out_shape=[jax.ShapeDtypeStruct((N_EXPERTS, ROWS_P, D_MODEL), F32),
                   jax.ShapeDtypeStruct((N_EXPERTS, ROWS_S, D_MODEL), F32)],
        compiler_params=_cparams(("arbitrary", "arbitrary")),
        name="moe_ffn",
    )(xg_p, xg_s, w_gate, w_up, w_down, gw_p, gw_s)


def _combine_kernel(idx_ref, y_ref, o_ref, *, eg, cap):
    b = pl.program_id(0)
    g = pl.program_id(1)

    @pl.when(g == 0)
    def _():
        o_ref[...] = jnp.zeros_like(o_ref)

    for e in range(eg):
        base = (g * eg + e) * cap

        def body(c, carry):
            t = idx_ref[b, base + c]
            o_ref[pl.ds(t, 1), :] = o_ref[pl.ds(t, 1), :] + y_ref[e, pl.ds(c, 1), :]
            return carry

        lax.fori_loop(0, cap, body, 0, unroll=4)


def _combine(y, idx, n_seq, seq_len, cap, eg):
    return pl.pallas_call(
        functools.partial(_combine_kernel, eg=eg, cap=cap),
        grid_spec=pltpu.PrefetchScalarGridSpec(
            num_scalar_prefetch=1,
            grid=(n_seq, N_EXPERTS // eg),
            in_specs=[pl.BlockSpec((eg, cap, D_MODEL), lambda b, g, idx: (g, b, 0))],
            out_specs=pl.BlockSpec((seq_len, D_MODEL), lambda b, g, idx: (b, 0)),
        ),
        out_shape=jax.ShapeDtypeStruct((n_seq * seq_len, D_MODEL), F32),
        compiler_params=_cparams(("arbitrary", "arbitrary")),
        name="moe_combine",
    )(idx, y)


def _postnorm_kernel(x_ref, yp_ref, ys_ref, mod_ref, g_ref, o_ref):
    is_prompt = pl.program_id(0) < (T_PROMPT // TM)
    y = jnp.where(is_prompt, yp_ref[...], ys_ref[...])
    o_ref[...] = x_ref[...] + mod_ref[5:6, :] * _rms(y, g_ref[3:4, :])


def _postnorm(x, y_p, y_s, mod, layer, gains):
    npt = T_PROMPT // TM
    row = pl.BlockSpec((TM, D_MODEL), lambda i: (i, 0))
    return pl.pallas_call(
        _postnorm_kernel,
        grid=(T_ALL // TM,),
        in_specs=[row,
                  pl.BlockSpec((TM, D_MODEL), lambda i: (jnp.minimum(i, npt - 1), 0)),
                  pl.BlockSpec((TM, D_MODEL), lambda i: (jnp.maximum(i - npt, 0), 0)),
                  pl.BlockSpec((None, None, 6, D_MODEL), lambda i: (layer, _mod_index(i), 0, 0)),
                  pl.BlockSpec((4, D_MODEL), lambda i: (0, 0))],
        out_specs=row,
        out_shape=jax.ShapeDtypeStruct((T_ALL, D_MODEL), F32),
        compiler_params=_cparams(("arbitrary",)),
        name="postnorm",
    )(x, y_p, y_s, mod, gains)


def _route(logits_t):
    aff = jax.nn.softmax(logits_t, axis=0)
    aff_p = aff[:, :T_PROMPT].reshape(N_EXPERTS, BATCH, SEQ).transpose(1, 0, 2)
    aff_s = aff[:, T_PROMPT:].reshape(N_EXPERTS, DEC_BATCH, DEC_SEQ).transpose(1, 0, 2)
    gw_p, idx_p = lax.top_k(aff_p, CAP_P)
    gw_s, idx_s = lax.top_k(aff_s, CAP_S)
    return gw_p, idx_p, gw_s, idx_s


def _expert_ffn(hf, logits_t, w_gate, w_up, w_down, layer):
    gw_p, idx_p, gw_s, idx_s = _route(logits_t)
    idx_p = idx_p.reshape(BATCH, N_EXPERTS * CAP_P).astype(jnp.int32)
    idx_s = idx_s.reshape(DEC_BATCH, N_EXPERTS * CAP_S).astype(jnp.int32)
    xg_p = _gather(hf, idx_p, BATCH, SEQ, CAP_P, 0, N_EXPERTS)
    xg_s = _gather(hf, idx_s, DEC_BATCH, DEC_SEQ, CAP_S, T_PROMPT, 4)
    gw_p = gw_p.transpose(1, 0, 2).reshape(N_EXPERTS, ROWS_P, 1)
    gw_s = gw_s.transpose(1, 0, 2).reshape(N_EXPERTS, ROWS_S, 1)
    y_p, y_s = _moe_ffn(xg_p, xg_s, w_gate, w_up, w_down, layer, gw_p, gw_s)
    o_p = _combine(y_p, idx_p, BATCH, SEQ, CAP_P, N_EXPERTS)
    o_s = _combine(y_s, idx_s, DEC_BATCH, DEC_SEQ, CAP_S, 2)
    return o_p, o_s


def _rope_tables():
    t = jnp.arange(DEC_SEQ)
    row = (t // GRID_W).astype(F32)
    col = (t % GRID_W).astype(F32)
    inv = ROPE_THETA ** (-jnp.arange(0, AXIS_DIM, 2, dtype=F32) / AXIS_DIM)
    ar = row[:, None] * inv[None]
    ac = col[:, None] * inv[None]
    ang = jnp.concatenate([ar, ar, ac, ac], axis=-1)
    ang = jnp.concatenate([ang, ang], axis=-1)
    cos = jnp.concatenate([jnp.ones((TM, LANES), F32), jnp.cos(ang)], axis=0)
    sin = jnp.concatenate([jnp.zeros((TM, LANES), F32), jnp.sin(ang)], axis=0)
    return cos, sin


def _split_kv_heads(rows, n_heads):
    return rows.reshape(BATCH, SEQ, n_heads, HEAD_DIM).transpose(0, 2, 1, 3)[:, None]


def kernel(x_prompt, x_sample, state_ret_fwd, state_ret_bwd, cache_win_k, cache_win_v, cache_attn_k, cache_attn_v, c, c_ctx, w_ada, b_ada, norm_gains, w_in_even, w_out_even, ret_decay_fwd, ret_decay_bwd, win_sink, w_in_odd, w_out_odd, q_norm, k_norm, w_router, w_gate, w_up, w_down):
    x = jnp.concatenate([x_prompt.reshape(T_PROMPT, D_MODEL), x_sample.reshape(T_SAMPLE, D_MODEL)], axis=0)
    cond = jnp.concatenate([c_ctx[None, :], c, jnp.zeros((8 - 1 - DEC_BATCH, D_MODEL), F32)], axis=0)
    mod = _modulation(cond, w_ada, b_ada).reshape(DEPTH, 8, 6, D_MODEL)
    cos_t, sin_t = _rope_tables()
    outs = {}
    for layer in range(DEPTH):
        j = layer // 2
        gains = norm_gains[layer]
        if layer % 2 == 0:
            p = _inproj(x, mod, layer, gains[0], w_in_even[j].astype(BF16), cos_t, sin_t)
            tabs = _retention_tables(ret_decay_fwd[j], ret_decay_bwd[j])
            zero = jnp.zeros((BATCH, RET_HEADS // 2, LANES, LANES), F32)
            ret_p, sf, sb = _retention(p, tabs, zero, zero, BATCH, SEQ, 0)
            ret_s, _, _ = _retention(p, tabs, _pair_states(state_ret_fwd[:, j]),
                                     _pair_states(state_ret_bwd[:, j]), DEC_BATCH, DEC_SEQ, T_PROMPT)
            sink = _sink_table(win_sink[j], 1)
            qcb = (4 * RET_QW) // (4 * LANES)
            kcb = (4 * RET_QW + WIN_QW) // LANES
            win_p = _full_attention(p, 0, qcb, p, 0, kcb, p, 0, kcb + 1,
                                    BATCH, SEQ, SEQ, SEQ, 1, sink=sink)
            win_s = _window_attention(p, _cache_rows(cache_win_k[:, j]), _cache_rows(cache_win_v[:, j]),
                                      sink[0])
            w_out = w_out_even[j].astype(BF16)
            parts = [jnp.concatenate([ret_p, ret_s], axis=0), jnp.concatenate([win_p, win_s], axis=0)]
            w_parts = [w_out[:RET_QW], w_out[RET_QW:]]
            outs["ret_f"] = _unpair_states(sf)[:, None]
            outs["ret_b"] = _unpair_states(sb)[:, None]
            c0 = 4 * RET_QW + WIN_QW
            outs["win_k"] = _split_kv_heads(p[:T_PROMPT, c0:c0 + WIN_KW], WIN_KV_HEADS)
            outs["win_v"] = _split_kv_heads(p[:T_PROMPT, c0 + WIN_KW:c0 + 2 * WIN_KW], WIN_KV_HEADS)
        else:
            p = _inproj(x, mod, layer, gains[0], w_in_odd[j].astype(BF16), cos_t, sin_t,
                        qn=q_norm[j], kn=k_norm[j])
            ngr = ATT_KV_HEADS // 2
            kcb = ATT_QW // LANES
            att_p = _full_attention(p, 0, 0, p, 0, kcb, p, 0, kcb + ngr,
                                    BATCH, SEQ, SEQ, SEQ, ngr)
            k_lat = p[T_PROMPT:, ATT_QW:ATT_QW + ATT_KW].reshape(DEC_BATCH, DEC_SEQ, ATT_KW)
            v_lat = p[T_PROMPT:, ATT_QW + ATT_KW:].reshape(DEC_BATCH, DEC_SEQ, ATT_KW)
            lk = DEC_SEQ + PAST_LEN
            k_all = jnp.concatenate([k_lat, _cache_rows(cache_attn_k[:, j])], axis=1).reshape(DEC_BATCH * lk, ATT_KW)
            v_all = jnp.concatenate([v_lat, _cache_rows(cache_attn_v[:, j])], axis=1).reshape(DEC_BATCH * lk, ATT_KW)
            tq = 128
            att_s = _full_attention(p, T_PROMPT // tq, 0, k_all, 0, 0, v_all, 0, 0,
                                    DEC_BATCH, DEC_SEQ, lk, tq, ngr)
            parts = [jnp.concatenate([att_p, att_s], axis=0)]
            w_parts = [w_out_odd[j].astype(BF16)]
            outs["att_k"] = _split_kv_heads(p[:T_PROMPT, ATT_QW:ATT_QW + ATT_KW], ATT_KV_HEADS)
            outs["att_v"] = _split_kv_heads(p[:T_PROMPT, ATT_QW + ATT_KW:], ATT_KV_HEADS)
        xn, hf, logits_t = _outproj(parts, w_parts, x, mod, layer, gains,
                                    w_router[layer].T.astype(BF16))
        o_p, o_s = _expert_ffn(hf, logits_t, w_gate, w_up, w_down, layer)
        x = _postnorm(xn, o_p, o_s, mod, layer, gains)
    y_prompt = x[:T_PROMPT].reshape(BATCH, SEQ, D_MODEL)
    y_sample = x[T_PROMPT:].reshape(DEC_BATCH, DEC_SEQ, D_MODEL)
    return (y_prompt, y_sample, outs["ret_f"], outs["ret_b"], outs["win_k"], outs["win_v"],
            outs["att_k"], outs["att_v"])
```

```python
import functools

import jax
import jax.numpy as jnp
from jax import lax
from jax.experimental import pallas as pl
from jax.experimental.pallas import tpu as pltpu

D_MODEL = 1024
BATCH = 32
SEQ = 256
DEPTH = 2
DEC_BATCH = 2
DEC_SEQ = 4096
PAST_LEN = 512
GRID_W = 64
HEAD_DIM = 64
AXIS_DIM = HEAD_DIM // 2
ROPE_THETA = 10000.0
BLK = 128
WINDOW = 128
RET_HEADS = D_MODEL // 128
WIN_HEADS = D_MODEL // 128
WIN_KV_HEADS = WIN_HEADS // 4
ATT_HEADS = D_MODEL // HEAD_DIM
ATT_KV_HEADS = ATT_HEADS // 4
RET_QW = RET_HEADS * HEAD_DIM
WIN_QW = WIN_HEADS * HEAD_DIM
WIN_KW = WIN_KV_HEADS * HEAD_DIM
EVEN_IN = 4 * RET_QW + WIN_QW + 2 * WIN_KW
ATT_QW = ATT_HEADS * HEAD_DIM
ATT_KW = ATT_KV_HEADS * HEAD_DIM
ODD_IN = ATT_QW + 2 * ATT_KW
N_EXPERTS = 16
CAPACITY_FACTOR = 2
EXPERT_FF = ((8 * D_MODEL // 3 + 127) // 128) * 128
EPS = 1e-6
NEG_INF = -1e30
F32 = jnp.float32
BF16 = jnp.bfloat16

LANES = 128
T_PROMPT = BATCH * SEQ
T_SAMPLE = DEC_BATCH * DEC_SEQ
T_ALL = T_PROMPT + T_SAMPLE
TM = 512
CAP_P = CAPACITY_FACTOR * SEQ // N_EXPERTS
CAP_S = CAPACITY_FACTOR * DEC_SEQ // N_EXPERTS
ROWS_P = BATCH * CAP_P
ROWS_S = DEC_BATCH * CAP_S
TF = 256
VMEM_LIMIT = 56 * 1024 * 1024
QK_SCALE = HEAD_DIM ** -0.5


def _cparams(sem):
    return pltpu.CompilerParams(dimension_semantics=sem, vmem_limit_bytes=VMEM_LIMIT)


def _silu(x):
    return x * (1.0 / (1.0 + jnp.exp(-x)))


def _dot(a, b):
    return jnp.dot(a, b, preferred_element_type=F32)


def _dot_nt(a, b):
    return lax.dot_general(a, b, (((1,), (1,)), ((), ())), preferred_element_type=F32)


def _rms(x, g):
    return x * lax.rsqrt(jnp.mean(x * x, axis=-1, keepdims=True) + EPS) * g


def _pair_block_diag(shape):
    r = lax.broadcasted_iota(jnp.int32, shape, 0)
    c = lax.broadcasted_iota(jnp.int32, shape, 1)
    return (r // HEAD_DIM) == (c // HEAD_DIM)


def _head_sum(x, bd):
    hi = x.astype(BF16)
    lo = (x - hi.astype(F32)).astype(BF16)
    return _dot(hi, bd) + _dot(lo, bd)


def _rope(xb, cos, sin):
    lane = lax.broadcasted_iota(jnp.int32, xb.shape, 1)
    half = AXIS_DIM // 2
    rot = jnp.where((lane % AXIS_DIM) < half,
                    -pltpu.roll(xb, LANES - half, 1), pltpu.roll(xb, half, 1))
    return xb * cos + rot * sin


def _mod_kernel(c_ref, w_ref, b_ref, o_ref):
    o_ref[...] = _dot(_silu(c_ref[...]), w_ref[...]) + b_ref[...]


def _modulation(cond, w_ada, b_ada):
    tn = 1536
    n = 6 * D_MODEL
    return pl.pallas_call(
        _mod_kernel,
        grid=(DEPTH, n // tn),
        in_specs=[
            pl.BlockSpec((8, D_MODEL), lambda l, j: (0, 0)),
            pl.BlockSpec((None, D_MODEL, tn), lambda l, j: (l, 0, j)),
            pl.BlockSpec((None, 1, tn), lambda l, j: (l, 0, j)),
        ],
        out_specs=pl.BlockSpec((None, 8, tn), lambda l, j: (l, 0, j)),
        out_shape=jax.ShapeDtypeStruct((DEPTH, 8, n), F32),
        compiler_params=_cparams(("arbitrary", "arbitrary")),
        name="modulation",
    )(cond, w_ada, b_ada.reshape(DEPTH, 1, n))


def _mod_index(i):
    npt = T_PROMPT // TM
    return jnp.where(i < npt, 0, 1 + (i - npt) // (DEC_SEQ // TM))


def _rope_index(i):
    npt = T_PROMPT // TM
    return jnp.where(i < npt, 0, 1 + (i - npt) % (DEC_SEQ // TM))


NPT = T_PROMPT // TM


def _pair_specs(width):
    return [pl.BlockSpec((TM, width), lambda i: (jnp.minimum(i, NPT - 1), 0)),
            pl.BlockSpec((TM, width), lambda i: (jnp.maximum(i - NPT, 0), 0))]


def _pick(p_ref, s_ref):
    return jnp.where(pl.program_id(0) < NPT, p_ref[...], s_ref[...])


def _inproj_kernel(xp_ref, xs_ref, mod_ref, g_ref, w_ref, cos_ref, sin_ref, *rest, even):
    if even:
        (o_ref,) = rest
    else:
        qn_ref, kn_ref, o_ref = rest
    h = _rms(_pick(xp_ref, xs_ref), g_ref[...]) * (1.0 + mod_ref[1:2, :]) + mod_ref[0:1, :]
    p = _dot(h.astype(BF16), w_ref[...])
    cos = cos_ref[...]
    sin = sin_ref[...]
    nblk = p.shape[1] // LANES
    if even:
        nq = RET_QW // LANES
        rope_blocks = set(range(0, 2 * nq)) | set(range(4 * nq, 5 * nq + 1))
        scaled = set(range(nq, 2 * nq)) | set(range(4 * nq, 5 * nq))
        normed = {}
    else:
        nq = ATT_QW // LANES
        nk = ATT_KW // LANES
        rope_blocks = set(range(0, nq + nk))
        scaled = set(range(0, nq))
        normed = {b: (qn_ref if b < nq else kn_ref) for b in range(nq + nk)}
        bd = _pair_block_diag((LANES, LANES)).astype(BF16)
    for b in range(nblk):
        blk = p[:, b * LANES:(b + 1) * LANES]
        if b in normed:
            ms = _head_sum(blk * blk, bd) * (1.0 / HEAD_DIM)
            blk = blk * lax.rsqrt(ms + EPS) * normed[b][...]
        if b in rope_blocks:
            blk = _rope(blk, cos, sin)
        if b in scaled:
            blk = blk * QK_SCALE
        o_ref[:, b * LANES:(b + 1) * LANES] = blk


def _inproj(x, mod, layer, gain, w, cos_t, sin_t, qn=None, kn=None):
    even = qn is None
    n = w.shape[1]
    in_specs = _pair_specs(D_MODEL) + [
        pl.BlockSpec((None, None, 6, D_MODEL), lambda i: (layer, _mod_index(i), 0, 0)),
        pl.BlockSpec((1, D_MODEL), lambda i: (0, 0)),
        pl.BlockSpec((D_MODEL, n), lambda i: (0, 0)),
        pl.BlockSpec((TM, LANES), lambda i: (_rope_index(i), 0)),
        pl.BlockSpec((TM, LANES), lambda i: (_rope_index(i), 0)),
    ]
    args = [x[0], x[1], mod, gain.reshape(1, D_MODEL), w, cos_t, sin_t]
    if not even:
        in_specs += [pl.BlockSpec((1, LANES), lambda i: (0, 0))] * 2
        args += [jnp.tile(qn, 2).reshape(1, LANES), jnp.tile(kn, 2).reshape(1, LANES)]
    return pl.pallas_call(
        functools.partial(_inproj_kernel, even=even),
        grid=(T_ALL // TM,),
        in_specs=in_specs,
        out_specs=pl.BlockSpec((TM, n), lambda i: (i, 0)),
        out_shape=jax.ShapeDtypeStruct((T_ALL, n), F32),
        compiler_params=_cparams(("arbitrary",)),
        name="inproj_even" if even else "inproj_odd",
    )(*args)


def _ret_kernel(q_ref, k_ref, v_ref, g_ref, dm_ref, qd_ref, kd_ref, cm_ref, s0f_ref, s0b_ref,
                o_ref, sf_ref, sb_ref, of_scr, ob_scr, *, nc):
    lane = lax.broadcasted_iota(jnp.int32, (BLK, LANES), 1)
    lo = lane < HEAD_DIM
    bd = _pair_block_diag((LANES, LANES))
    bd16 = bd.astype(BF16)

    def chunk(c, s, d):
        r = pl.ds(pl.multiple_of(c * BLK, BLK), BLK)
        qc = q_ref[r, :]
        kc = k_ref[r, :]
        vc = v_ref[r, :]
        a0 = _dot_nt(qc, jnp.where(lo, kc, 0.0)) * dm_ref[d, 0]
        a1 = _dot_nt(qc, jnp.where(lo, 0.0, kc)) * dm_ref[d, 1]
        o = (_dot(a0, jnp.where(lo, vc, 0.0)) + _dot(a1, jnp.where(lo, 0.0, vc))
             + _dot(qc * qd_ref[d], s))
        kv = _dot((kc * kd_ref[d]).T, vc)
        s = s * cm_ref[d] + jnp.where(bd, kv, 0.0)
        return r, o, s

    def body(c, carry):
        s_f, s_b = carry
        r, o, s_f = chunk(c, s_f, 0)
        of_scr[r, :] = o
        r, o, s_b = chunk(nc - 1 - c, s_b, 1)
        ob_scr[r, :] = o
        return s_f, s_b

    s_f, s_b = lax.fori_loop(0, nc, body, (s0f_ref[...], s0b_ref[...]))
    sf_ref[...] = s_f
    sb_ref[...] = s_b

    def readout(c, carry):
        r = pl.ds(pl.multiple_of(c * BLK, BLK), BLK)
        o = of_scr[r, :] + ob_scr[r, :]
        mu = _head_sum(o, bd16) * (1.0 / HEAD_DIM)
        dlt = o - mu
        var = _head_sum(dlt * dlt, bd16) * (1.0 / HEAD_DIM)
        o_ref[r, :] = dlt * lax.rsqrt(var + EPS) * _silu(g_ref[r, :])
        return carry

    lax.fori_loop(0, nc, readout, 0)


def _retention(p, tabs, s0f, s0b, n_seq, seq_len, row0):
    npair = RET_HEADS // 2
    rb0 = row0 // seq_len
    dm, qd, kd, cm = tabs

    def col(c0):
        return pl.BlockSpec((seq_len, LANES), lambda b, j: (rb0 + b, c0 + j))

    tab4 = pl.BlockSpec((None, 2, LANES, LANES), lambda b, j: (j, 0, 0, 0))
    st = pl.BlockSpec((None, None, LANES, LANES), lambda b, j: (b, j, 0, 0))
    return pl.pallas_call(
        functools.partial(_ret_kernel, nc=seq_len // BLK),
        grid=(n_seq, npair),
        in_specs=[col(0), col(npair), col(2 * npair), col(3 * npair),
                  pl.BlockSpec((None, 2, 2, BLK, BLK), lambda b, j: (j, 0, 0, 0, 0)),
                  tab4, tab4, tab4, st, st],
        out_specs=[pl.BlockSpec((seq_len, LANES), lambda b, j: (b, j)), st, st],
        out_shape=[jax.ShapeDtypeStruct((n_seq * seq_len, RET_QW), F32),
                   jax.ShapeDtypeStruct((n_seq, npair, LANES, LANES), F32),
                   jax.ShapeDtypeStruct((n_seq, npair, LANES, LANES), F32)],
        scratch_shapes=[pltpu.VMEM((seq_len, LANES), F32), pltpu.VMEM((seq_len, LANES), F32)],
        compiler_params=_cparams(("arbitrary", "arbitrary")),
        name="retention",
    )(p, p, p, p, dm, qd, kd, cm, s0f, s0b)


def _retention_tables(decay_f, decay_b):
    npair = RET_HEADS // 2
    idx = jnp.arange(BLK, dtype=F32)
    diff = idx[:, None] - idx[None, :]

    def one(decay, backward):
        lg = jax.nn.log_sigmoid(decay.astype(F32))
        dmask = jnp.where(diff >= 0, jnp.exp(lg[:, None, None] * jnp.maximum(diff, 0.0)), 0.0)
        q_dec = jnp.exp(lg[:, None] * (idx + 1.0))
        k_dec = jnp.exp(lg[:, None] * (BLK - 1.0 - idx))
        c_dec = jnp.exp(lg * BLK)
        if backward:
            dmask = jnp.swapaxes(dmask, 1, 2)
            q_dec = q_dec[:, ::-1]
            k_dec = k_dec[:, ::-1]
        return dmask, q_dec, k_dec, c_dec

    def lanes(t):
        t = t.reshape(npair, 2, BLK)
        return jnp.repeat(jnp.swapaxes(t, 1, 2), HEAD_DIM, axis=2)

    parts = [one(decay_f, False), one(decay_b, True)]
    dm = jnp.stack([p[0].reshape(npair, 2, BLK, BLK) for p in parts], axis=1)
    qd = jnp.stack([lanes(p[1]) for p in parts], axis=1)
    kd = jnp.stack([lanes(p[2]) for p in parts], axis=1)
    bd = _pair_block_diag((LANES, LANES))
    cm = jnp.stack([jnp.where(bd[None], jnp.repeat(p[3].reshape(npair, 2), HEAD_DIM, axis=1)[:, :, None], 0.0)
                    for p in parts], axis=1)
    return dm, qd, kd, cm


def _pair_states(s):
    b = s.shape[0]
    s = s.astype(F32).reshape(b, RET_HEADS // 2, 2, HEAD_DIM, HEAD_DIM)
    z = jnp.zeros_like(s[:, :, 0])
    top = jnp.concatenate([s[:, :, 0], z], axis=-1)
    bot = jnp.concatenate([z, s[:, :, 1]], axis=-1)
    return jnp.concatenate([top, bot], axis=-2)


def _unpair_states(s):
    b = s.shape[0]
    h0 = s[:, :, :HEAD_DIM, :HEAD_DIM]
    h1 = s[:, :, HEAD_DIM:, HEAD_DIM:]
    return jnp.stack([h0, h1], axis=2).reshape(b, RET_HEADS, HEAD_DIM, HEAD_DIM)


def _attn_core(q, k, v, sink_ref, o_ref, mask):
    lane = lax.broadcasted_iota(jnp.int32, k.shape, 1)
    for half in (0, 1):
        sel = (lane < HEAD_DIM) if half == 0 else (lane >= HEAD_DIM)
        k_sel = jnp.where(sel, k, 0.0)
        v_sel = jnp.where(sel, v, 0.0)
        k_oth = pltpu.roll(k_sel, HEAD_DIM, 1)
        v_oth = pltpu.roll(v_sel, HEAD_DIM, 1)
        kv = ((k_sel, v_sel), (k_oth, v_oth)) if half == 0 else ((k_oth, v_oth), (k_sel, v_sel))
        for pp in (0, 1):
            pair = 2 * half + pp
            qp = q[:, pair * LANES:(pair + 1) * LANES]
            acc = None
            for hh, (kk, vv) in enumerate(kv):
                s = _dot_nt(qp, kk)
                if mask is not None:
                    s = jnp.where(mask, s, NEG_INF)
                m = jnp.max(s, axis=-1, keepdims=True)
                if sink_ref is not None:
                    snk = sink_ref[pair, hh:hh + 1, 0:1]
                    m = jnp.maximum(m, snk)
                e = jnp.exp(s - m)
                den = jnp.sum(e, axis=-1, keepdims=True)
                if sink_ref is not None:
                    den = den + jnp.exp(snk - m)
                o = _dot(e, vv) * (1.0 / den)
                acc = o if acc is None else acc + o
            o_ref[:, pair * LANES:(pair + 1) * LANES] = acc


def _full_attn_kernel(q_ref, k_ref, v_ref, *rest, has_sink):
    if has_sink:
        sink_ref, o_ref = rest
    else:
        sink_ref, (o_ref,) = None, rest
    _attn_core(q_ref[...], k_ref[...], v_ref[...], sink_ref, o_ref, None)


def _full_attention(q_arr, q_rb0, q_cb0, k_arr, k_rb0, k_cb0, v_arr, v_rb0, v_cb0,
                    n_seq, lq, lk, tq, n_groups, sink=None):
    nq = lq // tq
    qw = 4 * LANES
    in_specs = [
        pl.BlockSpec((tq, qw), lambda b, g, i: (q_rb0 + b * nq + i, q_cb0 + g)),
        pl.BlockSpec((lk, LANES), lambda b, g, i: (k_rb0 + b, k_cb0 + g)),
        pl.BlockSpec((lk, LANES), lambda b, g, i: (v_rb0 + b, v_cb0 + g)),
    ]
    args = [q_arr, k_arr, v_arr]
    if sink is not None:
        in_specs.append(pl.BlockSpec((None, 4, 2, LANES), lambda b, g, i: (g, 0, 0, 0)))
        args.append(sink)
    return pl.pallas_call(
        functools.partial(_full_attn_kernel, has_sink=sink is not None),
        grid=(n_seq, n_groups, nq),
        in_specs=in_specs,
        out_specs=pl.BlockSpec((tq, qw), lambda b, g, i: (b * nq + i, g)),
        out_shape=jax.ShapeDtypeStruct((n_seq * lq, n_groups * qw), F32),
        compiler_params=_cparams(("arbitrary", "arbitrary", "arbitrary")),
        name="full_attention",
    )(*args)


LOG2E = 1.4426950408889634


def _latent_attn_kernel(q_ref, k_ref, v_ref, kx_ref, vx_ref, o_ref, kvar, vvar, *, tq, kb):
    lk = DEC_SEQ + PAST_LEN

    @pl.when(pl.program_id(2) == 0)
    def _():
        for src_k, src_v, r0, nrows in ((k_ref, v_ref, 0, DEC_SEQ), (kx_ref, vx_ref, DEC_SEQ, PAST_LEN)):
            for c in range(nrows // kb):
                src = pl.ds(c * kb, kb)
                dst = pl.ds(r0 + c * kb, kb)
                lo = lax.broadcasted_iota(jnp.int32, (kb, LANES), 1) < HEAD_DIM
                for ref, var in ((src_k, kvar), (src_v, vvar)):
                    t = ref[src, :]
                    a = jnp.where(lo, t, 0.0)
                    b = jnp.where(lo, 0.0, t)
                    var[0, dst, :] = a.astype(BF16)
                    var[1, dst, :] = pltpu.roll(a, HEAD_DIM, 1).astype(BF16)
                    var[2, dst, :] = pltpu.roll(b, HEAD_DIM, 1).astype(BF16)
                    var[3, dst, :] = b.astype(BF16)

    for half in (0, 1):
        q2 = jnp.concatenate([q_ref[:, (2 * half) * LANES:(2 * half + 1) * LANES],
                              q_ref[:, (2 * half + 1) * LANES:(2 * half + 2) * LANES]], axis=0)
        q2 = (q2 * LOG2E).astype(BF16)
        out = None
        for hh in (0, 1):
            var = 2 * half + hh
            m = jnp.full((2 * tq, 1), -jnp.inf, F32)
            l = jnp.zeros((2 * tq, 1), F32)
            acc = jnp.zeros((2 * tq, LANES), F32)
            for j in range(lk // kb):
                rows = pl.ds(j * kb, kb)
                s = _dot_nt(q2, kvar[var, rows, :])
                m_new = jnp.maximum(m, jnp.max(s, axis=-1, keepdims=True))
                alpha = jnp.exp2(m - m_new)
                e = jnp.exp2(s - m_new)
                l = alpha * l + jnp.sum(e, axis=-1, keepdims=True)
                acc = alpha * acc + _dot(e.astype(BF16), vvar[var, rows, :])
                m = m_new
            o = acc * (1.0 / l)
            out = o if out is None else out + o
        o_ref[:, (2 * half) * LANES:(2 * half + 1) * LANES] = out[:tq]
        o_ref[:, (2 * half + 1) * LANES:(2 * half + 2) * LANES] = out[tq:]


def _latent_attention(p, kx, vx):
    tq, kb = 256, 512
    nq = DEC_SEQ // tq
    ngr = ATT_KV_HEADS // 2
    kcb = ATT_QW // LANES
    rb_q = T_PROMPT // tq
    rb_k = T_PROMPT // DEC_SEQ
    lk = DEC_SEQ + PAST_LEN
    ctx = pl.BlockSpec((None, PAST_LEN, LANES), lambda b, g, i: (b, 0, g))
    return pl.pallas_call(
        functools.partial(_latent_attn_kernel, tq=tq, kb=kb),
        grid=(DEC_BATCH, ngr, nq),
        in_specs=[pl.BlockSpec((tq, 4 * LANES), lambda b, g, i: (rb_q + b * nq + i, g)),
                  pl.BlockSpec((DEC_SEQ, LANES), lambda b, g, i: (rb_k + b, kcb + g)),
                  pl.BlockSpec((DEC_SEQ, LANES), lambda b, g, i: (rb_k + b, kcb + ngr + g)),
                  ctx, ctx],
        out_specs=pl.BlockSpec((tq, 4 * LANES), lambda b, g, i: (b * nq + i, g)),
        out_shape=jax.ShapeDtypeStruct((T_SAMPLE, ATT_QW), F32),
        scratch_shapes=[pltpu.VMEM((4, lk, LANES), BF16), pltpu.VMEM((4, lk, LANES), BF16)],
        compiler_params=_cparams(("arbitrary", "arbitrary", "arbitrary")),
        name="latent_attention",
    )(p, p, p, kx, vx)


def _window_attn_kernel(q_ref, kp_ref, kc_ref, kn_ref, vp_ref, vc_ref, vn_ref, kx_ref, vx_ref,
                        sink_ref, o_ref):
    n = pl.program_id(1)
    k = jnp.concatenate([kp_ref[...], kc_ref[...], kn_ref[...], kx_ref[...]], axis=0)
    v = jnp.concatenate([vp_ref[...], vc_ref[...], vn_ref[...], vx_ref[...]], axis=0)
    shape = (BLK, 3 * BLK + PAST_LEN)
    i = lax.broadcasted_iota(jnp.int32, shape, 0)
    r = lax.broadcasted_iota(jnp.int32, shape, 1)
    kpos = (n - 1) * BLK + r
    local = (jnp.abs(r - BLK - i) <= WINDOW) & (kpos >= 0) & (kpos < DEC_SEQ)
    mask = local | (r >= 3 * BLK)
    _attn_core(q_ref[...], k, v, sink_ref, o_ref, mask)


def _window_attention(p, kx, vx, sink):
    nb = DEC_SEQ // BLK
    rb0 = T_PROMPT // BLK
    qcb = (4 * RET_QW) // (4 * LANES)
    kcb = (4 * RET_QW + WIN_QW) // LANES
    vcb = kcb + 1

    def kv(cb, off):
        return pl.BlockSpec((BLK, LANES),
                            lambda b, n: (rb0 + b * nb + jnp.clip(n + off, 0, nb - 1), cb))

    ctx = pl.BlockSpec((None, PAST_LEN, LANES), lambda b, n: (b, 0, 0))
    return pl.pallas_call(
        _window_attn_kernel,
        grid=(DEC_BATCH, nb),
        in_specs=[pl.BlockSpec((BLK, 4 * LANES), lambda b, n: (rb0 + b * nb + n, qcb)),
                  kv(kcb, -1), kv(kcb, 0), kv(kcb, 1), kv(vcb, -1), kv(vcb, 0), kv(vcb, 1),
                  ctx, ctx,
                  pl.BlockSpec((4, 2, LANES), lambda b, n: (0, 0, 0))],
        out_specs=pl.BlockSpec((BLK, 4 * LANES), lambda b, n: (b * nb + n, 0)),
        out_shape=jax.ShapeDtypeStruct((T_SAMPLE, WIN_QW), F32),
        compiler_params=_cparams(("arbitrary", "arbitrary")),
        name="window_attention",
    )(p, p, p, p, p, p, p, kx, vx, sink)


def _sink_table(sink, n_groups):
    s = sink.astype(F32).reshape(n_groups, 4, 2, 1)
    return jnp.broadcast_to(s, (n_groups, 4, 2, LANES))


def _cache_rows(cache):
    b, h, l, d = cache.shape
    return cache.astype(F32).transpose(0, 2, 1, 3).reshape(b, l, h * d)


def _outproj_kernel(*refs, n_in):
    a_refs = refs[:2 * n_in]
    w_refs = refs[2 * n_in:3 * n_in]
    xp_ref, xs_ref, mod_ref, g_ref, wr_ref, xn_ref, hf_ref, lg_ref = refs[3 * n_in:]
    y = None
    for k, w_ref in enumerate(w_refs):
        a = _pick(a_refs[2 * k], a_refs[2 * k + 1])
        t = _dot(a.astype(BF16), w_ref[...])
        y = t if y is None else y + t
    xn = _pick(xp_ref, xs_ref) + mod_ref[2:3, :] * _rms(y, g_ref[1:2, :])
    hf = _rms(xn, g_ref[2:3, :]) * (1.0 + mod_ref[4:5, :]) + mod_ref[3:4, :]
    xn_ref[...] = xn
    hf_ref[...] = hf
    lg_ref[...] = _dot_nt(wr_ref[...], hf.astype(BF16))


def _outproj(parts, w_parts, x, mod, layer, gains, w_router_t):
    n_in = len(parts)
    in_specs = []
    args = []
    for a_p, a_s in parts:
        in_specs += _pair_specs(a_p.shape[1])
        args += [a_p, a_s]
    in_specs += [pl.BlockSpec(w.shape, lambda i: (0, 0)) for w in w_parts]
    in_specs += _pair_specs(D_MODEL) + [
        pl.BlockSpec((None, None, 6, D_MODEL), lambda i: (layer, _mod_index(i), 0, 0)),
        pl.BlockSpec((4, D_MODEL), lambda i: (0, 0)),
        pl.BlockSpec((N_EXPERTS, D_MODEL), lambda i: (0, 0)),
    ]
    row = pl.BlockSpec((TM, D_MODEL), lambda i: (i, 0))
    return pl.pallas_call(
        functools.partial(_outproj_kernel, n_in=n_in),
        grid=(T_ALL // TM,),
        in_specs=in_specs,
        out_specs=[row, row, pl.BlockSpec((N_EXPERTS, TM), lambda i: (0, i))],
        out_shape=[jax.ShapeDtypeStruct((T_ALL, D_MODEL), F32),
                   jax.ShapeDtypeStruct((T_ALL, D_MODEL), F32),
                   jax.ShapeDtypeStruct((N_EXPERTS, T_ALL), F32)],
        compiler_params=_cparams(("arbitrary",)),
        name="outproj",
    )(*args, *w_parts, x[0], x[1], mod, gains, w_router_t)


def _gather_kernel(idx_ref, h_ref, o_ref, buf, *, eg, cap):
    b = pl.program_id(0)
    g = pl.program_id(1)
    for e in range(eg):
        base = (g * eg + e) * cap

        def body(c, carry):
            t = idx_ref[b, base + c]
            buf[pl.ds(c, 1), :] = h_ref[pl.ds(t, 1), :]
            return carry

        lax.fori_loop(0, cap, body, 0, unroll=8)
        o_ref[e] = buf[...].astype(BF16)


def _gather(h, idx, n_seq, seq_len, cap, row0, eg):
    rb0 = row0 // seq_len
    return pl.pallas_call(
        functools.partial(_gather_kernel, eg=eg, cap=cap),
        grid_spec=pltpu.PrefetchScalarGridSpec(
            num_scalar_prefetch=1,
            grid=(n_seq, N_EXPERTS // eg),
            in_specs=[pl.BlockSpec((seq_len, D_MODEL), lambda b, g, idx: (rb0 + b, 0))],
            out_specs=pl.BlockSpec((eg, cap, D_MODEL), lambda b, g, idx: (g, b, 0)),
            scratch_shapes=[pltpu.VMEM((cap, D_MODEL), F32)],
        ),
        out_shape=jax.ShapeDtypeStruct((N_EXPERTS, n_seq * cap, D_MODEL), BF16),
        compiler_params=_cparams(("arbitrary", "arbitrary")),
        name="moe_gather",
    )(idx, h)


def _moe_kernel(xp_ref, xs_ref, wg_ref, wu_ref, wd_ref, gwp_ref, gws_ref, yp_ref, ys_ref):
    f = pl.program_id(1)
    wg = wg_ref[...].astype(BF16)
    wu = wu_ref[...].astype(BF16)
    wd = wd_ref[...].astype(BF16)
    for x_ref, gw_ref, y_ref in ((xp_ref, gwp_ref, yp_ref), (xs_ref, gws_ref, ys_ref)):
        x = x_ref[...]
        hid = (_silu(_dot(x, wg)) * _dot(x, wu)).astype(BF16)
        part = _dot(hid, wd)

        @pl.when(f == 0)
        def _():
            y_ref[...] = part

        @pl.when(f > 0)
        def _():
            y_ref[...] += part

        @pl.when(f == pl.num_programs(1) - 1)
        def _():
            y_ref[...] = y_ref[...] * gw_ref[...]


def _moe_ffn(xg_p, xg_s, w_gate, w_up, w_down, layer, gw_p, gw_s):
    nf = EXPERT_FF // TF
    xspec_p = pl.BlockSpec((None, ROWS_P, D_MODEL), lambda e, f: (e, 0, 0))
    xspec_s = pl.BlockSpec((None, ROWS_S, D_MODEL), lambda e, f: (e, 0, 0))
    return pl.pallas_call(
        _moe_kernel,
        grid=(N_EXPERTS, nf),
        in_specs=[xspec_p, xspec_s,
                  pl.BlockSpec((None, None, D_MODEL, TF), lambda e, f: (layer, e, 0, f)),
                  pl.BlockSpec((None, None, D_MODEL, TF), lambda e, f: (layer, e, 0, f)),
                  pl.BlockSpec((None, None, TF, D_MODEL), lambda e, f: (layer, e, f, 0)),
                  pl.BlockSpec((None, ROWS_P, 1), lambda e, f: (e, 0, 0)),
                  pl.BlockSpec((None, ROWS_S, 1), lambda e, f: (e, 0, 0))],
        out_specs=[xspec_p, xspec_s],
        out_shape=[jax.ShapeDtypeStruct((N_EXPERTS, ROWS_P, D_MODEL), F32),
                   jax.ShapeDtypeStruct((N_EXPERTS, ROWS_S, D_MODEL), F32)],
        compiler_params=_cparams(("arbitrary", "arbitrary")),
        name="moe_ffn",
    )(xg_p, xg_s, w_gate, w_up, w_down, gw_p, gw_s)


def _combine_kernel(idx_ref, y_ref, o_ref, *, eg, cap):
    b = pl.program_id(0)
    g = pl.program_id(1)

    @pl.when(g == 0)
    def _():
        o_ref[...] = jnp.zeros_like(o_ref)

    for e in range(eg):
        base = (g * eg + e) * cap

        def body(c, carry):
            t = idx_ref[b, base + c]
            o_ref[pl.ds(t, 1), :] = o_ref[pl.ds(t, 1), :] + y_ref[e, pl.ds(c, 1), :]
            return carry

        lax.fori_loop(0, cap, body, 0, unroll=4)


def _combine(y, idx, n_seq, seq_len, cap, eg):
    return pl.pallas_call(
        functools.partial(_combine_kernel, eg=eg, cap=cap),
        grid_spec=pltpu.PrefetchScalarGridSpec(
            num_scalar_prefetch=1,
            grid=(n_seq, N_EXPERTS // eg),
            in_specs=[pl.BlockSpec((eg, cap, D_MODEL), lambda b, g, idx: (g, b, 0))],
            out_specs=pl.BlockSpec((seq_len, D_MODEL), lambda b, g, idx: (b, 0)),
        ),
        out_shape=jax.ShapeDtypeStruct((n_seq * seq_len, D_MODEL), F32),
        compiler_params=_cparams(("arbitrary", "arbitrary")),
        name="moe_combine",
    )(idx, y)


def _postnorm_kernel(x_ref, yp_ref, ys_ref, mod_ref, g_ref, op_ref, os_ref):
    i = pl.program_id(0)
    o = x_ref[...] + mod_ref[5:6, :] * _rms(_pick(yp_ref, ys_ref), g_ref[3:4, :])

    @pl.when(i < NPT)
    def _():
        op_ref[...] = o

    @pl.when(i >= NPT)
    def _():
        os_ref[...] = o


def _postnorm(x, y_p, y_s, mod, layer, gains):
    return pl.pallas_call(
        _postnorm_kernel,
        grid=(T_ALL // TM,),
        in_specs=[pl.BlockSpec((TM, D_MODEL), lambda i: (i, 0))] + _pair_specs(D_MODEL) + [
            pl.BlockSpec((None, None, 6, D_MODEL), lambda i: (layer, _mod_index(i), 0, 0)),
            pl.BlockSpec((4, D_MODEL), lambda i: (0, 0))],
        out_specs=_pair_specs(D_MODEL),
        out_shape=[jax.ShapeDtypeStruct((T_PROMPT, D_MODEL), F32),
                   jax.ShapeDtypeStruct((T_SAMPLE, D_MODEL), F32)],
        compiler_params=_cparams(("arbitrary",)),
        name="postnorm",
    )(x, y_p, y_s, mod, gains)


def _excl_prefix(mask):
    r, n = mask.shape
    i0 = lax.broadcasted_iota(jnp.int32, (LANES, LANES), 0)
    i1 = lax.broadcasted_iota(jnp.int32, (LANES, LANES), 1)
    upper = jnp.where(i0 < i1, 1.0, 0.0).astype(BF16)
    ones = jnp.where(mask, 1.0, 0.0)
    carry = jnp.zeros((r, 1), F32)
    out = []
    for c in range(n // LANES):
        ch = ones[:, c * LANES:(c + 1) * LANES]
        out.append(_dot(ch.astype(BF16), upper) + carry)
        carry = carry + jnp.sum(ch, axis=1, keepdims=True)
    return jnp.concatenate(out, axis=1)


def _route_kernel(lg_ref, idx_ref, gw_ref, *, nb, n, cap):
    affs = []
    for s in range(nb):
        l = lg_ref[:, s * n:(s + 1) * n]
        e = jnp.exp(l - jnp.max(l, axis=0, keepdims=True))
        affs.append(e / jnp.sum(e, axis=0, keepdims=True))
    aff = affs[0] if nb == 1 else jnp.concatenate(affs, axis=0)
    rows = nb * N_EXPERTS
    bits = jnp.zeros((rows, 1), jnp.int32)
    for bit in range(30, -1, -1):
        cand = bits | (1 << bit)
        cnt = jnp.sum(jnp.where(aff >= pltpu.bitcast(cand, F32), 1.0, 0.0), axis=1, keepdims=True)
        bits = jnp.where(cnt >= cap, cand, bits)
    thr = pltpu.bitcast(bits, F32)
    gt = aff > thr
    eq = aff == thr
    need = cap - jnp.sum(jnp.where(gt, 1.0, 0.0), axis=1, keepdims=True)
    sel = gt | (eq & (_excl_prefix(eq) < need))
    pos = jnp.where(sel, _excl_prefix(sel), -1.0)
    tok = lax.broadcasted_iota(jnp.int32, (rows, n), 1).astype(F32)
    capw = max(cap, LANES)
    slot_lane = lax.broadcasted_iota(jnp.int32, (rows, capw), 1)

    def slot(s, carry):
        idx_acc, gw_acc = carry
        hit = pos == s.astype(F32)
        ic = jnp.sum(jnp.where(hit, tok, 0.0), axis=1, keepdims=True)
        gc = jnp.sum(jnp.where(hit, aff, 0.0), axis=1, keepdims=True)
        here = slot_lane == s
        return jnp.where(here, ic, idx_acc), jnp.where(here, gc, gw_acc)

    zero = jnp.zeros((rows, capw), F32)
    idx_acc, gw_acc = lax.fori_loop(0, cap, slot, (zero, zero))
    for s in range(nb):
        r = slice(s * N_EXPERTS, (s + 1) * N_EXPERTS)
        idx_ref[s] = idx_acc[r, :cap].astype(jnp.int32)
        gw_ref[s] = gw_acc[r, :cap]


def _route(logits_t, n_seq, seq_len, cap, col0, nb):
    cb0 = col0 // (nb * seq_len)
    out = pl.BlockSpec((nb, N_EXPERTS, cap), lambda b: (b, 0, 0))
    return pl.pallas_call(
        functools.partial(_route_kernel, nb=nb, n=seq_len, cap=cap),
        grid=(n_seq // nb,),
        in_specs=[pl.BlockSpec((N_EXPERTS, nb * seq_len), lambda b: (0, cb0 + b))],
        out_specs=[out, out],
        out_shape=[jax.ShapeDtypeStruct((n_seq, N_EXPERTS, cap), jnp.int32),
                   jax.ShapeDtypeStruct((n_seq, N_EXPERTS, cap), F32)],
        compiler_params=_cparams(("arbitrary",)),
        name="route",
    )(logits_t)


def _expert_ffn(hf, logits_t, w_gate, w_up, w_down, layer):
    idx_p, gw_p = _route(logits_t, BATCH, SEQ, CAP_P, 0, 8)
    idx_s, gw_s = _route(logits_t, DEC_BATCH, DEC_SEQ, CAP_S, T_PROMPT, 1)
    idx_p = idx_p.reshape(BATCH, N_EXPERTS * CAP_P)
    idx_s = idx_s.reshape(DEC_BATCH, N_EXPERTS * CAP_S)
    xg_p = _gather(hf, idx_p, BATCH, SEQ, CAP_P, 0, N_EXPERTS)
    xg_s = _gather(hf, idx_s, DEC_BATCH, DEC_SEQ, CAP_S, T_PROMPT, 4)
    gw_p = gw_p.transpose(1, 0, 2).reshape(N_EXPERTS, ROWS_P, 1)
    gw_s = gw_s.transpose(1, 0, 2).reshape(N_EXPERTS, ROWS_S, 1)
    y_p, y_s = _moe_ffn(xg_p, xg_s, w_gate, w_up, w_down, layer, gw_p, gw_s)
    o_p = _combine(y_p, idx_p, BATCH, SEQ, CAP_P, N_EXPERTS)
    o_s = _combine(y_s, idx_s, DEC_BATCH, DEC_SEQ, CAP_S, 2)
    return o_p, o_s


def _rope_tables():
    t = jnp.arange(DEC_SEQ)
    row = (t // GRID_W).astype(F32)
    col = (t % GRID_W).astype(F32)
    inv = ROPE_THETA ** (-jnp.arange(0, AXIS_DIM, 2, dtype=F32) / AXIS_DIM)
    ar = row[:, None] * inv[None]
    ac = col[:, None] * inv[None]
    ang = jnp.concatenate([ar, ar, ac, ac], axis=-1)
    ang = jnp.concatenate([ang, ang], axis=-1)
    cos = jnp.concatenate([jnp.ones((TM, LANES), F32), jnp.cos(ang)], axis=0)
    sin = jnp.concatenate([jnp.zeros((TM, LANES), F32), jnp.sin(ang)], axis=0)
    return cos, sin


def _split_kv_heads(rows, n_heads):
    return rows.reshape(BATCH, SEQ, n_heads, HEAD_DIM).transpose(0, 2, 1, 3)[:, None]


def kernel(x_prompt, x_sample, state_ret_fwd, state_ret_bwd, cache_win_k, cache_win_v, cache_attn_k, cache_attn_v, c, c_ctx, w_ada, b_ada, norm_gains, w_in_even, w_out_even, ret_decay_fwd, ret_decay_bwd, win_sink, w_in_odd, w_out_odd, q_norm, k_norm, w_router, w_gate, w_up, w_down):
    x = (x_prompt.reshape(T_PROMPT, D_MODEL), x_sample.reshape(T_SAMPLE, D_MODEL))
    cond =jnp.concatenate([c_ctx[None, :], c, jnp.zeros((8 - 1 - DEC_BATCH, D_MODEL), F32)], axis=0)
    mod = _modulation(cond, w_ada, b_ada).reshape(DEPTH, 8, 6, D_MODEL)
    cos_t, sin_t = _rope_tables()
    outs = {}
    for layer in range(DEPTH):
        j = layer // 2
        gains = norm_gains[layer]
        if layer % 2 == 0:
            p = _inproj(x, mod, layer, gains[0], w_in_even[j].astype(BF16), cos_t, sin_t)
            tabs = _retention_tables(ret_decay_fwd[j], ret_decay_bwd[j])
            zero = jnp.zeros((BATCH, RET_HEADS // 2, LANES, LANES), F32)
            ret_p, sf, sb = _retention(p, tabs, zero, zero, BATCH, SEQ, 0)
            ret_s, _, _ = _retention(p, tabs, _pair_states(state_ret_fwd[:, j]),
                                     _pair_states(state_ret_bwd[:, j]), DEC_BATCH, DEC_SEQ, T_PROMPT)
            sink = _sink_table(win_sink[j], 1)
            qcb = (4 * RET_QW) // (4 * LANES)
            kcb = (4 * RET_QW + WIN_QW) // LANES
            win_p = _full_attention(p, 0, qcb, p, 0, kcb, p, 0, kcb + 1,
                                    BATCH, SEQ, SEQ, SEQ, 1, sink=sink)
            win_s = _window_attention(p, _cache_rows(cache_win_k[:, j]), _cache_rows(cache_win_v[:, j]),
                                      sink[0])
            w_out = w_out_even[j].astype(BF16)
            parts = [(ret_p, ret_s), (win_p, win_s)]
            w_parts = [w_out[:RET_QW], w_out[RET_QW:]]
            outs["ret_f"] = _unpair_states(sf)[:, None]
            outs["ret_b"] = _unpair_states(sb)[:, None]
            c0 = 4 * RET_QW + WIN_QW
            outs["win_k"] = _split_kv_heads(p[:T_PROMPT, c0:c0 + WIN_KW], WIN_KV_HEADS)
            outs["win_v"] = _split_kv_heads(p[:T_PROMPT, c0 + WIN_KW:c0 + 2 * WIN_KW], WIN_KV_HEADS)
        else:
            p = _inproj(x, mod, layer, gains[0], w_in_odd[j].astype(BF16), cos_t, sin_t,
                        qn=q_norm[j], kn=k_norm[j])
            ngr = ATT_KV_HEADS // 2
            kcb = ATT_QW // LANES
            att_p = _full_attention(p, 0, 0, p, 0, kcb, p, 0, kcb + ngr,
                                    BATCH, SEQ, SEQ, SEQ, ngr)
            att_s = _latent_attention(p, _cache_rows(cache_attn_k[:, j]), _cache_rows(cache_attn_v[:, j]))
            parts = [(att_p, att_s)]
            w_parts = [w_out_odd[j].astype(BF16)]
            outs["att_k"] = _split_kv_heads(p[:T_PROMPT, ATT_QW:ATT_QW + ATT_KW], ATT_KV_HEADS)
            outs["att_v"] = _split_kv_heads(p[:T_PROMPT, ATT_QW + ATT_KW:], ATT_KV_HEADS)
        xn, hf, logits_t = _outproj(parts, w_parts, x, mod, layer, gains,
                                    w_router[layer].T.astype(BF16))
        o_p, o_s = _expert_ffn(hf, logits_t, w_gate, w_up, w_down, layer)
        x = _postnorm(xn, o_p, o_s, mod, layer, gains)
    y_prompt = x[0].reshape(BATCH, SEQ, D_MODEL)
    y_sample = x[1].reshape(DEC_BATCH, DEC_SEQ, D_MODEL)
    return (y_prompt, y_sample, outs["ret_f"], outs["ret_b"], outs["win_k"], outs["win_v"],
            outs["att_k"], outs["att_v"])
```

```python
import functools

import jax
import jax.numpy as jnp
from jax import lax
from jax.experimental import pallas as pl
from jax.experimental.pallas import tpu as pltpu

D_MODEL = 1024
BATCH = 32
SEQ = 256
DEPTH = 2
DEC_BATCH = 2
DEC_SEQ = 4096
PAST_LEN = 512
GRID_W = 64
HEAD_DIM = 64
AXIS_DIM = HEAD_DIM // 2
ROPE_THETA = 10000.0
BLK = 128
WINDOW = 128
RET_HEADS = D_MODEL // 128
WIN_HEADS = D_MODEL // 128
WIN_KV_HEADS = WIN_HEADS // 4
ATT_HEADS = D_MODEL // HEAD_DIM
ATT_KV_HEADS = ATT_HEADS // 4
RET_QW = RET_HEADS * HEAD_DIM
WIN_QW = WIN_HEADS * HEAD_DIM
WIN_KW = WIN_KV_HEADS * HEAD_DIM
EVEN_IN = 4 * RET_QW + WIN_QW + 2 * WIN_KW
ATT_QW = ATT_HEADS * HEAD_DIM
ATT_KW = ATT_KV_HEADS * HEAD_DIM
ODD_IN = ATT_QW + 2 * ATT_KW
N_EXPERTS = 16
CAPACITY_FACTOR = 2
EXPERT_FF = ((8 * D_MODEL // 3 + 127) // 128) * 128
EPS = 1e-6
NEG_INF = -1e30
F32 = jnp.float32
BF16 = jnp.bfloat16

LANES = 128
T_PROMPT = BATCH * SEQ
T_SAMPLE = DEC_BATCH * DEC_SEQ
T_ALL = T_PROMPT + T_SAMPLE
TM = 512
CAP_P = CAPACITY_FACTOR * SEQ // N_EXPERTS
CAP_S = CAPACITY_FACTOR * DEC_SEQ // N_EXPERTS
ROWS_P = BATCH * CAP_P
ROWS_S = DEC_BATCH * CAP_S
TF = 256
VMEM_LIMIT = 56 * 1024 * 1024
QK_SCALE = HEAD_DIM ** -0.5


def _cparams(sem):
    return pltpu.CompilerParams(dimension_semantics=sem, vmem_limit_bytes=VMEM_LIMIT)


def _silu(x):
    return x * (1.0 / (1.0 + jnp.exp(-x)))


def _dot(a, b):
    return jnp.dot(a, b, preferred_element_type=F32)


def _dot_nt(a, b):
    return lax.dot_general(a, b, (((1,), (1,)), ((), ())), preferred_element_type=F32)


def _rms(x, g):
    return x * lax.rsqrt(jnp.mean(x * x, axis=-1, keepdims=True) + EPS) * g


def _pair_block_diag(shape):
    r = lax.broadcasted_iota(jnp.int32, shape, 0)
    c = lax.broadcasted_iota(jnp.int32, shape, 1)
    return (r // HEAD_DIM) == (c // HEAD_DIM)


def _head_sum(x, bd):
    hi = x.astype(BF16)
    lo = (x - hi.astype(F32)).astype(BF16)
    return _dot(hi, bd) + _dot(lo, bd)


def _rope(xb, cos, sin):
    lane = lax.broadcasted_iota(jnp.int32, xb.shape, 1)
    half = AXIS_DIM // 2
    rot = jnp.where((lane % AXIS_DIM) < half,
                    -pltpu.roll(xb, LANES - half, 1), pltpu.roll(xb, half, 1))
    return xb * cos + rot * sin


def _mod_kernel(c_ref, w_ref, b_ref, o_ref):
    o_ref[...] = _dot(_silu(c_ref[...]), w_ref[...]) + b_ref[...]


def _modulation(cond, w_ada, b_ada):
    tn = 1536
    n = 6 * D_MODEL
    return pl.pallas_call(
        _mod_kernel,
        grid=(DEPTH, n // tn),
        in_specs=[
            pl.BlockSpec((8, D_MODEL), lambda l, j: (0, 0)),
            pl.BlockSpec((None, D_MODEL, tn), lambda l, j: (l, 0, j)),
            pl.BlockSpec((None, 1, tn), lambda l, j: (l, 0, j)),
        ],
        out_specs=pl.BlockSpec((None, 8, tn), lambda l, j: (l, 0, j)),
        out_shape=jax.ShapeDtypeStruct((DEPTH, 8, n), F32),
        compiler_params=_cparams(("arbitrary", "arbitrary")),
        name="modulation",
    )(cond, w_ada, b_ada.reshape(DEPTH, 1, n))


def _mod_index(i):
    npt = T_PROMPT // TM
    return jnp.where(i < npt, 0, 1 + (i - npt) // (DEC_SEQ // TM))


def _rope_index(i):
    npt = T_PROMPT // TM
    return jnp.where(i < npt, 0, 1 + (i - npt) % (DEC_SEQ // TM))


NPT = T_PROMPT // TM


def _pair_specs(width):
    return [pl.BlockSpec((TM, width), lambda i: (jnp.minimum(i, NPT - 1), 0)),
            pl.BlockSpec((TM, width), lambda i: (jnp.maximum(i - NPT, 0), 0))]


def _pick(p_ref, s_ref):
    return jnp.where(pl.program_id(0) < NPT, p_ref[...], s_ref[...])


def _inproj_kernel(xp_ref, xs_ref, mod_ref, g_ref, w_ref, cos_ref, sin_ref, *rest, even):
    if even:
        (o_ref,) = rest
    else:
        qn_ref, kn_ref, o_ref = rest
    h = _rms(_pick(xp_ref, xs_ref), g_ref[...]) * (1.0 + mod_ref[1:2, :]) + mod_ref[0:1, :]
    p = _dot(h.astype(BF16), w_ref[...])
    cos = cos_ref[...]
    sin = sin_ref[...]
    nblk = p.shape[1] // LANES
    if even:
        nq = RET_QW // LANES
        rope_blocks = set(range(0, 2 * nq)) | set(range(4 * nq, 5 * nq + 1))
        scaled = set(range(nq, 2 * nq)) | set(range(4 * nq, 5 * nq))
        normed = {}
    else:
        nq = ATT_QW // LANES
        nk = ATT_KW // LANES
        rope_blocks = set(range(0, nq + nk))
        scaled = set(range(0, nq))
        normed = {b: (qn_ref if b < nq else kn_ref) for b in range(nq + nk)}
        bd = _pair_block_diag((LANES, LANES)).astype(BF16)
    for b in range(nblk):
        blk = p[:, b * LANES:(b + 1) * LANES]
        if b in normed:
            ms = _head_sum(blk * blk, bd) * (1.0 / HEAD_DIM)
            blk = blk * lax.rsqrt(ms + EPS) * normed[b][...]
        if b in rope_blocks:
            blk = _rope(blk, cos, sin)
        if b in scaled:
            blk = blk * QK_SCALE
        o_ref[:, b * LANES:(b + 1) * LANES] = blk


def _inproj(x, mod, layer, gain, w, cos_t, sin_t, qn=None, kn=None):
    even = qn is None
    n = w.shape[1]
    in_specs = _pair_specs(D_MODEL) + [
        pl.BlockSpec((None, None, 6, D_MODEL), lambda i: (layer, _mod_index(i), 0, 0)),
        pl.BlockSpec((1, D_MODEL), lambda i: (0, 0)),
        pl.BlockSpec((D_MODEL, n), lambda i: (0, 0)),
        pl.BlockSpec((TM, LANES), lambda i: (_rope_index(i), 0)),
        pl.BlockSpec((TM, LANES), lambda i: (_rope_index(i), 0)),
    ]
    args = [x[0], x[1], mod, gain.reshape(1, D_MODEL), w, cos_t, sin_t]
    if not even:
        in_specs += [pl.BlockSpec((1, LANES), lambda i: (0, 0))] * 2
        args += [jnp.tile(qn, 2).reshape(1, LANES), jnp.tile(kn, 2).reshape(1, LANES)]
    return pl.pallas_call(
        functools.partial(_inproj_kernel, even=even),
        grid=(T_ALL // TM,),
        in_specs=in_specs,
        out_specs=pl.BlockSpec((TM, n), lambda i: (i, 0)),
        out_shape=jax.ShapeDtypeStruct((T_ALL, n), F32),
        compiler_params=_cparams(("arbitrary",)),
        name="inproj_even" if even else "inproj_odd",
    )(*args)


def _ret_kernel(q_ref, k_ref, v_ref, g_ref, dm_ref, qd_ref, kd_ref, cm_ref, s0f_ref, s0b_ref,
                o_ref, sf_ref, sb_ref, of_scr, ob_scr, *, nc):
    lane = lax.broadcasted_iota(jnp.int32, (BLK, LANES), 1)
    lo = lane < HEAD_DIM
    bd = _pair_block_diag((LANES, LANES))
    bd16 = bd.astype(BF16)

    def chunk(c, s, d):
        r = pl.ds(pl.multiple_of(c * BLK, BLK), BLK)
        qc = q_ref[r, :]
        kc = k_ref[r, :]
        vc = v_ref[r, :]
        a0 = _dot_nt(qc, jnp.where(lo, kc, 0.0)) * dm_ref[d, 0]
        a1 = _dot_nt(qc, jnp.where(lo, 0.0, kc)) * dm_ref[d, 1]
        o = (_dot(a0, jnp.where(lo, vc, 0.0)) + _dot(a1, jnp.where(lo, 0.0, vc))
             + _dot(qc * qd_ref[d], s))
        kv = _dot((kc * kd_ref[d]).T, vc)
        s = s * cm_ref[d] + jnp.where(bd, kv, 0.0)
        return r, o, s

    def body(c, carry):
        s_f, s_b = carry
        r, o, s_f = chunk(c, s_f, 0)
        of_scr[r, :] = o
        r, o, s_b = chunk(nc - 1 - c, s_b, 1)
        ob_scr[r, :] = o
        return s_f, s_b

    s_f, s_b = lax.fori_loop(0, nc, body, (s0f_ref[...], s0b_ref[...]))
    sf_ref[...] = s_f
    sb_ref[...] = s_b

    def readout(c, carry):
        r = pl.ds(pl.multiple_of(c * BLK, BLK), BLK)
        o = of_scr[r, :] + ob_scr[r, :]
        mu = _head_sum(o, bd16) * (1.0 / HEAD_DIM)
        dlt = o - mu
        var = _head_sum(dlt * dlt, bd16) * (1.0 / HEAD_DIM)
        o_ref[r, :] = dlt * lax.rsqrt(var + EPS) * _silu(g_ref[r, :])
        return carry

    lax.fori_loop(0, nc, readout, 0)


def _retention(p, tabs, s0f, s0b, n_seq, seq_len, row0):
    npair = RET_HEADS // 2
    rb0 = row0 // seq_len
    dm, qd, kd, cm = tabs

    def col(c0):
        return pl.BlockSpec((seq_len, LANES), lambda b, j: (rb0 + b, c0 + j))

    tab4 = pl.BlockSpec((None, 2, LANES, LANES), lambda b, j: (j, 0, 0, 0))
    st = pl.BlockSpec((None, None, LANES, LANES), lambda b, j: (b, j, 0, 0))
    return pl.pallas_call(
        functools.partial(_ret_kernel, nc=seq_len // BLK),
        grid=(n_seq, npair),
        in_specs=[col(0), col(npair), col(2 * npair), col(3 * npair),
                  pl.BlockSpec((None, 2, 2, BLK, BLK), lambda b, j: (j, 0, 0, 0, 0)),
                  tab4, tab4, tab4, st, st],
        out_specs=[pl.BlockSpec((seq_len, LANES), lambda b, j: (b, j)), st, st],
        out_shape=[jax.ShapeDtypeStruct((n_seq * seq_len, RET_QW), F32),
                   jax.ShapeDtypeStruct((n_seq, npair, LANES, LANES), F32),
                   jax.ShapeDtypeStruct((n_seq, npair, LANES, LANES), F32)],
        scratch_shapes=[pltpu.VMEM((seq_len, LANES), F32), pltpu.VMEM((seq_len, LANES), F32)],
        compiler_params=_cparams(("arbitrary", "arbitrary")),
        name="retention",
    )(p, p, p, p, dm, qd, kd, cm, s0f, s0b)


def _retention_tables(decay_f, decay_b):
    npair = RET_HEADS // 2
    idx = jnp.arange(BLK, dtype=F32)
    diff = idx[:, None] - idx[None, :]

    def one(decay, backward):
        lg = jax.nn.log_sigmoid(decay.astype(F32))
        dmask = jnp.where(diff >= 0, jnp.exp(lg[:, None, None] * jnp.maximum(diff, 0.0)), 0.0)
        q_dec = jnp.exp(lg[:, None] * (idx + 1.0))
        k_dec = jnp.exp(lg[:, None] * (BLK - 1.0 - idx))
        c_dec = jnp.exp(lg * BLK)
        if backward:
            dmask = jnp.swapaxes(dmask, 1, 2)
            q_dec = q_dec[:, ::-1]
            k_dec = k_dec[:, ::-1]
        return dmask, q_dec, k_dec, c_dec

    def lanes(t):
        t = t.reshape(npair, 2, BLK)
        return jnp.repeat(jnp.swapaxes(t, 1, 2), HEAD_DIM, axis=2)

    parts = [one(decay_f, False), one(decay_b, True)]
    dm = jnp.stack([p[0].reshape(npair, 2, BLK, BLK) for p in parts], axis=1)
    qd = jnp.stack([lanes(p[1]) for p in parts], axis=1)
    kd = jnp.stack([lanes(p[2]) for p in parts], axis=1)
    bd = _pair_block_diag((LANES, LANES))
    cm = jnp.stack([jnp.where(bd[None], jnp.repeat(p[3].reshape(npair, 2), HEAD_DIM, axis=1)[:, :, None], 0.0)
                    for p in parts], axis=1)
    return dm, qd, kd, cm


def _pair_states(s):
    b = s.shape[0]
    s = s.astype(F32).reshape(b, RET_HEADS // 2, 2, HEAD_DIM, HEAD_DIM)
    z = jnp.zeros_like(s[:, :, 0])
    top = jnp.concatenate([s[:, :, 0], z], axis=-1)
    bot = jnp.concatenate([z, s[:, :, 1]], axis=-1)
    return jnp.concatenate([top, bot], axis=-2)


def _unpair_states(s):
    b = s.shape[0]
    h0 = s[:, :, :HEAD_DIM, :HEAD_DIM]
    h1 = s[:, :, HEAD_DIM:, HEAD_DIM:]
    return jnp.stack([h0, h1], axis=2).reshape(b, RET_HEADS, HEAD_DIM, HEAD_DIM)


def _attn_core(q, k, v, sink_ref, o_ref, mask):
    lane = lax.broadcasted_iota(jnp.int32, k.shape, 1)
    for half in (0, 1):
        sel = (lane < HEAD_DIM) if half == 0 else (lane >= HEAD_DIM)
        k_sel = jnp.where(sel, k, 0.0)
        v_sel = jnp.where(sel, v, 0.0)
        k_oth = pltpu.roll(k_sel, HEAD_DIM, 1)
        v_oth = pltpu.roll(v_sel, HEAD_DIM, 1)
        kv = ((k_sel, v_sel), (k_oth, v_oth)) if half == 0 else ((k_oth, v_oth), (k_sel, v_sel))
        for pp in (0, 1):
            pair = 2 * half + pp
            qp = q[:, pair * LANES:(pair + 1) * LANES]
            acc = None
            for hh, (kk, vv) in enumerate(kv):
                s = _dot_nt(qp, kk)
                if mask is not None:
                    s = jnp.where(mask, s, NEG_INF)
                m = jnp.max(s, axis=-1, keepdims=True)
                if sink_ref is not None:
                    snk = sink_ref[pair, hh:hh + 1, 0:1]
                    m = jnp.maximum(m, snk)
                e = jnp.exp(s - m)
                den = jnp.sum(e, axis=-1, keepdims=True)
                if sink_ref is not None:
                    den = den + jnp.exp(snk - m)
                o = _dot(e, vv) * (1.0 / den)
                acc = o if acc is None else acc + o
            o_ref[:, pair * LANES:(pair + 1) * LANES] = acc


def _full_attn_kernel(q_ref, k_ref, v_ref, *rest, has_sink):
    if has_sink:
        sink_ref, o_ref = rest
    else:
        sink_ref, (o_ref,) = None, rest
    _attn_core(q_ref[...], k_ref[...], v_ref[...], sink_ref, o_ref, None)


def _full_attention(q_arr, q_rb0, q_cb0, k_arr, k_rb0, k_cb0, v_arr, v_rb0, v_cb0,
                    n_seq, lq, lk, tq, n_groups, sink=None):
    nq = lq // tq
    qw = 4 * LANES
    in_specs = [
        pl.BlockSpec((tq, qw), lambda b, g, i: (q_rb0 + b * nq + i, q_cb0 + g)),
        pl.BlockSpec((lk, LANES), lambda b, g, i: (k_rb0 + b, k_cb0 + g)),
        pl.BlockSpec((lk, LANES), lambda b, g, i: (v_rb0 + b, v_cb0 + g)),
    ]
    args = [q_arr, k_arr, v_arr]
    if sink is not None:
        in_specs.append(pl.BlockSpec((None, 4, 2, LANES), lambda b, g, i: (g, 0, 0, 0)))
        args.append(sink)
    return pl.pallas_call(
        functools.partial(_full_attn_kernel, has_sink=sink is not None),
        grid=(n_seq, n_groups, nq),
        in_specs=in_specs,
        out_specs=pl.BlockSpec((tq, qw), lambda b, g, i: (b * nq + i, g)),
        out_shape=jax.ShapeDtypeStruct((n_seq * lq, n_groups * qw), F32),
        compiler_params=_cparams(("arbitrary", "arbitrary", "arbitrary")),
        name="full_attention",
    )(*args)


LOG2E = 1.4426950408889634


def _latent_attn_kernel(q_ref, k_ref, v_ref, kx_ref, vx_ref, o_ref, kvar, vvar, *, tq, kb):
    lk = DEC_SEQ + PAST_LEN

    @pl.when(pl.program_id(2) == 0)
    def _():
        for src_k, src_v, r0, nrows in ((k_ref, v_ref, 0, DEC_SEQ), (kx_ref, vx_ref, DEC_SEQ, PAST_LEN)):
            for c in range(nrows // kb):
                src = pl.ds(c * kb, kb)
                dst = pl.ds(r0 + c * kb, kb)
                lo = lax.broadcasted_iota(jnp.int32, (kb, LANES), 1) < HEAD_DIM
                for ref, var in ((src_k, kvar), (src_v, vvar)):
                    t = ref[src, :]
                    a = jnp.where(lo, t, 0.0)
                    b = jnp.where(lo, 0.0, t)
                    var[0, dst, :] = a.astype(BF16)
                    var[1, dst, :] = pltpu.roll(a, HEAD_DIM, 1).astype(BF16)
                    var[2, dst, :] = pltpu.roll(b, HEAD_DIM, 1).astype(BF16)
                    var[3, dst, :] = b.astype(BF16)

    for half in (0, 1):
        q2 = jnp.concatenate([q_ref[:, (2 * half) * LANES:(2 * half + 1) * LANES],
                              q_ref[:, (2 * half + 1) * LANES:(2 * half + 2) * LANES]], axis=0)
        q2 = (q2 * LOG2E).astype(BF16)
        out = None
        for hh in (0, 1):
            var = 2 * half + hh
            m = jnp.full((2 * tq, 1), -jnp.inf, F32)
            l = jnp.zeros((2 * tq, 1), F32)
            acc = jnp.zeros((2 * tq, LANES), F32)
            for j in range(lk // kb):
                rows = pl.ds(j * kb, kb)
                s = _dot_nt(q2, kvar[var, rows, :])
                m_new = jnp.maximum(m, jnp.max(s, axis=-1, keepdims=True))
                alpha = jnp.exp2(m - m_new)
                e = jnp.exp2(s - m_new)
                l = alpha * l + jnp.sum(e, axis=-1, keepdims=True)
                acc = alpha * acc + _dot(e.astype(BF16), vvar[var, rows, :])
                m = m_new
            o = acc * (1.0 / l)
            out = o if out is None else out + o
        o_ref[:, (2 * half) * LANES:(2 * half + 1) * LANES] = out[:tq]
        o_ref[:, (2 * half + 1) * LANES:(2 * half + 2) * LANES] = out[tq:]


def _latent_attention(p, kx, vx):
    tq, kb = 256, 512
    nq = DEC_SEQ // tq
    ngr = ATT_KV_HEADS // 2
    kcb = ATT_QW // LANES
    rb_q = T_PROMPT // tq
    rb_k = T_PROMPT // DEC_SEQ
    lk = DEC_SEQ + PAST_LEN
    ctx = pl.BlockSpec((None, PAST_LEN, LANES), lambda b, g, i: (b, 0, g))
    return pl.pallas_call(
        functools.partial(_latent_attn_kernel, tq=tq, kb=kb),
        grid=(DEC_BATCH, ngr, nq),
        in_specs=[pl.BlockSpec((tq, 4 * LANES), lambda b, g, i: (rb_q + b * nq + i, g)),
                  pl.BlockSpec((DEC_SEQ, LANES), lambda b, g, i: (rb_k + b, kcb + g)),
                  pl.BlockSpec((DEC_SEQ, LANES), lambda b, g, i: (rb_k + b, kcb + ngr + g)),
                  ctx, ctx],
        out_specs=pl.BlockSpec((tq, 4 * LANES), lambda b, g, i: (b * nq + i, g)),
        out_shape=jax.ShapeDtypeStruct((T_SAMPLE, ATT_QW), F32),
        scratch_shapes=[pltpu.VMEM((4, lk, LANES), BF16), pltpu.VMEM((4, lk, LANES), BF16)],
        compiler_params=_cparams(("arbitrary", "arbitrary", "arbitrary")),
        name="latent_attention",
    )(p, p, p, kx, vx)


def _window_attn_kernel(q_ref, kp_ref, kc_ref, kn_ref, vp_ref, vc_ref, vn_ref, kx_ref, vx_ref,
                        sink_ref, o_ref):
    n = pl.program_id(1)
    k = jnp.concatenate([kp_ref[...], kc_ref[...], kn_ref[...], kx_ref[...]], axis=0)
    v = jnp.concatenate([vp_ref[...], vc_ref[...], vn_ref[...], vx_ref[...]], axis=0)
    shape = (BLK, 3 * BLK + PAST_LEN)
    i = lax.broadcasted_iota(jnp.int32, shape, 0)
    r = lax.broadcasted_iota(jnp.int32, shape, 1)
    kpos = (n - 1) * BLK + r
    local = (jnp.abs(r - BLK - i) <= WINDOW) & (kpos >= 0) & (kpos < DEC_SEQ)
    mask = local | (r >= 3 * BLK)
    _attn_core(q_ref[...], k, v, sink_ref, o_ref, mask)


def _window_attention(p, kx, vx, sink):
    nb = DEC_SEQ // BLK
    rb0 = T_PROMPT // BLK
    qcb = (4 * RET_QW) // (4 * LANES)
    kcb = (4 * RET_QW + WIN_QW) // LANES
    vcb = kcb + 1

    def kv(cb, off):
        return pl.BlockSpec((BLK, LANES),
                            lambda b, n: (rb0 + b * nb + jnp.clip(n + off, 0, nb - 1), cb))

    ctx = pl.BlockSpec((None, PAST_LEN, LANES), lambda b, n: (b, 0, 0))
    return pl.pallas_call(
        _window_attn_kernel,
        grid=(DEC_BATCH, nb),
        in_specs=[pl.BlockSpec((BLK, 4 * LANES), lambda b, n: (rb0 + b * nb + n, qcb)),
                  kv(kcb, -1), kv(kcb, 0), kv(kcb, 1), kv(vcb, -1), kv(vcb, 0), kv(vcb, 1),
                  ctx, ctx,
                  pl.BlockSpec((4, 2, LANES), lambda b, n: (0, 0, 0))],
        out_specs=pl.BlockSpec((BLK, 4 * LANES), lambda b, n: (b * nb + n, 0)),
        out_shape=jax.ShapeDtypeStruct((T_SAMPLE, WIN_QW), F32),
        compiler_params=_cparams(("arbitrary", "arbitrary")),
        name="window_attention",
    )(p, p, p, p, p, p, p, kx, vx, sink)


def _sink_table(sink, n_groups):
    s = sink.astype(F32).reshape(n_groups, 4, 2, 1)
    return jnp.broadcast_to(s, (n_groups, 4, 2, LANES))


def _cache_rows(cache):
    b, h, l, d = cache.shape
    return cache.astype(F32).transpose(0, 2, 1, 3).reshape(b, l, h * d)


def _outproj_kernel(*refs, n_in):
    a_refs = refs[:2 * n_in]
    w_refs = refs[2 * n_in:3 * n_in]
    xp_ref, xs_ref, mod_ref, g_ref, wr_ref, xn_ref, hf_ref, lg_ref = refs[3 * n_in:]
    y = None
    for k, w_ref in enumerate(w_refs):
        a = _pick(a_refs[2 * k], a_refs[2 * k + 1])
        t = _dot(a.astype(BF16), w_ref[...])
        y = t if y is None else y + t
    xn = _pick(xp_ref, xs_ref) + mod_ref[2:3, :] * _rms(y, g_ref[1:2, :])
    hf = _rms(xn, g_ref[2:3, :]) * (1.0 + mod_ref[4:5, :]) + mod_ref[3:4, :]
    xn_ref[...] = xn
    hf_ref[...] = hf
    lg_ref[...] = _dot_nt(wr_ref[...], hf.astype(BF16))


def _outproj(parts, w_parts, x, mod, layer, gains, w_router_t):
    n_in = len(parts)
    in_specs = []
    args = []
    for a_p, a_s in parts:
        in_specs += _pair_specs(a_p.shape[1])
        args += [a_p, a_s]
    in_specs += [pl.BlockSpec(w.shape, lambda i: (0, 0)) for w in w_parts]
    in_specs += _pair_specs(D_MODEL) + [
        pl.BlockSpec((None, None, 6, D_MODEL), lambda i: (layer, _mod_index(i), 0, 0)),
        pl.BlockSpec((4, D_MODEL), lambda i: (0, 0)),
        pl.BlockSpec((N_EXPERTS, D_MODEL), lambda i: (0, 0)),
    ]
    row = pl.BlockSpec((TM, D_MODEL), lambda i: (i, 0))
    return pl.pallas_call(
        functools.partial(_outproj_kernel, n_in=n_in),
        grid=(T_ALL // TM,),
        in_specs=in_specs,
        out_specs=[row, row, pl.BlockSpec((N_EXPERTS, TM), lambda i: (0, i))],
        out_shape=[jax.ShapeDtypeStruct((T_ALL, D_MODEL), F32),
                   jax.ShapeDtypeStruct((T_ALL, D_MODEL), F32),
                   jax.ShapeDtypeStruct((N_EXPERTS, T_ALL), F32)],
        compiler_params=_cparams(("arbitrary",)),
        name="outproj",
    )(*args, *w_parts, x[0], x[1], mod, gains, w_router_t)


def _gather_kernel(idx_ref, h_ref, o_ref, buf, *, eg, cap):
    b = pl.program_id(0)
    g = pl.program_id(1)
    for e in range(eg):
        base = (g * eg + e) * cap

        def body(c, carry):
            t = idx_ref[b, base + c]
            buf[pl.ds(c, 1), :] = h_ref[pl.ds(t, 1), :]
            return carry

        lax.fori_loop(0, cap, body, 0, unroll=8)
        o_ref[e] = buf[...].astype(BF16)


def _gather(h, idx, n_seq, seq_len, cap, row0, eg):
    rb0 = row0 // seq_len
    return pl.pallas_call(
        functools.partial(_gather_kernel, eg=eg, cap=cap),
        grid_spec=pltpu.PrefetchScalarGridSpec(
            num_scalar_prefetch=1,
            grid=(n_seq, N_EXPERTS // eg),
            in_specs=[pl.BlockSpec((seq_len, D_MODEL), lambda b, g, idx: (rb0 + b, 0))],
            out_specs=pl.BlockSpec((eg, cap, D_MODEL), lambda b, g, idx: (g, b, 0)),
            scratch_shapes=[pltpu.VMEM((cap, D_MODEL), F32)],
        ),
        out_shape=jax.ShapeDtypeStruct((N_EXPERTS, n_seq * cap, D_MODEL), BF16),
        compiler_params=_cparams(("arbitrary", "arbitrary")),
        name="moe_gather",
    )(idx, h)


def _moe_kernel(xp_ref, xs_ref, wg_ref, wu_ref, wd_ref, gwp_ref, gws_ref, yp_ref, ys_ref, hid_scr, wd_scr):
    f = pl.program_id(1)
    wg = wg_ref[...].astype(BF16)
    wu = wu_ref[...].astype(BF16)
    cols = pl.ds(pl.multiple_of(f * TF, TF), TF)
    wd_scr[cols, :] = wd_ref[...].astype(BF16)
    for h, x_ref in enumerate((xp_ref, xs_ref)):
        x = x_ref[...]
        hid_scr[h, :, cols] = (_silu(_dot(x, wg)) * _dot(x, wu)).astype(BF16)

    @pl.when(f == pl.num_programs(1) - 1)
    def _():
        for h, (gw_ref, y_ref) in enumerate(((gwp_ref, yp_ref), (gws_ref, ys_ref))):
            for r in range(0, hid_scr.shape[1], MOE_ROW_TILE):
                rows = slice(r, r + MOE_ROW_TILE)
                y_ref[rows, :] = _dot(hid_scr[h, rows, :], wd_scr[...]) * gw_ref[rows, :]


MOE_ROW_TILE = 512


def _moe_ffn(xg_p, xg_s, w_gate, w_up, w_down, layer, gw_p, gw_s):
    assert ROWS_P == ROWS_S
    nf = EXPERT_FF // TF
    xspec_p = pl.BlockSpec((None, ROWS_P, D_MODEL), lambda e, f: (e, 0, 0))
    xspec_s = pl.BlockSpec((None, ROWS_S, D_MODEL), lambda e, f: (e, 0, 0))
    return pl.pallas_call(
        _moe_kernel,
        grid=(N_EXPERTS, nf),
        in_specs=[xspec_p, xspec_s,
                  pl.BlockSpec((None, None, D_MODEL, TF), lambda e, f: (layer, e, 0, f)),
                  pl.BlockSpec((None, None, D_MODEL, TF), lambda e, f: (layer, e, 0, f)),
                  pl.BlockSpec((None, None, TF, D_MODEL), lambda e, f: (layer, e, f, 0)),
                  pl.BlockSpec((None, ROWS_P, 1), lambda e, f: (e, 0, 0)),
                  pl.BlockSpec((None, ROWS_S, 1), lambda e, f: (e, 0, 0))],
        out_specs=[xspec_p, xspec_s],
        out_shape=[jax.ShapeDtypeStruct((N_EXPERTS, ROWS_P, D_MODEL), F32),
                   jax.ShapeDtypeStruct((N_EXPERTS, ROWS_S, D_MODEL), F32)],
        scratch_shapes=[pltpu.VMEM((2, ROWS_P, EXPERT_FF), BF16),
                        pltpu.VMEM((EXPERT_FF, D_MODEL), BF16)],
        compiler_params=_cparams(("arbitrary", "arbitrary")),
        name="moe_ffn",
    )(xg_p, xg_s, w_gate, w_up, w_down, gw_p, gw_s)


COMBINE_GROUP = 8


def _combine_kernel(idx_ref, y_ref, o_ref, *, eg, cap):
    b = pl.program_id(0)
    g = pl.program_id(1)

    @pl.when(g == 0)
    def _():
        o_ref[...] = jnp.zeros_like(o_ref)

    for e in range(eg):
        base = (g * eg + e) * cap

        def body(c, carry):
            c0 = c * COMBINE_GROUP
            toks = [idx_ref[b, base + c0 + k] for k in range(COMBINE_GROUP)]
            rows = [o_ref[pl.ds(t, 1), :] + y_ref[e, pl.ds(c0 + k, 1), :] for k, t in enumerate(toks)]
            for t, row in zip(toks, rows):
                o_ref[pl.ds(t, 1), :] = row
            return carry

        lax.fori_loop(0, cap // COMBINE_GROUP, body, 0)


def _combine(y, idx, n_seq, seq_len, cap, eg):
    return pl.pallas_call(
        functools.partial(_combine_kernel, eg=eg, cap=cap),
        grid_spec=pltpu.PrefetchScalarGridSpec(
            num_scalar_prefetch=1,
            grid=(n_seq, N_EXPERTS // eg),
            in_specs=[pl.BlockSpec((eg, cap, D_MODEL), lambda b, g, idx: (g, b, 0))],
            out_specs=pl.BlockSpec((seq_len, D_MODEL), lambda b, g, idx: (b, 0)),
        ),
        out_shape=jax.ShapeDtypeStruct((n_seq * seq_len, D_MODEL), F32),
        compiler_params=_cparams(("arbitrary", "arbitrary")),
        name="moe_combine",
    )(idx, y)


def _onehot_t(idx_ref, n_tok):
    n_slots = idx_ref.shape[-1]
    tok = lax.broadcasted_iota(jnp.int32, (n_tok, n_slots), 0)
    return jnp.where(tok == idx_ref[...], 1.0, 0.0)


def _gather_mm_kernel(idx_ref, h_ref, o_ref, *, cap):
    sel = _onehot_t(idx_ref, h_ref.shape[0]).T.astype(BF16)
    rows = _dot(sel, h_ref[...].astype(BF16)).astype(BF16)
    for e in range(N_EXPERTS):
        o_ref[e] = rows[e * cap:(e + 1) * cap]


def _gather_mm(h, idx, n_seq, seq_len, cap):
    return pl.pallas_call(
        functools.partial(_gather_mm_kernel, cap=cap),
        grid=(n_seq,),
        in_specs=[pl.BlockSpec((None, 1, N_EXPERTS * cap), lambda b: (b, 0, 0)),
                  pl.BlockSpec((seq_len, D_MODEL), lambda b: (b, 0))],
        out_specs=pl.BlockSpec((N_EXPERTS, cap, D_MODEL), lambda b: (0, b, 0)),
        out_shape=jax.ShapeDtypeStruct((N_EXPERTS, n_seq * cap, D_MODEL), BF16),
        compiler_params=_cparams(("arbitrary",)),
        name="moe_gather_mm",
    )(idx, h)


def _combine_mm_kernel(idx_ref, y_ref, o_ref):
    sel = _onehot_t(idx_ref, o_ref.shape[0]).astype(BF16)
    y = jnp.concatenate([y_ref[e] for e in range(N_EXPERTS)], axis=0)
    hi = y.astype(BF16)
    lo = (y - hi.astype(F32)).astype(BF16)
    o_ref[...] = _dot(sel, hi) + _dot(sel, lo)


def _combine_mm(y, idx, n_seq, seq_len, cap):
    return pl.pallas_call(
        _combine_mm_kernel,
        grid=(n_seq,),
        in_specs=[pl.BlockSpec((None, 1, N_EXPERTS * cap), lambda b: (b, 0, 0)),
                  pl.BlockSpec((N_EXPERTS, cap, D_MODEL), lambda b: (0, b, 0))],
        out_specs=pl.BlockSpec((seq_len, D_MODEL), lambda b: (b, 0)),
        out_shape=jax.ShapeDtypeStruct((n_seq * seq_len, D_MODEL), F32),
        compiler_params=_cparams(("arbitrary",)),
        name="moe_combine_mm",
    )(idx, y)


def _postnorm_kernel(x_ref, yp_ref, ys_ref, mod_ref, g_ref, op_ref, os_ref):
    i = pl.program_id(0)
    o = x_ref[...] + mod_ref[5:6, :] * _rms(_pick(yp_ref, ys_ref), g_ref[3:4, :])

    @pl.when(i < NPT)
    def _():
        op_ref[...] = o

    @pl.when(i >= NPT)
    def _():
        os_ref[...] = o


def _postnorm(x, y_p, y_s, mod, layer, gains):
    return pl.pallas_call(
        _postnorm_kernel,
        grid=(T_ALL // TM,),
        in_specs=[pl.BlockSpec((TM, D_MODEL), lambda i: (i, 0))] + _pair_specs(D_MODEL) + [
            pl.BlockSpec((None, None, 6, D_MODEL), lambda i: (layer, _mod_index(i), 0, 0)),
            pl.BlockSpec((4, D_MODEL), lambda i: (0, 0))],
        out_specs=_pair_specs(D_MODEL),
        out_shape=[jax.ShapeDtypeStruct((T_PROMPT, D_MODEL), F32),
                   jax.ShapeDtypeStruct((T_SAMPLE, D_MODEL), F32)],
        compiler_params=_cparams(("arbitrary",)),
        name="postnorm",
    )(x, y_p, y_s, mod, gains)


def _excl_prefix(mask):
    r, n = mask.shape
    i0 = lax.broadcasted_iota(jnp.int32, (LANES, LANES), 0)
    i1 = lax.broadcasted_iota(jnp.int32, (LANES, LANES), 1)
    upper = jnp.where(i0 < i1, 1.0, 0.0).astype(BF16)
    ones = jnp.where(mask, 1.0, 0.0)
    carry = jnp.zeros((r, 1), F32)
    out = []
    for c in range(n // LANES):
        ch = ones[:, c * LANES:(c + 1) * LANES]
        out.append(_dot(ch.astype(BF16), upper) + carry)
        carry = carry + jnp.sum(ch, axis=1, keepdims=True)
    return jnp.concatenate(out, axis=1)


def _route_kernel(lg_ref, idx_ref, gw_ref, *, nb, n, cap):
    affs = []
    for s in range(nb):
        l = lg_ref[:, s * n:(s + 1) * n]
        e = jnp.exp(l - jnp.max(l, axis=0, keepdims=True))
        affs.append(e / jnp.sum(e, axis=0, keepdims=True))
    aff = affs[0] if nb == 1 else jnp.concatenate(affs, axis=0)
    rows = nb * N_EXPERTS
    bits = jnp.zeros((rows, 1), jnp.int32)
    for bit in range(30, -1, -1):
        cand = bits | (1 << bit)
        cnt = jnp.sum(jnp.where(aff >= pltpu.bitcast(cand, F32), 1.0, 0.0), axis=1, keepdims=True)
        bits = jnp.where(cnt >= cap, cand, bits)
    thr = pltpu.bitcast(bits, F32)
    gt = aff > thr
    eq = aff == thr
    need = cap - jnp.sum(jnp.where(gt, 1.0, 0.0), axis=1, keepdims=True)
    sel = gt | (eq & (_excl_prefix(eq) < need))
    pos = jnp.where(sel, _excl_prefix(sel), -1.0)
    tok = lax.broadcasted_iota(jnp.int32, (rows, n), 1).astype(F32)
    capw = max(cap, LANES)
    slot_lane = lax.broadcasted_iota(jnp.int32, (rows, capw), 1)

    def slot(s, carry):
        idx_acc, gw_acc = carry
        hit = pos == lax.convert_element_type(s, F32)
        ic = jnp.sum(jnp.where(hit, tok, 0.0), axis=1, keepdims=True)
        gc = jnp.sum(jnp.where(hit, aff, 0.0), axis=1, keepdims=True)
        here = slot_lane == s
        return jnp.where(here, ic, idx_acc), jnp.where(here, gc, gw_acc)

    zero = jnp.zeros((rows, capw), F32)
    idx_acc, gw_acc = lax.fori_loop(0, cap, slot, (zero, zero), unroll=8)
    for s in range(nb):
        r = slice(s * N_EXPERTS, (s + 1) * N_EXPERTS)
        idx_ref[s] = idx_acc[r, :cap].astype(jnp.int32)
        gw_ref[s] = gw_acc[r, :cap]


def _route(logits_t, n_seq, seq_len, cap, col0, nb):
    cb0 = col0 // (nb * seq_len)
    out = pl.BlockSpec((nb, N_EXPERTS, cap), lambda b: (b, 0, 0))
    return pl.pallas_call(
        functools.partial(_route_kernel, nb=nb, n=seq_len, cap=cap),
        grid=(n_seq // nb,),
        in_specs=[pl.BlockSpec((N_EXPERTS, nb * seq_len), lambda b: (0, cb0 + b))],
        out_specs=[out, out],
        out_shape=[jax.ShapeDtypeStruct((n_seq, N_EXPERTS, cap), jnp.int32),
                   jax.ShapeDtypeStruct((n_seq, N_EXPERTS, cap), F32)],
        compiler_params=_cparams(("arbitrary",)),
        name="route",
    )(logits_t)


def _expert_ffn(hf, logits_t, w_gate, w_up, w_down, layer):
    idx_p, gw_p = _route(logits_t, BATCH, SEQ, CAP_P, 0, 8)
    idx_s, gw_s = _route(logits_t, DEC_BATCH, DEC_SEQ, CAP_S, T_PROMPT, 1)
    idx_p = idx_p.reshape(BATCH, 1, N_EXPERTS * CAP_P)
    idx_s = idx_s.reshape(DEC_BATCH, N_EXPERTS * CAP_S)
    xg_p = _gather_mm(hf, idx_p, BATCH, SEQ, CAP_P)
    xg_s = _gather(hf, idx_s, DEC_BATCH, DEC_SEQ, CAP_S, T_PROMPT, 4)
    gw_p = gw_p.transpose(1, 0, 2).reshape(N_EXPERTS, ROWS_P, 1)
    gw_s = gw_s.transpose(1, 0, 2).reshape(N_EXPERTS, ROWS_S, 1)
    y_p, y_s = _moe_ffn(xg_p, xg_s, w_gate, w_up, w_down, layer, gw_p, gw_s)
    o_p = _combine_mm(y_p, idx_p, BATCH, SEQ, CAP_P)
    o_s = _combine(y_s, idx_s, DEC_BATCH, DEC_SEQ, CAP_S, 2)
    return o_p, o_s


def _rope_tables():
    t = jnp.arange(DEC_SEQ)
    row = (t // GRID_W).astype(F32)
    col = (t % GRID_W).astype(F32)
    inv = ROPE_THETA ** (-jnp.arange(0, AXIS_DIM, 2, dtype=F32) / AXIS_DIM)
    ar = row[:, None] * inv[None]
    ac = col[:, None] * inv[None]
    ang = jnp.concatenate([ar, ar, ac, ac], axis=-1)
    ang = jnp.concatenate([ang, ang], axis=-1)
    cos = jnp.concatenate([jnp.ones((TM, LANES), F32), jnp.cos(ang)], axis=0)
    sin = jnp.concatenate([jnp.zeros((TM, LANES), F32), jnp.sin(ang)], axis=0)
    return cos, sin


def _split_kv_heads(rows, n_heads):
    return rows.reshape(BATCH, SEQ, n_heads, HEAD_DIM).transpose(0, 2, 1, 3)[:, None]


def kernel(x_prompt, x_sample, state_ret_fwd, state_ret_bwd, cache_win_k, cache_win_v, cache_attn_k, cache_attn_v, c, c_ctx, w_ada, b_ada, norm_gains, w_in_even, w_out_even, ret_decay_fwd, ret_decay_bwd, win_sink, w_in_odd, w_out_odd, q_norm, k_norm, w_router, w_gate, w_up, w_down):
    x = (x_prompt.reshape(T_PROMPT, D_MODEL), x_sample.reshape(T_SAMPLE, D_MODEL))
    cond =jnp.concatenate([c_ctx[None, :], c, jnp.zeros((8 - 1 - DEC_BATCH, D_MODEL), F32)], axis=0)
    mod = _modulation(cond, w_ada, b_ada).reshape(DEPTH, 8, 6, D_MODEL)
    cos_t, sin_t = _rope_tables()
    outs = {}
    for layer in range(DEPTH):
        j = layer // 2
        gains = norm_gains[layer]
        if layer % 2 == 0:
            p = _inproj(x, mod, layer, gains[0], w_in_even[j].astype(BF16), cos_t, sin_t)
            tabs = _retention_tables(ret_decay_fwd[j], ret_decay_bwd[j])
            zero = jnp.zeros((BATCH, RET_HEADS // 2, LANES, LANES), F32)
            ret_p, sf, sb = _retention(p, tabs, zero, zero, BATCH, SEQ, 0)
            ret_s, _, _ = _retention(p, tabs, _pair_states(state_ret_fwd[:, j]),
                                     _pair_states(state_ret_bwd[:, j]), DEC_BATCH, DEC_SEQ, T_PROMPT)
            sink = _sink_table(win_sink[j], 1)
            qcb = (4 * RET_QW) // (4 * LANES)
            kcb = (4 * RET_QW + WIN_QW) // LANES
            win_p = _full_attention(p, 0, qcb, p, 0, kcb, p, 0, kcb + 1,
                                    BATCH, SEQ, SEQ, SEQ, 1, sink=sink)
            win_s = _window_attention(p, _cache_rows(cache_win_k[:, j]), _cache_rows(cache_win_v[:, j]),
                                      sink[0])
            w_out = w_out_even[j].astype(BF16)
            parts = [(ret_p, ret_s), (win_p, win_s)]
            w_parts = [w_out[:RET_QW], w_out[RET_QW:]]
            outs["ret_f"] = _unpair_states(sf)[:, None]
            outs["ret_b"] = _unpair_states(sb)[:, None]
            c0 = 4 * RET_QW + WIN_QW
            outs["win_k"] = _split_kv_heads(p[:T_PROMPT, c0:c0 + WIN_KW], WIN_KV_HEADS)
            outs["win_v"] = _split_kv_heads(p[:T_PROMPT, c0 + WIN_KW:c0 + 2 * WIN_KW], WIN_KV_HEADS)
        else:
            p = _inproj(x, mod, layer, gains[0], w_in_odd[j].astype(BF16), cos_t, sin_t,
                        qn=q_norm[j], kn=k_norm[j])
            ngr = ATT_KV_HEADS // 2
            kcb = ATT_QW // LANES
            att_p = _full_attention(p, 0, 0, p, 0, kcb, p, 0, kcb + ngr,
                                    BATCH, SEQ, SEQ, SEQ, ngr)
            att_s = _latent_attention(p, _cache_rows(cache_attn_k[:, j]), _cache_rows(cache_attn_v[:, j]))
            parts = [(att_p, att_s)]
            w_parts = [w_out_odd[j].astype(BF16)]
            outs["att_k"] = _split_kv_heads(p[:T_PROMPT, ATT_QW:ATT_QW + ATT_KW], ATT_KV_HEADS)
            outs["att_v"] = _split_kv_heads(p[:T_PROMPT, ATT_QW + ATT_KW:], ATT_KV_HEADS)
        xn, hf, logits_t = _outproj(parts, w_parts, x, mod, layer, gains,
                                    w_router[layer].T.astype(BF16))
        o_p, o_s = _expert_ffn(hf, logits_t, w_gate, w_up, w_down, layer)
        x = _postnorm(xn, o_p, o_s, mod, layer, gains)
    y_prompt = x[0].reshape(BATCH, SEQ, D_MODEL)
    y_sample = x[1].reshape(DEC_BATCH, DEC_SEQ, D_MODEL)
    return (y_prompt, y_sample, outs["ret_f"], outs["ret_b"], outs["win_k"], outs["win_v"],
            outs["att_k"], outs["att_v"])
```

```python
import functools

import jax
import jax.numpy as jnp
from jax import lax
from jax.experimental import pallas as pl
from jax.experimental.pallas import tpu as pltpu

D_MODEL = 1024
BATCH = 32
SEQ = 256
DEPTH = 2
DEC_BATCH = 2
DEC_SEQ = 4096
PAST_LEN = 512
GRID_W = 64
HEAD_DIM = 64
AXIS_DIM = HEAD_DIM // 2
ROPE_THETA = 10000.0
BLK = 128
WINDOW = 128
RET_HEADS = D_MODEL // 128
WIN_HEADS = D_MODEL // 128
WIN_KV_HEADS = WIN_HEADS // 4
ATT_HEADS = D_MODEL // HEAD_DIM
ATT_KV_HEADS = ATT_HEADS // 4
RET_QW = RET_HEADS * HEAD_DIM
WIN_QW = WIN_HEADS * HEAD_DIM
WIN_KW = WIN_KV_HEADS * HEAD_DIM
EVEN_IN = 4 * RET_QW + WIN_QW + 2 * WIN_KW
ATT_QW = ATT_HEADS * HEAD_DIM
ATT_KW = ATT_KV_HEADS * HEAD_DIM
ODD_IN = ATT_QW + 2 * ATT_KW
N_EXPERTS = 16
CAPACITY_FACTOR = 2
EXPERT_FF = ((8 * D_MODEL // 3 + 127) // 128) * 128
EPS = 1e-6
NEG_INF = -1e30
F32 = jnp.float32
BF16 = jnp.bfloat16

LANES = 128
T_PROMPT = BATCH * SEQ
T_SAMPLE = DEC_BATCH * DEC_SEQ
T_ALL = T_PROMPT + T_SAMPLE
TM = 512
CAP_P = CAPACITY_FACTOR * SEQ // N_EXPERTS
CAP_S = CAPACITY_FACTOR * DEC_SEQ // N_EXPERTS
ROWS_P = BATCH * CAP_P
ROWS_S = DEC_BATCH * CAP_S
TF = 256
VMEM_LIMIT = 56 * 1024 * 1024
QK_SCALE = HEAD_DIM ** -0.5


def _cparams(sem):
    return pltpu.CompilerParams(dimension_semantics=sem, vmem_limit_bytes=VMEM_LIMIT)


def _silu(x):
    return x * (1.0 / (1.0 + jnp.exp(-x)))


def _dot(a, b):
    return jnp.dot(a, b, preferred_element_type=F32)


def _dot_nt(a, b):
    return lax.dot_general(a, b, (((1,), (1,)), ((), ())), preferred_element_type=F32)


def _rms(x, g):
    return x * lax.rsqrt(jnp.mean(x * x, axis=-1, keepdims=True) + EPS) * g


def _pair_block_diag(shape):
    r = lax.broadcasted_iota(jnp.int32, shape, 0)
    c = lax.broadcasted_iota(jnp.int32, shape, 1)
    return (r // HEAD_DIM) == (c // HEAD_DIM)


def _head_sum(x, bd):
    hi = x.astype(BF16)
    lo = (x - hi.astype(F32)).astype(BF16)
    return _dot(hi, bd) + _dot(lo, bd)


def _rope(xb, cos, sin):
    lane = lax.broadcasted_iota(jnp.int32, xb.shape, 1)
    half = AXIS_DIM // 2
    rot = jnp.where((lane % AXIS_DIM) < half,
                    -pltpu.roll(xb, LANES - half, 1), pltpu.roll(xb, half, 1))
    return xb * cos + rot * sin


def _mod_kernel(c_ref, w_ref, b_ref, o_ref):
    o_ref[...] = _dot(_silu(c_ref[...]), w_ref[...]) + b_ref[...]


def _modulation(cond, w_ada, b_ada):
    tn = 1536
    n = 6 * D_MODEL
    return pl.pallas_call(
        _mod_kernel,
        grid=(DEPTH, n // tn),
        in_specs=[
            pl.BlockSpec((8, D_MODEL), lambda l, j: (0, 0)),
            pl.BlockSpec((None, D_MODEL, tn), lambda l, j: (l, 0, j)),
            pl.BlockSpec((None, 1, tn), lambda l, j: (l, 0, j)),
        ],
        out_specs=pl.BlockSpec((None, 8, tn), lambda l, j: (l, 0, j)),
        out_shape=jax.ShapeDtypeStruct((DEPTH, 8, n), F32),
        compiler_params=_cparams(("arbitrary", "arbitrary")),
        name="modulation",
    )(cond, w_ada, b_ada.reshape(DEPTH, 1, n))


def _mod_index(i):
    npt = T_PROMPT // TM
    return jnp.where(i < npt, 0, 1 + (i - npt) // (DEC_SEQ // TM))


def _rope_index(i):
    npt = T_PROMPT // TM
    return jnp.where(i < npt, 0, 1 + (i - npt) % (DEC_SEQ // TM))


NPT = T_PROMPT // TM


def _pair_specs(width):
    return [pl.BlockSpec((TM, width), lambda i: (jnp.minimum(i, NPT - 1), 0)),
            pl.BlockSpec((TM, width), lambda i: (jnp.maximum(i - NPT, 0), 0))]


def _pick(p_ref, s_ref):
    return jnp.where(pl.program_id(0) < NPT, p_ref[...], s_ref[...])


def _inproj_kernel(xp_ref, xs_ref, mod_ref, g_ref, w_ref, cos_ref, sin_ref, *rest, even):
    if even:
        (o_ref,) = rest
    else:
        qn_ref, kn_ref, o_ref = rest
    h = _rms(_pick(xp_ref, xs_ref), g_ref[...]) * (1.0 + mod_ref[1:2, :]) + mod_ref[0:1, :]
    p = _dot(h.astype(BF16), w_ref[...])
    cos = cos_ref[...]
    sin = sin_ref[...]
    nblk = p.shape[1] // LANES
    if even:
        nq = RET_QW // LANES
        rope_blocks = set(range(0, 2 * nq)) | set(range(4 * nq, 5 * nq + 1))
        scaled = set(range(nq, 2 * nq)) | set(range(4 * nq, 5 * nq))
        normed = {}
    else:
        nq = ATT_QW // LANES
        nk = ATT_KW // LANES
        rope_blocks = set(range(0, nq + nk))
        scaled = set(range(0, nq))
        normed = {b: (qn_ref if b < nq else kn_ref) for b in range(nq + nk)}
        bd = _pair_block_diag((LANES, LANES)).astype(BF16)
    for b in range(nblk):
        blk = p[:, b * LANES:(b + 1) * LANES]
        if b in normed:
            ms = _head_sum(blk * blk, bd) * (1.0 / HEAD_DIM)
            blk = blk * lax.rsqrt(ms + EPS) * normed[b][...]
        if b in rope_blocks:
            blk = _rope(blk, cos, sin)
        if b in scaled:
            blk = blk * QK_SCALE
        o_ref[:, b * LANES:(b + 1) * LANES] = blk


def _inproj(x, mod, layer, gain, w, cos_t, sin_t, qn=None, kn=None):
    even = qn is None
    n = w.shape[1]
    in_specs = _pair_specs(D_MODEL) + [
        pl.BlockSpec((None, None, 6, D_MODEL), lambda i: (layer, _mod_index(i), 0, 0)),
        pl.BlockSpec((1, D_MODEL), lambda i: (0, 0)),
        pl.BlockSpec((D_MODEL, n), lambda i: (0, 0)),
        pl.BlockSpec((TM, LANES), lambda i: (_rope_index(i), 0)),
        pl.BlockSpec((TM, LANES), lambda i: (_rope_index(i), 0)),
    ]
    args = [x[0], x[1], mod, gain.reshape(1, D_MODEL), w, cos_t, sin_t]
    if not even:
        in_specs += [pl.BlockSpec((1, LANES), lambda i: (0, 0))] * 2
        args += [jnp.tile(qn, 2).reshape(1, LANES), jnp.tile(kn, 2).reshape(1, LANES)]
    return pl.pallas_call(
        functools.partial(_inproj_kernel, even=even),
        grid=(T_ALL // TM,),
        in_specs=in_specs,
        out_specs=pl.BlockSpec((TM, n), lambda i: (i, 0)),
        out_shape=jax.ShapeDtypeStruct((T_ALL, n), F32),
        compiler_params=_cparams(("arbitrary",)),
        name="inproj_even" if even else "inproj_odd",
    )(*args)


RET_READOUT_ROWS = 512


def _ret_kernel(q_ref, k_ref, v_ref, g_ref, dm_ref, qd_ref, kd_ref, cm_ref, s0f_ref, s0b_ref,
                o_ref, sf_ref, sb_ref, ob_scr, *, nc, npb):
    lane = lax.broadcasted_iota(jnp.int32, (BLK, LANES), 1)
    lo = lane < HEAD_DIM
    bd = _pair_block_diag((LANES, LANES))
    bd16 = bd.astype(BF16)
    sf_ref[...] = s0f_ref[...]
    sb_ref[...] = s0b_ref[...]

    def chunk(c, j, d, s_ref, dst):
        r = pl.ds(pl.multiple_of(c * BLK, BLK), BLK)
        cols = slice(j * LANES, (j + 1) * LANES)
        qc = q_ref[r, cols]
        kc = k_ref[r, cols]
        vc = v_ref[r, cols]
        s = s_ref[j]
        a = _dot_nt(jnp.concatenate([jnp.where(lo, qc, 0.0), jnp.where(lo, 0.0, qc)], axis=0), kc)
        lhs = jnp.concatenate([a[:BLK] * dm_ref[j, d, 0], a[BLK:] * dm_ref[j, d, 1], qc * qd_ref[j, d]], axis=1)
        rhs = jnp.concatenate([jnp.where(lo, vc, 0.0), jnp.where(lo, 0.0, vc), s], axis=0)
        dst[r, cols] = _dot(lhs, rhs)
        kv = _dot((kc * kd_ref[j, d]).T, vc)
        s_ref[j] = s * cm_ref[j, d] + jnp.where(bd, kv, 0.0)

    def body(c, carry):
        for j in range(npb):
            chunk(c, j, 0, sf_ref, o_ref)
            chunk(nc - 1 - c, j, 1, sb_ref, ob_scr)
        return carry

    lax.fori_loop(0, nc, body, 0)

    rt = min(nc * BLK, RET_READOUT_ROWS)

    def readout(c, carry):
        r = pl.ds(pl.multiple_of(c * rt, rt), rt)
        for j in range(npb):
            cols = slice(j * LANES, (j + 1) * LANES)
            o = o_ref[r, cols] + ob_scr[r, cols]
            mu = _head_sum(o, bd16) * (1.0 / HEAD_DIM)
            dlt = o - mu
            var = _head_sum(dlt * dlt, bd16) * (1.0 / HEAD_DIM)
            o_ref[r, cols] = dlt * lax.rsqrt(var + EPS) * _silu(g_ref[r, cols])
        return carry

    lax.fori_loop(0, nc * BLK // rt, readout, 0)


def _retention(p, tabs, s0f, s0b, n_seq, seq_len, row0, npb):
    npair = RET_HEADS // 2
    ng = npair // npb
    rb0 = row0 // seq_len
    dm, qd, kd, cm = tabs
    w = npb * LANES

    def col(c0):
        return pl.BlockSpec((seq_len, w), lambda b, j: (rb0 + b, c0 + j))

    tab4 = pl.BlockSpec((npb, 2, LANES, LANES), lambda b, j: (j, 0, 0, 0))
    st = pl.BlockSpec((None, npb, LANES, LANES), lambda b, j: (b, j, 0, 0))
    return pl.pallas_call(
        functools.partial(_ret_kernel, nc=seq_len // BLK, npb=npb),
        grid=(n_seq, ng),
        in_specs=[col(0), col(ng), col(2 * ng), col(3 * ng),
                  pl.BlockSpec((npb, 2, 2, BLK, BLK), lambda b, j: (j, 0, 0, 0, 0)),
                  tab4, tab4, tab4, st, st],
        out_specs=[pl.BlockSpec((seq_len, w), lambda b, j: (b, j)), st, st],
        out_shape=[jax.ShapeDtypeStruct((n_seq * seq_len, RET_QW), F32),
                   jax.ShapeDtypeStruct((n_seq, npair, LANES, LANES), F32),
                   jax.ShapeDtypeStruct((n_seq, npair, LANES, LANES), F32)],
        scratch_shapes=[pltpu.VMEM((seq_len, w), F32)],
        compiler_params=_cparams(("arbitrary", "arbitrary")),
        name="retention",
    )(p, p, p, p, dm, qd, kd, cm, s0f, s0b)


def _retention_tables(decay_f, decay_b):
    npair = RET_HEADS // 2
    idx = jnp.arange(BLK, dtype=F32)
    diff = idx[:, None] - idx[None, :]

    def one(decay, backward):
        lg = jax.nn.log_sigmoid(decay.astype(F32))
        dmask = jnp.where(diff >= 0, jnp.exp(lg[:, None, None] * jnp.maximum(diff, 0.0)), 0.0)
        q_dec = jnp.exp(lg[:, None] * (idx + 1.0))
        k_dec = jnp.exp(lg[:, None] * (BLK - 1.0 - idx))
        c_dec = jnp.exp(lg * BLK)
        if backward:
            dmask = jnp.swapaxes(dmask, 1, 2)
            q_dec = q_dec[:, ::-1]
            k_dec = k_dec[:, ::-1]
        return dmask, q_dec, k_dec, c_dec

    def lanes(t):
        t = t.reshape(npair, 2, BLK)
        return jnp.repeat(jnp.swapaxes(t, 1, 2), HEAD_DIM, axis=2)

    parts = [one(decay_f, False), one(decay_b, True)]
    dm = jnp.stack([p[0].reshape(npair, 2, BLK, BLK) for p in parts], axis=1)
    qd = jnp.stack([lanes(p[1]) for p in parts], axis=1)
    kd = jnp.stack([lanes(p[2]) for p in parts], axis=1)
    bd = _pair_block_diag((LANES, LANES))
    cm = jnp.stack([jnp.where(bd[None], jnp.repeat(p[3].reshape(npair, 2), HEAD_DIM, axis=1)[:, :, None], 0.0)
                    for p in parts], axis=1)
    return dm, qd, kd, cm


def _pair_states(s):
    b = s.shape[0]
    s = s.astype(F32).reshape(b, RET_HEADS // 2, 2, HEAD_DIM, HEAD_DIM)
    z = jnp.zeros_like(s[:, :, 0])
    top = jnp.concatenate([s[:, :, 0], z], axis=-1)
    bot = jnp.concatenate([z, s[:, :, 1]], axis=-1)
    return jnp.concatenate([top, bot], axis=-2)


def _unpair_states(s):
    b = s.shape[0]
    h0 = s[:, :, :HEAD_DIM, :HEAD_DIM]
    h1 = s[:, :, HEAD_DIM:, HEAD_DIM:]
    return jnp.stack([h0, h1], axis=2).reshape(b, RET_HEADS, HEAD_DIM, HEAD_DIM)


LOG2E = 1.4426950408889634


def _kv_variants(k, v):
    lane = lax.broadcasted_iota(jnp.int32, k.shape, 1)
    lo = lane < HEAD_DIM
    ka = jnp.where(lo, k, 0.0)
    kb = jnp.where(lo, 0.0, k)
    va = jnp.where(lane == HEAD_DIM, 1.0, jnp.where(lo, v, 0.0))
    vb = jnp.where(lane == 0, 1.0, jnp.where(lo, 0.0, v))
    ks = (ka, pltpu.roll(ka, HEAD_DIM, 1), pltpu.roll(kb, HEAD_DIM, 1), kb)
    vs = (va, pltpu.roll(va, HEAD_DIM, 1), pltpu.roll(vb, HEAD_DIM, 1), vb)
    return [t.astype(BF16) for t in ks], [t.astype(BF16) for t in vs]


def _attn_core(q, k, v, sink_ref, o_ref, mask):
    tq = q.shape[0]
    ks, vs = _kv_variants(k, v)
    lane = lax.broadcasted_iota(jnp.int32, (2 * tq, LANES), 1)
    first = lax.broadcasted_iota(jnp.int32, (2 * tq, 1), 0) < tq
    if mask is not None:
        mask = jnp.concatenate([mask, mask], axis=0)
    for half in (0, 1):
        q2 = jnp.concatenate([q[:, (2 * half) * LANES:(2 * half + 1) * LANES],
                              q[:, (2 * half + 1) * LANES:(2 * half + 2) * LANES]], axis=0)
        q2 = (q2 * LOG2E).astype(BF16)
        out = None
        for hh in (0, 1):
            var = 2 * half + hh
            s = _dot_nt(q2, ks[var])
            if mask is not None:
                s = jnp.where(mask, s, NEG_INF)
            m = jnp.max(s, axis=-1, keepdims=True)
            if sink_ref is not None:
                snk = LOG2E * jnp.where(first, sink_ref[2 * half, hh:hh + 1, 0:1],
                                        sink_ref[2 * half + 1, hh:hh + 1, 0:1])
                m = jnp.maximum(m, snk)
            acc = _dot(jnp.exp2((s - m).astype(BF16)), vs[var])
            ones_lane = HEAD_DIM if hh == 0 else 0
            den = acc[:, ones_lane:ones_lane + 1]
            if sink_ref is not None:
                den = den + jnp.exp2(snk - m)
            own = (lane < HEAD_DIM) if hh == 0 else (lane >= HEAD_DIM)
            o = jnp.where(own, acc, 0.0) * (1.0 / den)
            out = o if out is None else out + o
        o_ref[:, (2 * half) * LANES:(2 * half + 1) * LANES] = out[:tq]
        o_ref[:, (2 * half + 1) * LANES:(2 * half + 2) * LANES] = out[tq:]


def _full_attn_kernel(q_ref, k_ref, v_ref, *rest, has_sink):
    if has_sink:
        sink_ref, o_ref = rest
    else:
        sink_ref, (o_ref,) = None, rest
    _attn_core(q_ref[...], k_ref[...], v_ref[...], sink_ref, o_ref, None)


def _full_attention(q_arr, q_rb0, q_cb0, k_arr, k_rb0, k_cb0, v_arr, v_rb0, v_cb0,
                    n_seq, lq, lk, tq, n_groups, sink=None):
    nq = lq // tq
    qw = 4 * LANES
    in_specs = [
        pl.BlockSpec((tq, qw), lambda b, g, i: (q_rb0 + b * nq + i, q_cb0 + g)),
        pl.BlockSpec((lk, LANES), lambda b, g, i: (k_rb0 + b, k_cb0 + g)),
        pl.BlockSpec((lk, LANES), lambda b, g, i: (v_rb0 + b, v_cb0 + g)),
    ]
    args = [q_arr, k_arr, v_arr]
    if sink is not None:
        in_specs.append(pl.BlockSpec((None, 4, 2, LANES), lambda b, g, i: (g, 0, 0, 0)))
        args.append(sink)
    return pl.pallas_call(
        functools.partial(_full_attn_kernel, has_sink=sink is not None),
        grid=(n_seq, n_groups, nq),
        in_specs=in_specs,
        out_specs=pl.BlockSpec((tq, qw), lambda b, g, i: (b * nq + i, g)),
        out_shape=jax.ShapeDtypeStruct((n_seq * lq, n_groups * qw), F32),
        compiler_params=_cparams(("arbitrary", "arbitrary", "arbitrary")),
        name="full_attention",
    )(*args)


def _latent_attn_kernel(q_ref, k_ref, v_ref, kx_ref, vx_ref, o_ref, kvar, vvar, *, tq, kb):
    lk = DEC_SEQ + PAST_LEN

    @pl.when(pl.program_id(2) == 0)
    def _():
        for src_k, src_v, r0, nrows in ((k_ref, v_ref, 0, DEC_SEQ), (kx_ref, vx_ref, DEC_SEQ, PAST_LEN)):
            for c in range(nrows // kb):
                src = pl.ds(c * kb, kb)
                dst = pl.ds(r0 + c * kb, kb)
                ks, vs = _kv_variants(src_k[src, :], src_v[src, :])
                for i in range(4):
                    kvar[i, dst, :] = ks[i]
                    vvar[i, dst, :] = vs[i]

    lane = lax.broadcasted_iota(jnp.int32, (2 * tq, LANES), 1)
    for half in (0, 1):
        q2 = jnp.concatenate([q_ref[:, (2 * half) * LANES:(2 * half + 1) * LANES],
                              q_ref[:, (2 * half + 1) * LANES:(2 * half + 2) * LANES]], axis=0)
        q2 = (q2 * LOG2E).astype(BF16)
        out = None
        for hh in (0, 1):
            var = 2 * half + hh
            m = jnp.full((2 * tq, 1), -jnp.inf, F32)
            acc = jnp.zeros((2 * tq, LANES), F32)
            for j in range(lk // kb):
                rows = pl.ds(j * kb, kb)
                s = _dot_nt(q2, kvar[var, rows, :])
                m_new = jnp.maximum(m, jnp.max(s, axis=-1, keepdims=True))
                e = jnp.exp2((s - m_new).astype(BF16))
                acc = jnp.exp2(m - m_new) * acc + _dot(e, vvar[var, rows, :])
                m = m_new
            own = (lane < HEAD_DIM) if hh == 0 else (lane >= HEAD_DIM)
            ones_lane = HEAD_DIM if hh == 0 else 0
            o = jnp.where(own, acc, 0.0) * (1.0 / acc[:, ones_lane:ones_lane + 1])
            out = o if out is None else out + o
        o_ref[:, (2 * half) * LANES:(2 * half + 1) * LANES] = out[:tq]
        o_ref[:, (2 * half + 1) * LANES:(2 * half + 2) * LANES] = out[tq:]


def _latent_attention(p, kx, vx):
    tq, kb = 256, 512
    nq = DEC_SEQ // tq
    ngr = ATT_KV_HEADS // 2
    kcb = ATT_QW // LANES
    rb_q = T_PROMPT // tq
    rb_k = T_PROMPT // DEC_SEQ
    lk = DEC_SEQ + PAST_LEN
    ctx = pl.BlockSpec((None, PAST_LEN, LANES), lambda b, g, i: (b, 0, g))
    return pl.pallas_call(
        functools.partial(_latent_attn_kernel, tq=tq, kb=kb),
        grid=(DEC_BATCH, ngr, nq),
        in_specs=[pl.BlockSpec((tq, 4 * LANES), lambda b, g, i: (rb_q + b * nq + i, g)),
                  pl.BlockSpec((DEC_SEQ, LANES), lambda b, g, i: (rb_k + b, kcb + g)),
                  pl.BlockSpec((DEC_SEQ, LANES), lambda b, g, i: (rb_k + b, kcb + ngr + g)),
                  ctx, ctx],
        out_specs=pl.BlockSpec((tq, 4 * LANES), lambda b, g, i: (b * nq + i, g)),
        out_shape=jax.ShapeDtypeStruct((T_SAMPLE, ATT_QW), F32),
        scratch_shapes=[pltpu.VMEM((4, lk, LANES), BF16), pltpu.VMEM((4, lk, LANES), BF16)],
        compiler_params=_cparams(("arbitrary", "arbitrary", "arbitrary")),
        name="latent_attention",
    )(p, p, p, kx, vx)


WIN_TQ = 2 * BLK
WIN_KBLKS = WIN_TQ // BLK + 2


def _window_attn_kernel(q_ref, *refs):
    k_refs = refs[:WIN_KBLKS]
    v_refs = refs[WIN_KBLKS:2 * WIN_KBLKS]
    kx_ref, vx_ref, sink_ref, o_ref = refs[2 * WIN_KBLKS:]
    n = pl.program_id(1)
    k = jnp.concatenate([r[...] for r in k_refs] + [kx_ref[...]], axis=0)
    v = jnp.concatenate([r[...] for r in v_refs] + [vx_ref[...]], axis=0)
    nloc = WIN_KBLKS * BLK
    shape = (WIN_TQ, nloc + PAST_LEN)
    i = lax.broadcasted_iota(jnp.int32, shape, 0)
    r = lax.broadcasted_iota(jnp.int32, shape, 1)
    kpos = n * WIN_TQ - BLK + r
    local = (jnp.abs(r - BLK - i) <= WINDOW) & (kpos >= 0) & (kpos < DEC_SEQ)
    mask = local | (r >= nloc)
    _attn_core(q_ref[...], k, v, sink_ref, o_ref, mask)


def _window_attention(p, kx, vx, sink):
    nb = DEC_SEQ // BLK
    nq = DEC_SEQ // WIN_TQ
    rb0 = T_PROMPT // BLK
    qcb = (4 * RET_QW) // (4 * LANES)
    kcb = (4 * RET_QW + WIN_QW) // LANES
    vcb = kcb + 1

    def kv(cb, off):
        return pl.BlockSpec((BLK, LANES),
                            lambda b, n: (rb0 + b * nb + jnp.clip(n * (WIN_TQ // BLK) + off, 0, nb - 1), cb))

    offs = range(-1, WIN_KBLKS - 1)
    ctx = pl.BlockSpec((None, PAST_LEN, LANES), lambda b, n: (b, 0, 0))
    qspec = pl.BlockSpec((WIN_TQ, 4 * LANES), lambda b, n: (T_PROMPT // WIN_TQ + b * nq + n, qcb))
    return pl.pallas_call(
        _window_attn_kernel,
        grid=(DEC_BATCH, nq),
        in_specs=[qspec] + [kv(kcb, o) for o in offs] + [kv(vcb, o) for o in offs] + [
            ctx, ctx, pl.BlockSpec((4, 2, LANES), lambda b, n: (0, 0, 0))],
        out_specs=pl.BlockSpec((WIN_TQ, 4 * LANES), lambda b, n: (b * nq + n, 0)),
        out_shape=jax.ShapeDtypeStruct((T_SAMPLE, WIN_QW), F32),
        compiler_params=_cparams(("arbitrary", "arbitrary")),
        name="window_attention",
    )(*([p] * (1 + 2 * WIN_KBLKS)), kx, vx, sink)


def _sink_table(sink, n_groups):
    s = sink.astype(F32).reshape(n_groups, 4, 2, 1)
    return jnp.broadcast_to(s, (n_groups, 4, 2, LANES))


def _cache_rows(cache):
    b, h, l, d = cache.shape
    return cache.astype(F32).transpose(0, 2, 1, 3).reshape(b, l, h * d)


def _outproj_kernel(*refs, n_in):
    a_refs = refs[:2 * n_in]
    w_refs = refs[2 * n_in:3 * n_in]
    xp_ref, xs_ref, mod_ref, g_ref, wr_ref, xn_ref, hf_ref, lg_ref = refs[3 * n_in:]
    y = None
    for k, w_ref in enumerate(w_refs):
        a = _pick(a_refs[2 * k], a_refs[2 * k + 1])
        t = _dot(a.astype(BF16), w_ref[...])
        y = t if y is None else y + t
    xn = _pick(xp_ref, xs_ref) + mod_ref[2:3, :] * _rms(y, g_ref[1:2, :])
    hf = _rms(xn, g_ref[2:3, :]) * (1.0 + mod_ref[4:5, :]) + mod_ref[3:4, :]
    xn_ref[...] = xn
    hf_ref[...] = hf
    lg_ref[...] = _dot_nt(wr_ref[...], hf.astype(BF16))


def _outproj(parts, w_parts, x, mod, layer, gains, w_router_t):
    n_in = len(parts)
    in_specs = []
    args = []
    for a_p, a_s in parts:
        in_specs += _pair_specs(a_p.shape[1])
        args += [a_p, a_s]
    in_specs += [pl.BlockSpec(w.shape, lambda i: (0, 0)) for w in w_parts]
    in_specs += _pair_specs(D_MODEL) + [
        pl.BlockSpec((None, None, 6, D_MODEL), lambda i: (layer, _mod_index(i), 0, 0)),
        pl.BlockSpec((4, D_MODEL), lambda i: (0, 0)),
        pl.BlockSpec((N_EXPERTS, D_MODEL), lambda i: (0, 0)),
    ]
    row = pl.BlockSpec((TM, D_MODEL), lambda i: (i, 0))
    return pl.pallas_call(
        functools.partial(_outproj_kernel, n_in=n_in),
        grid=(T_ALL // TM,),
        in_specs=in_specs,
        out_specs=[row, row, pl.BlockSpec((N_EXPERTS, TM), lambda i: (0, i))],
        out_shape=[jax.ShapeDtypeStruct((T_ALL, D_MODEL), F32),
                   jax.ShapeDtypeStruct((T_ALL, D_MODEL), F32),
                   jax.ShapeDtypeStruct((N_EXPERTS, T_ALL), F32)],
        compiler_params=_cparams(("arbitrary",)),
        name="outproj",
    )(*args, *w_parts, x[0], x[1], mod, gains, w_router_t)


def _gather_kernel(idx_ref, h_ref, o_ref, buf, *, eg, cap):
    b = pl.program_id(0)
    g = pl.program_id(1)
    for e in range(eg):
        base = (g * eg + e) * cap

        def body(c, carry):
            t = idx_ref[b, base + c]
            buf[pl.ds(c, 1), :] = h_ref[pl.ds(t, 1), :]
            return carry

        lax.fori_loop(0, cap, body, 0, unroll=8)
        o_ref[e] = buf[...].astype(BF16)


def _gather(h, idx, n_seq, seq_len, cap, row0, eg):
    rb0 = row0 // seq_len
    return pl.pallas_call(
        functools.partial(_gather_kernel, eg=eg, cap=cap),
        grid_spec=pltpu.PrefetchScalarGridSpec(
            num_scalar_prefetch=1,
            grid=(n_seq, N_EXPERTS // eg),
            in_specs=[pl.BlockSpec((seq_len, D_MODEL), lambda b, g, idx: (rb0 + b, 0))],
            out_specs=pl.BlockSpec((eg, cap, D_MODEL), lambda b, g, idx: (g, b, 0)),
            scratch_shapes=[pltpu.VMEM((cap, D_MODEL), F32)],
        ),
        out_shape=jax.ShapeDtypeStruct((N_EXPERTS, n_seq * cap, D_MODEL), BF16),
        compiler_params=_cparams(("arbitrary", "arbitrary")),
        name="moe_gather",
    )(idx, h)


def _moe_kernel(xp_ref, xs_ref, wg_ref, wu_ref, wd_ref, gwp_ref, gws_ref, yp_ref, ys_ref, hid_scr, wd_scr):
    f = pl.program_id(1)
    wg = wg_ref[...].astype(BF16)
    wu = wu_ref[...].astype(BF16)
    cols = pl.ds(pl.multiple_of(f * TF, TF), TF)
    wd_scr[cols, :] = wd_ref[...].astype(BF16)
    for h, x_ref in enumerate((xp_ref, xs_ref)):
        x = x_ref[...]
        hid_scr[h, :, cols] = (_silu(_dot(x, wg)) * _dot(x, wu)).astype(BF16)

    @pl.when(f == pl.num_programs(1) - 1)
    def _():
        for h, (gw_ref, y_ref) in enumerate(((gwp_ref, yp_ref), (gws_ref, ys_ref))):
            for r in range(0, hid_scr.shape[1], MOE_ROW_TILE):
                rows = slice(r, r + MOE_ROW_TILE)
                y_ref[rows, :] = _dot(hid_scr[h, rows, :], wd_scr[...]) * gw_ref[rows, :]


MOE_ROW_TILE = 512


def _moe_ffn(xg_p, xg_s, w_gate, w_up, w_down, layer, gw_p, gw_s):
    assert ROWS_P == ROWS_S
    nf = EXPERT_FF // TF
    xspec_p = pl.BlockSpec((None, ROWS_P, D_MODEL), lambda e, f: (e, 0, 0))
    xspec_s = pl.BlockSpec((None, ROWS_S, D_MODEL), lambda e, f: (e, 0, 0))
    return pl.pallas_call(
        _moe_kernel,
        grid=(N_EXPERTS, nf),
        in_specs=[xspec_p, xspec_s,
                  pl.BlockSpec((None, None, D_MODEL, TF), lambda e, f: (layer, e, 0, f)),
                  pl.BlockSpec((None, None, D_MODEL, TF), lambda e, f: (layer, e, 0, f)),
                  pl.BlockSpec((None, None, TF, D_MODEL), lambda e, f: (layer, e, f, 0)),
                  pl.BlockSpec((None, ROWS_P, 1), lambda e, f: (e, 0, 0)),
                  pl.BlockSpec((None, ROWS_S, 1), lambda e, f: (e, 0, 0))],
        out_specs=[xspec_p, xspec_s],
        out_shape=[jax.ShapeDtypeStruct((N_EXPERTS, ROWS_P, D_MODEL), F32),
                   jax.ShapeDtypeStruct((N_EXPERTS, ROWS_S, D_MODEL), F32)],
        scratch_shapes=[pltpu.VMEM((2, ROWS_P, EXPERT_FF), BF16),
                        pltpu.VMEM((EXPERT_FF, D_MODEL), BF16)],
        compiler_params=_cparams(("arbitrary", "arbitrary")),
        name="moe_ffn",
    )(xg_p, xg_s, w_gate, w_up, w_down, gw_p, gw_s)


COMBINE_GROUP = 8


def _combine_kernel(idx_ref, y_ref, o_ref, *, eg, cap):
    b = pl.program_id(0)
    g = pl.program_id(1)

    @pl.when(g == 0)
    def _():
        o_ref[...] = jnp.zeros_like(o_ref)

    for e in range(eg):
        base = (g * eg + e) * cap

        def body(c, carry):
            c0 = c * COMBINE_GROUP
            toks = [idx_ref[b, base + c0 + k] for k in range(COMBINE_GROUP)]
            rows = [o_ref[pl.ds(t, 1), :] + y_ref[e, pl.ds(c0 + k, 1), :] for k, t in enumerate(toks)]
            for t, row in zip(toks, rows):
                o_ref[pl.ds(t, 1), :] = row
            return carry

        lax.fori_loop(0, cap // COMBINE_GROUP, body, 0)


def _combine(y, idx, n_seq, seq_len, cap, eg):
    return pl.pallas_call(
        functools.partial(_combine_kernel, eg=eg, cap=cap),
        grid_spec=pltpu.PrefetchScalarGridSpec(
            num_scalar_prefetch=1,
            grid=(n_seq, N_EXPERTS // eg),
            in_specs=[pl.BlockSpec((eg, cap, D_MODEL), lambda b, g, idx: (g, b, 0))],
            out_specs=pl.BlockSpec((seq_len, D_MODEL), lambda b, g, idx: (b, 0)),
        ),
        out_shape=jax.ShapeDtypeStruct((n_seq * seq_len, D_MODEL), F32),
        compiler_params=_cparams(("arbitrary", "arbitrary")),
        name="moe_combine",
    )(idx, y)


def _onehot_t(idx_ref, n_tok):
    n_slots = idx_ref.shape[-1]
    tok = lax.broadcasted_iota(jnp.int32, (n_tok, n_slots), 0)
    return jnp.where(tok == idx_ref[...], 1.0, 0.0)


def _gather_mm_kernel(idx_ref, h_ref, o_ref, *, cap):
    sel = _onehot_t(idx_ref, h_ref.shape[0]).T.astype(BF16)
    rows = _dot(sel, h_ref[...].astype(BF16)).astype(BF16)
    for e in range(N_EXPERTS):
        o_ref[e] = rows[e * cap:(e + 1) * cap]


def _gather_mm(h, idx, n_seq, seq_len, cap):
    return pl.pallas_call(
        functools.partial(_gather_mm_kernel, cap=cap),
        grid=(n_seq,),
        in_specs=[pl.BlockSpec((None, 1, N_EXPERTS * cap), lambda b: (b, 0, 0)),
                  pl.BlockSpec((seq_len, D_MODEL), lambda b: (b, 0))],
        out_specs=pl.BlockSpec((N_EXPERTS, cap, D_MODEL), lambda b: (0, b, 0)),
        out_shape=jax.ShapeDtypeStruct((N_EXPERTS, n_seq * cap, D_MODEL), BF16),
        compiler_params=_cparams(("arbitrary",)),
        name="moe_gather_mm",
    )(idx, h)


def _combine_mm_kernel(idx_ref, y_ref, o_ref):
    sel = _onehot_t(idx_ref, o_ref.shape[0]).astype(BF16)
    y = jnp.concatenate([y_ref[e] for e in range(N_EXPERTS)], axis=0)
    hi = y.astype(BF16)
    lo = (y - hi.astype(F32)).astype(BF16)
    o_ref[...] = _dot(sel, hi) + _dot(sel, lo)


def _combine_mm(y, idx, n_seq, seq_len, cap):
    return pl.pallas_call(
        _combine_mm_kernel,
        grid=(n_seq,),
        in_specs=[pl.BlockSpec((None, 1, N_EXPERTS * cap), lambda b: (b, 0, 0)),
                  pl.BlockSpec((N_EXPERTS, cap, D_MODEL), lambda b: (0, b, 0))],
        out_specs=pl.BlockSpec((seq_len, D_MODEL), lambda b: (b, 0)),
        out_shape=jax.ShapeDtypeStruct((n_seq * seq_len, D_MODEL), F32),
        compiler_params=_cparams(("arbitrary",)),
        name="moe_combine_mm",
    )(idx, y)


def _postnorm_kernel(x_ref, yp_ref, ys_ref, mod_ref, g_ref, op_ref, os_ref):
    i = pl.program_id(0)
    o = x_ref[...] + mod_ref[5:6, :] * _rms(_pick(yp_ref, ys_ref), g_ref[3:4, :])

    @pl.when(i < NPT)
    def _():
        op_ref[...] = o

    @pl.when(i >= NPT)
    def _():
        os_ref[...] = o


def _postnorm(x, y_p, y_s, mod, layer, gains):
    return pl.pallas_call(
        _postnorm_kernel,
        grid=(T_ALL // TM,),
        in_specs=[pl.BlockSpec((TM, D_MODEL), lambda i: (i, 0))] + _pair_specs(D_MODEL) + [
            pl.BlockSpec((None, None, 6, D_MODEL), lambda i: (layer, _mod_index(i), 0, 0)),
            pl.BlockSpec((4, D_MODEL), lambda i: (0, 0))],
        out_specs=_pair_specs(D_MODEL),
        out_shape=[jax.ShapeDtypeStruct((T_PROMPT, D_MODEL), F32),
                   jax.ShapeDtypeStruct((T_SAMPLE, D_MODEL), F32)],
        compiler_params=_cparams(("arbitrary",)),
        name="postnorm",
    )(x, y_p, y_s, mod, gains)


def _excl_prefix(mask):
    r, n = mask.shape
    i0 = lax.broadcasted_iota(jnp.int32, (LANES, LANES), 0)
    i1 = lax.broadcasted_iota(jnp.int32, (LANES, LANES), 1)
    upper = jnp.where(i0 < i1, 1.0, 0.0).astype(BF16)
    ones = jnp.where(mask, 1.0, 0.0)
    carry = jnp.zeros((r, 1), F32)
    out = []
    for c in range(n // LANES):
        ch = ones[:, c * LANES:(c + 1) * LANES]
        out.append(_dot(ch.astype(BF16), upper) + carry)
        carry = carry + jnp.sum(ch, axis=1, keepdims=True)
    return jnp.concatenate(out, axis=1)


def _route_kernel(lg_ref, idx_ref, gw_ref, *, nb, n, cap):
    affs = []
    for s in range(nb):
        l = lg_ref[:, s * n:(s + 1) * n]
        e = jnp.exp(l - jnp.max(l, axis=0, keepdims=True))
        affs.append(e / jnp.sum(e, axis=0, keepdims=True))
    aff = affs[0] if nb == 1 else jnp.concatenate(affs, axis=0)
    rows = nb * N_EXPERTS
    bits = jnp.zeros((rows, 1), jnp.int32)
    for bit in range(30, -1, -1):
        cand = bits | (1 << bit)
        cnt = jnp.sum(jnp.where(aff >= pltpu.bitcast(cand, F32), 1.0, 0.0), axis=1, keepdims=True)
        bits = jnp.where(cnt >= cap, cand, bits)
    thr = pltpu.bitcast(bits, F32)
    gt = aff > thr
    eq = aff == thr
    need = cap - jnp.sum(jnp.where(gt, 1.0, 0.0), axis=1, keepdims=True)
    sel = gt | (eq & (_excl_prefix(eq) < need))
    pos = jnp.where(sel, _excl_prefix(sel), -1.0)
    tok = lax.broadcasted_iota(jnp.int32, (rows, n), 1).astype(F32)
    capw = max(cap, LANES)
    slot_lane = lax.broadcasted_iota(jnp.int32, (rows, capw), 1)

    def slot(s, carry):
        idx_acc, gw_acc = carry
        hit = pos == lax.convert_element_type(s, F32)
        ic = jnp.sum(jnp.where(hit, tok, 0.0), axis=1, keepdims=True)
        gc = jnp.sum(jnp.where(hit, aff, 0.0), axis=1, keepdims=True)
        here = slot_lane == s
        return jnp.where(here, ic, idx_acc), jnp.where(here, gc, gw_acc)

    zero = jnp.zeros((rows, capw), F32)
    idx_acc, gw_acc = lax.fori_loop(0, cap, slot, (zero, zero), unroll=8)
    for s in range(nb):
        r = slice(s * N_EXPERTS, (s + 1) * N_EXPERTS)
        idx_ref[s] = idx_acc[r, :cap].astype(jnp.int32)
        gw_ref[s] = gw_acc[r, :cap]


def _route(logits_t, n_seq, seq_len, cap, col0, nb):
    cb0 = col0 // (nb * seq_len)
    out = pl.BlockSpec((nb, N_EXPERTS, cap), lambda b: (b, 0, 0))
    return pl.pallas_call(
        functools.partial(_route_kernel, nb=nb, n=seq_len, cap=cap),
        grid=(n_seq // nb,),
        in_specs=[pl.BlockSpec((N_EXPERTS, nb * seq_len), lambda b: (0, cb0 + b))],
        out_specs=[out, out],
        out_shape=[jax.ShapeDtypeStruct((n_seq, N_EXPERTS, cap), jnp.int32),
                   jax.ShapeDtypeStruct((n_seq, N_EXPERTS, cap), F32)],
        compiler_params=_cparams(("arbitrary",)),
        name="route",
    )(logits_t)


def _expert_ffn(hf, logits_t, w_gate, w_up, w_down, layer):
    idx_p, gw_p = _route(logits_t, BATCH, SEQ, CAP_P, 0, 8)
    idx_s, gw_s = _route(logits_t, DEC_BATCH, DEC_SEQ, CAP_S, T_PROMPT, 1)
    idx_p = idx_p.reshape(BATCH, 1, N_EXPERTS * CAP_P)
    idx_s = idx_s.reshape(DEC_BATCH, N_EXPERTS * CAP_S)
    xg_p = _gather_mm(hf, idx_p, BATCH, SEQ, CAP_P)
    xg_s = _gather(hf, idx_s, DEC_BATCH, DEC_SEQ, CAP_S, T_PROMPT, 4)
    gw_p = gw_p.transpose(1, 0, 2).reshape(N_EXPERTS, ROWS_P, 1)
    gw_s = gw_s.transpose(1, 0, 2).reshape(N_EXPERTS, ROWS_S, 1)
    y_p, y_s = _moe_ffn(xg_p, xg_s, w_gate, w_up, w_down, layer, gw_p, gw_s)
    o_p = _combine_mm(y_p, idx_p, BATCH, SEQ, CAP_P)
    o_s = _combine(y_s, idx_s, DEC_BATCH, DEC_SEQ, CAP_S, 2)
    return o_p, o_s


def _rope_tables():
    t = jnp.arange(DEC_SEQ)
    row = (t // GRID_W).astype(F32)
    col = (t % GRID_W).astype(F32)
    inv = ROPE_THETA ** (-jnp.arange(0, AXIS_DIM, 2, dtype=F32) / AXIS_DIM)
    ar = row[:, None] * inv[None]
    ac = col[:, None] * inv[None]
    ang = jnp.concatenate([ar, ar, ac, ac], axis=-1)
    ang = jnp.concatenate([ang, ang], axis=-1)
    cos = jnp.concatenate([jnp.ones((TM, LANES), F32), jnp.cos(ang)], axis=0)
    sin = jnp.concatenate([jnp.zeros((TM, LANES), F32), jnp.sin(ang)], axis=0)
    return cos, sin


def _split_kv_heads(rows, n_heads):
    return rows.reshape(BATCH, SEQ, n_heads, HEAD_DIM).transpose(0, 2, 1, 3)[:, None]


def kernel(x_prompt, x_sample, state_ret_fwd, state_ret_bwd, cache_win_k, cache_win_v, cache_attn_k, cache_attn_v, c, c_ctx, w_ada, b_ada, norm_gains, w_in_even, w_out_even, ret_decay_fwd, ret_decay_bwd, win_sink, w_in_odd, w_out_odd, q_norm, k_norm, w_router, w_gate, w_up, w_down):
    x = (x_prompt.reshape(T_PROMPT, D_MODEL), x_sample.reshape(T_SAMPLE, D_MODEL))
    cond =jnp.concatenate([c_ctx[None, :], c, jnp.zeros((8 - 1 - DEC_BATCH, D_MODEL), F32)], axis=0)
    mod = _modulation(cond, w_ada, b_ada).reshape(DEPTH, 8, 6, D_MODEL)
    cos_t, sin_t = _rope_tables()
    outs = {}
    for layer in range(DEPTH):
        j = layer // 2
        gains = norm_gains[layer]
        if layer % 2 == 0:
            p = _inproj(x, mod, layer, gains[0], w_in_even[j].astype(BF16), cos_t, sin_t)
            tabs = _retention_tables(ret_decay_fwd[j], ret_decay_bwd[j])
            zero = jnp.zeros((BATCH, RET_HEADS // 2, LANES, LANES), F32)
            ret_p, sf, sb = _retention(p, tabs, zero, zero, BATCH, SEQ, 0, 4)
            ret_s, _, _ = _retention(p, tabs, _pair_states(state_ret_fwd[:, j]),
                                     _pair_states(state_ret_bwd[:, j]), DEC_BATCH, DEC_SEQ, T_PROMPT, 2)
            sink = _sink_table(win_sink[j], 1)
            qcb = (4 * RET_QW) // (4 * LANES)
            kcb = (4 * RET_QW + WIN_QW) // LANES
            win_p = _full_attention(p, 0, qcb, p, 0, kcb, p, 0, kcb + 1,
                                    BATCH, SEQ, SEQ, SEQ, 1, sink=sink)
            win_s = _window_attention(p, _cache_rows(cache_win_k[:, j]), _cache_rows(cache_win_v[:, j]),
                                      sink[0])
            w_out = w_out_even[j].astype(BF16)
            parts = [(ret_p, ret_s), (win_p, win_s)]
            w_parts = [w_out[:RET_QW], w_out[RET_QW:]]
            outs["ret_f"] = _unpair_states(sf)[:, None]
            outs["ret_b"] = _unpair_states(sb)[:, None]
            c0 = 4 * RET_QW + WIN_QW
            outs["win_k"] = _split_kv_heads(p[:T_PROMPT, c0:c0 + WIN_KW], WIN_KV_HEADS)
            outs["win_v"] = _split_kv_heads(p[:T_PROMPT, c0 + WIN_KW:c0 + 2 * WIN_KW], WIN_KV_HEADS)
        else:
            p = _inproj(x, mod, layer, gains[0], w_in_odd[j].astype(BF16), cos_t, sin_t,
                        qn=q_norm[j], kn=k_norm[j])
            ngr = ATT_KV_HEADS // 2
            kcb = ATT_QW // LANES
            att_p = _full_attention(p, 0, 0, p, 0, kcb, p, 0, kcb + ngr,
                                    BATCH, SEQ, SEQ, SEQ, ngr)
            att_s = _latent_attention(p, _cache_rows(cache_attn_k[:, j]), _cache_rows(cache_attn_v[:, j]))
            parts = [(att_p, att_s)]
            w_parts = [w_out_odd[j].astype(BF16)]
            outs["att_k"] = _split_kv_heads(p[:T_PROMPT, ATT_QW:ATT_QW + ATT_KW], ATT_KV_HEADS)
            outs["att_v"] = _split_kv_heads(p[:T_PROMPT, ATT_QW + ATT_KW:], ATT_KV_HEADS)
        xn, hf, logits_t = _outproj(parts, w_parts, x, mod, layer, gains,
                                    w_router[layer].T.astype(BF16))
        o_p, o_s = _expert_ffn(hf, logits_t, w_gate, w_up, w_down, layer)
        x = _postnorm(xn, o_p, o_s, mod, layer, gains)
    y_prompt = x[0].reshape(BATCH, SEQ, D_MODEL)
    y_sample = x[1].reshape(DEC_BATCH, DEC_SEQ, D_MODEL)
    return (y_prompt, y_sample, outs["ret_f"], outs["ret_b"], outs["win_k"], outs["win_v"],
            outs["att_k"], outs["att_v"])
```

```python
import functools

import jax
import jax.numpy as jnp
from jax import lax
from jax.experimental import pallas as pl
from jax.experimental.pallas import tpu as pltpu

D_MODEL = 1024
BATCH = 32
SEQ = 256
DEPTH = 2
DEC_BATCH = 2
DEC_SEQ = 4096
PAST_LEN = 512
GRID_W = 64
HEAD_DIM = 64
AXIS_DIM = HEAD_DIM // 2
ROPE_THETA = 10000.0
BLK = 128
WINDOW = 128
RET_HEADS = D_MODEL // 128
WIN_HEADS = D_MODEL // 128
WIN_KV_HEADS = WIN_HEADS // 4
ATT_HEADS = D_MODEL // HEAD_DIM
ATT_KV_HEADS = ATT_HEADS // 4
RET_QW = RET_HEADS * HEAD_DIM
WIN_QW = WIN_HEADS * HEAD_DIM
WIN_KW = WIN_KV_HEADS * HEAD_DIM
EVEN_IN = 4 * RET_QW + WIN_QW + 2 * WIN_KW
ATT_QW = ATT_HEADS * HEAD_DIM
ATT_KW = ATT_KV_HEADS * HEAD_DIM
ODD_IN = ATT_QW + 2 * ATT_KW
N_EXPERTS = 16
CAPACITY_FACTOR = 2
EXPERT_FF = ((8 * D_MODEL // 3 + 127) // 128) * 128
EPS = 1e-6
NEG_INF = -1e30
F32 = jnp.float32
BF16 = jnp.bfloat16

LANES = 128
T_PROMPT = BATCH * SEQ
T_SAMPLE = DEC_BATCH * DEC_SEQ
T_ALL = T_PROMPT + T_SAMPLE
TM = 512
CAP_P = CAPACITY_FACTOR * SEQ // N_EXPERTS
CAP_S = CAPACITY_FACTOR * DEC_SEQ // N_EXPERTS
ROWS_P = BATCH * CAP_P
ROWS_S = DEC_BATCH * CAP_S
TF = 256
VMEM_LIMIT = 56 * 1024 * 1024
QK_SCALE = HEAD_DIM ** -0.5


def _cparams(sem):
    return pltpu.CompilerParams(dimension_semantics=sem, vmem_limit_bytes=VMEM_LIMIT)


def _silu(x):
    return x * (1.0 / (1.0 + jnp.exp(-x)))


def _dot(a, b):
    return jnp.dot(a, b, preferred_element_type=F32)


def _dot_nt(a, b):
    return lax.dot_general(a, b, (((1,), (1,)), ((), ())), preferred_element_type=F32)


def _rms(x, g):
    return x * lax.rsqrt(jnp.mean(x * x, axis=-1, keepdims=True) + EPS) * g


def _pair_block_diag(shape):
    r = lax.broadcasted_iota(jnp.int32, shape, 0)
    c = lax.broadcasted_iota(jnp.int32, shape, 1)
    return (r // HEAD_DIM) == (c // HEAD_DIM)


def _head_sum(x, bd):
    hi = x.astype(BF16)
    lo = (x - hi.astype(F32)).astype(BF16)
    return _dot(hi, bd) + _dot(lo, bd)


def _rope(xb, cos, sin):
    lane = lax.broadcasted_iota(jnp.int32, xb.shape, 1)
    half = AXIS_DIM // 2
    rot = jnp.where((lane % AXIS_DIM) < half,
                    -pltpu.roll(xb, LANES - half, 1), pltpu.roll(xb, half, 1))
    return xb * cos + rot * sin


def _mod_kernel(c_ref, w_ref, b_ref, o_ref):
    o_ref[...] = _dot(_silu(c_ref[...]), w_ref[...]) + b_ref[...]


def _modulation(cond, w_ada, b_ada):
    tn = 1536
    n = 6 * D_MODEL
    return pl.pallas_call(
        _mod_kernel,
        grid=(DEPTH, n // tn),
        in_specs=[
            pl.BlockSpec((8, D_MODEL), lambda l, j: (0, 0)),
            pl.BlockSpec((None, D_MODEL, tn), lambda l, j: (l, 0, j)),
            pl.BlockSpec((None, 1, tn), lambda l, j: (l, 0, j)),
        ],
        out_specs=pl.BlockSpec((None, 8, tn), lambda l, j: (l, 0, j)),
        out_shape=jax.ShapeDtypeStruct((DEPTH, 8, n), F32),
        compiler_params=_cparams(("arbitrary", "arbitrary")),
        name="modulation",
    )(cond, w_ada, b_ada.reshape(DEPTH, 1, n))


def _mod_index(i):
    npt = T_PROMPT // TM
    return jnp.where(i < npt, 0, 1 + (i - npt) // (DEC_SEQ // TM))


def _rope_index(i):
    npt = T_PROMPT // TM
    return jnp.where(i < npt, 0, 1 + (i - npt) % (DEC_SEQ // TM))


NPT = T_PROMPT // TM


def _pair_specs(width):
    return [pl.BlockSpec((TM, width), lambda i: (jnp.minimum(i, NPT - 1), 0)),
            pl.BlockSpec((TM, width), lambda i: (jnp.maximum(i - NPT, 0), 0))]


def _pick(p_ref, s_ref):
    return jnp.where(pl.program_id(0) < NPT, p_ref[...], s_ref[...])


def _inproj_kernel(xp_ref, xs_ref, mod_ref, g_ref, w_ref, cos_ref, sin_ref, *rest, even):
    if even:
        (o_ref,) = rest
    else:
        qn_ref, kn_ref, o_ref = rest
    h = _rms(_pick(xp_ref, xs_ref), g_ref[...]) * (1.0 + mod_ref[1:2, :]) + mod_ref[0:1, :]
    p = _dot(h.astype(BF16), w_ref[...])
    cos = cos_ref[...]
    sin = sin_ref[...]
    nblk = p.shape[1] // LANES
    if even:
        nq = RET_QW // LANES
        rope_blocks = set(range(0, 2 * nq)) | set(range(4 * nq, 5 * nq + 1))
        scaled = set(range(nq, 2 * nq)) | set(range(4 * nq, 5 * nq))
        normed = {}
    else:
        nq = ATT_QW // LANES
        nk = ATT_KW // LANES
        rope_blocks = set(range(0, nq + nk))
        scaled = set(range(0, nq))
        normed = {b: (qn_ref if b < nq else kn_ref) for b in range(nq + nk)}
        bd = _pair_block_diag((LANES, LANES)).astype(BF16)
    for b in range(nblk):
        blk = p[:, b * LANES:(b + 1) * LANES]
        if b in normed:
            ms = _head_sum(blk * blk, bd) * (1.0 / HEAD_DIM)
            blk = blk * lax.rsqrt(ms + EPS) * normed[b][...]
        if b in rope_blocks:
            blk = _rope(blk, cos, sin)
        if b in scaled:
            blk = blk * QK_SCALE
        o_ref[:, b * LANES:(b + 1) * LANES] = blk


def _inproj(x, mod, layer, gain, w, cos_t, sin_t, qn=None, kn=None):
    even = qn is None
    n = w.shape[1]
    in_specs = _pair_specs(D_MODEL) + [
        pl.BlockSpec((None, None, 6, D_MODEL), lambda i: (layer, _mod_index(i), 0, 0)),
        pl.BlockSpec((1, D_MODEL), lambda i: (0, 0)),
        pl.BlockSpec((D_MODEL, n), lambda i: (0, 0)),
        pl.BlockSpec((TM, LANES), lambda i: (_rope_index(i), 0)),
        pl.BlockSpec((TM, LANES), lambda i: (_rope_index(i), 0)),
    ]
    args = [x[0], x[1], mod, gain.reshape(1, D_MODEL), w, cos_t, sin_t]
    if not even:
        in_specs += [pl.BlockSpec((1, LANES), lambda i: (0, 0))] * 2
        args += [jnp.tile(qn, 2).reshape(1, LANES), jnp.tile(kn, 2).reshape(1, LANES)]
    return pl.pallas_call(
        functools.partial(_inproj_kernel, even=even),
        grid=(T_ALL // TM,),
        in_specs=in_specs,
        out_specs=pl.BlockSpec((TM, n), lambda i: (i, 0)),
        out_shape=jax.ShapeDtypeStruct((T_ALL, n), F32),
        compiler_params=_cparams(("arbitrary",)),
        name="inproj_even" if even else "inproj_odd",
    )(*args)


RET_READOUT_ROWS = 512


def _ret_kernel(q_ref, k_ref, v_ref, g_ref, dm_ref, qd_ref, kd_ref, cm_ref, s0f_ref, s0b_ref,
                o_ref, sf_ref, sb_ref, ob_scr, *, nc, npb):
    lane = lax.broadcasted_iota(jnp.int32, (BLK, LANES), 1)
    lo = lane < HEAD_DIM
    bd = _pair_block_diag((LANES, LANES))
    bd16 = bd.astype(BF16)
    sf_ref[...] = s0f_ref[...]
    sb_ref[...] = s0b_ref[...]

    def chunk(c, j, d, s_ref, dst):
        r = pl.ds(pl.multiple_of(c * BLK, BLK), BLK)
        cols = slice(j * LANES, (j + 1) * LANES)
        qc = q_ref[r, cols]
        kc = k_ref[r, cols]
        vc = v_ref[r, cols]
        s = s_ref[j]
        a = _dot_nt(jnp.concatenate([jnp.where(lo, qc, 0.0), jnp.where(lo, 0.0, qc)], axis=0), kc)
        lhs = jnp.concatenate([a[:BLK] * dm_ref[j, d, 0], a[BLK:] * dm_ref[j, d, 1], qc * qd_ref[j, d]], axis=1)
        rhs = jnp.concatenate([jnp.where(lo, vc, 0.0), jnp.where(lo, 0.0, vc), s], axis=0)
        dst[r, cols] = _dot(lhs, rhs)
        kv = _dot((kc * kd_ref[j, d]).T, vc)
        s_ref[j] = s * cm_ref[j, d] + jnp.where(bd, kv, 0.0)

    def body(c, carry):
        for j in range(npb):
            chunk(c, j, 0, sf_ref, o_ref)
            chunk(nc - 1 - c, j, 1, sb_ref, ob_scr)
        return carry

    lax.fori_loop(0, nc, body, 0)

    rt = min(nc * BLK, RET_READOUT_ROWS)

    def readout(c, carry):
        r = pl.ds(pl.multiple_of(c * rt, rt), rt)
        for j in range(npb):
            cols = slice(j * LANES, (j + 1) * LANES)
            o = o_ref[r, cols] + ob_scr[r, cols]
            mu = _head_sum(o, bd16) * (1.0 / HEAD_DIM)
            dlt = o - mu
            var = _head_sum(dlt * dlt, bd16) * (1.0 / HEAD_DIM)
            o_ref[r, cols] = dlt * lax.rsqrt(var + EPS) * _silu(g_ref[r, cols])
        return carry

    lax.fori_loop(0, nc * BLK // rt, readout, 0)


def _retention(p, tabs, s0f, s0b, n_seq, seq_len, row0, npb):
    npair = RET_HEADS // 2
    ng = npair // npb
    rb0 = row0 // seq_len
    dm, qd, kd, cm = tabs
    w = npb * LANES

    def col(c0):
        return pl.BlockSpec((seq_len, w), lambda b, j: (rb0 + b, c0 + j))

    tab4 = pl.BlockSpec((npb, 2, LANES, LANES), lambda b, j: (j, 0, 0, 0))
    st = pl.BlockSpec((None, npb, LANES, LANES), lambda b, j: (b, j, 0, 0))
    return pl.pallas_call(
        functools.partial(_ret_kernel, nc=seq_len // BLK, npb=npb),
        grid=(n_seq, ng),
        in_specs=[col(0), col(ng), col(2 * ng), col(3 * ng),
                  pl.BlockSpec((npb, 2, 2, BLK, BLK), lambda b, j: (j, 0, 0, 0, 0)),
                  tab4, tab4, tab4, st, st],
        out_specs=[pl.BlockSpec((seq_len, w), lambda b, j: (b, j)), st, st],
        out_shape=[jax.ShapeDtypeStruct((n_seq * seq_len, RET_QW), F32),
                   jax.ShapeDtypeStruct((n_seq, npair, LANES, LANES), F32),
                   jax.ShapeDtypeStruct((n_seq, npair, LANES, LANES), F32)],
        scratch_shapes=[pltpu.VMEM((seq_len, w), F32)],
        compiler_params=_cparams(("arbitrary", "arbitrary")),
        name="retention",
    )(p, p, p, p, dm, qd, kd, cm, s0f, s0b)


def _retention_tables(decay_f, decay_b):
    npair = RET_HEADS // 2
    idx = jnp.arange(BLK, dtype=F32)
    diff = idx[:, None] - idx[None, :]

    def one(decay, backward):
        lg = jax.nn.log_sigmoid(decay.astype(F32))
        dmask = jnp.where(diff >= 0, jnp.exp(lg[:, None, None] * jnp.maximum(diff, 0.0)), 0.0)
        q_dec = jnp.exp(lg[:, None] * (idx + 1.0))
        k_dec = jnp.exp(lg[:, None] * (BLK - 1.0 - idx))
        c_dec = jnp.exp(lg * BLK)
        if backward:
            dmask = jnp.swapaxes(dmask, 1, 2)
            q_dec = q_dec[:, ::-1]
            k_dec = k_dec[:, ::-1]
        return dmask, q_dec, k_dec, c_dec

    def lanes(t):
        t = t.reshape(npair, 2, BLK)
        return jnp.repeat(jnp.swapaxes(t, 1, 2), HEAD_DIM, axis=2)

    parts = [one(decay_f, False), one(decay_b, True)]
    dm = jnp.stack([p[0].reshape(npair, 2, BLK, BLK) for p in parts], axis=1)
    qd = jnp.stack([lanes(p[1]) for p in parts], axis=1)
    kd = jnp.stack([lanes(p[2]) for p in parts], axis=1)
    bd = _pair_block_diag((LANES, LANES))
    cm = jnp.stack([jnp.where(bd[None], jnp.repeat(p[3].reshape(npair, 2), HEAD_DIM, axis=1)[:, :, None], 0.0)
                    for p in parts], axis=1)
    return dm, qd, kd, cm


def _pair_states(s):
    b = s.shape[0]
    s = s.astype(F32).reshape(b, RET_HEADS // 2, 2, HEAD_DIM, HEAD_DIM)
    z = jnp.zeros_like(s[:, :, 0])
    top = jnp.concatenate([s[:, :, 0], z], axis=-1)
    bot = jnp.concatenate([z, s[:, :, 1]], axis=-1)
    return jnp.concatenate([top, bot], axis=-2)


def _unpair_states(s):
    b = s.shape[0]
    h0 = s[:, :, :HEAD_DIM, :HEAD_DIM]
    h1 = s[:, :, HEAD_DIM:, HEAD_DIM:]
    return jnp.stack([h0, h1], axis=2).reshape(b, RET_HEADS, HEAD_DIM, HEAD_DIM)


LOG2E = 1.4426950408889634


def _kv_variants(k, v):
    lane = lax.broadcasted_iota(jnp.int32, k.shape, 1)
    lo = lane < HEAD_DIM
    ka = jnp.where(lo, k, 0.0)
    kb = jnp.where(lo, 0.0, k)
    va = jnp.where(lane == HEAD_DIM, 1.0, jnp.where(lo, v, 0.0))
    vb = jnp.where(lane == 0, 1.0, jnp.where(lo, 0.0, v))
    ks = (ka, pltpu.roll(ka, HEAD_DIM, 1), pltpu.roll(kb, HEAD_DIM, 1), kb)
    vs = (va, pltpu.roll(va, HEAD_DIM, 1), pltpu.roll(vb, HEAD_DIM, 1), vb)
    return [t.astype(BF16) for t in ks], [t.astype(BF16) for t in vs]


def _attn_core(q, k, v, sink_ref, o_ref, mask):
    tq = q.shape[0]
    ks, vs = _kv_variants(k, v)
    lane = lax.broadcasted_iota(jnp.int32, (2 * tq, LANES), 1)
    first = lax.broadcasted_iota(jnp.int32, (2 * tq, 1), 0) < tq
    if mask is not None:
        mask = jnp.concatenate([mask, mask], axis=0)
    for half in (0, 1):
        q2 = jnp.concatenate([q[:, (2 * half) * LANES:(2 * half + 1) * LANES],
                              q[:, (2 * half + 1) * LANES:(2 * half + 2) * LANES]], axis=0)
        q2 = (q2 * LOG2E).astype(BF16)
        out = None
        for hh in (0, 1):
            var = 2 * half + hh
            s = _dot_nt(q2, ks[var])
            if mask is not None:
                s = jnp.where(mask, s, NEG_INF)
            m = jnp.max(s, axis=-1, keepdims=True)
            if sink_ref is not None:
                snk = LOG2E * jnp.where(first, sink_ref[2 * half, hh:hh + 1, 0:1],
                                        sink_ref[2 * half + 1, hh:hh + 1, 0:1])
                m = jnp.maximum(m, snk)
            acc = _dot(jnp.exp2((s - m).astype(BF16)), vs[var])
            ones_lane = HEAD_DIM if hh == 0 else 0
            den = acc[:, ones_lane:ones_lane + 1]
            if sink_ref is not None:
                den = den + jnp.exp2(snk - m)
            own = (lane < HEAD_DIM) if hh == 0 else (lane >= HEAD_DIM)
            o = jnp.where(own, acc, 0.0) * (1.0 / den)
            out = o if out is None else out + o
        o_ref[:, (2 * half) * LANES:(2 * half + 1) * LANES] = out[:tq]
        o_ref[:, (2 * half + 1) * LANES:(2 * half + 2) * LANES] = out[tq:]


def _full_attn_kernel(q_ref, k_ref, v_ref, *rest, has_sink):
    if has_sink:
        sink_ref, o_ref = rest
    else:
        sink_ref, (o_ref,) = None, rest
    _attn_core(q_ref[...], k_ref[...], v_ref[...], sink_ref, o_ref, None)


def _full_attention(q_arr, q_rb0, q_cb0, k_arr, k_rb0, k_cb0, v_arr, v_rb0, v_cb0,
                    n_seq, lq, lk, tq, n_groups, sink=None):
    nq = lq // tq
    qw = 4 * LANES
    in_specs = [
        pl.BlockSpec((tq, qw), lambda b, g, i: (q_rb0 + b * nq + i, q_cb0 + g)),
        pl.BlockSpec((lk, LANES), lambda b, g, i: (k_rb0 + b, k_cb0 + g)),
        pl.BlockSpec((lk, LANES), lambda b, g, i: (v_rb0 + b, v_cb0 + g)),
    ]
    args = [q_arr, k_arr, v_arr]
    if sink is not None:
        in_specs.append(pl.BlockSpec((None, 4, 2, LANES), lambda b, g, i: (g, 0, 0, 0)))
        args.append(sink)
    return pl.pallas_call(
        functools.partial(_full_attn_kernel, has_sink=sink is not None),
        grid=(n_seq, n_groups, nq),
        in_specs=in_specs,
        out_specs=pl.BlockSpec((tq, qw), lambda b, g, i: (b * nq + i, g)),
        out_shape=jax.ShapeDtypeStruct((n_seq * lq, n_groups * qw), F32),
        compiler_params=_cparams(("arbitrary", "arbitrary", "arbitrary")),
        name="full_attention",
    )(*args)


def _latent_attn_kernel(q_ref, k_ref, v_ref, kx_ref, vx_ref, o_ref, kvar, vvar, *, tq, kb):
    lk = DEC_SEQ + PAST_LEN

    @pl.when(pl.program_id(2) == 0)
    def _():
        for src_k, src_v, r0, nrows in ((k_ref, v_ref, 0, DEC_SEQ), (kx_ref, vx_ref, DEC_SEQ, PAST_LEN)):
            for c in range(nrows // kb):
                src = pl.ds(c * kb, kb)
                dst = pl.ds(r0 + c * kb, kb)
                ks, vs = _kv_variants(src_k[src, :], src_v[src, :])
                for i in range(4):
                    kvar[i, dst, :] = ks[i]
                    vvar[i, dst, :] = vs[i]

    lane = lax.broadcasted_iota(jnp.int32, (2 * tq, LANES), 1)
    for half in (0, 1):
        q2 = jnp.concatenate([q_ref[:, (2 * half) * LANES:(2 * half + 1) * LANES],
                              q_ref[:, (2 * half + 1) * LANES:(2 * half + 2) * LANES]], axis=0)
        q2 = (q2 * LOG2E).astype(BF16)
        out = None
        for hh in (0, 1):
            var = 2 * half + hh
            m = jnp.full((2 * tq, 1), -jnp.inf, F32)
            acc = jnp.zeros((2 * tq, LANES), F32)
            for j in range(lk // kb):
                rows = pl.ds(j * kb, kb)
                s = _dot_nt(q2, kvar[var, rows, :])
                m_new = jnp.maximum(m, jnp.max(s, axis=-1, keepdims=True))
                e = jnp.exp2((s - m_new).astype(BF16))
                acc = jnp.exp2(m - m_new) * acc + _dot(e, vvar[var, rows, :])
                m = m_new
            own = (lane < HEAD_DIM) if hh == 0 else (lane >= HEAD_DIM)
            ones_lane = HEAD_DIM if hh == 0 else 0
            o = jnp.where(own, acc, 0.0) * (1.0 / acc[:, ones_lane:ones_lane + 1])
            out = o if out is None else out + o
        o_ref[:, (2 * half) * LANES:(2 * half + 1) * LANES] = out[:tq]
        o_ref[:, (2 * half + 1) * LANES:(2 * half + 2) * LANES] = out[tq:]


def _latent_attention(p, kx, vx):
    tq, kb = 256, 512
    nq = DEC_SEQ // tq
    ngr = ATT_KV_HEADS // 2
    kcb = ATT_QW // LANES
    rb_q = T_PROMPT // tq
    rb_k = T_PROMPT // DEC_SEQ
    lk = DEC_SEQ + PAST_LEN
    ctx = pl.BlockSpec((None, PAST_LEN, LANES), lambda b, g, i: (b, 0, g))
    return pl.pallas_call(
        functools.partial(_latent_attn_kernel, tq=tq, kb=kb),
        grid=(DEC_BATCH, ngr, nq),
        in_specs=[pl.BlockSpec((tq, 4 * LANES), lambda b, g, i: (rb_q + b * nq + i, g)),
                  pl.BlockSpec((DEC_SEQ, LANES), lambda b, g, i: (rb_k + b, kcb + g)),
                  pl.BlockSpec((DEC_SEQ, LANES), lambda b, g, i: (rb_k + b, kcb + ngr + g)),
                  ctx, ctx],
        out_specs=pl.BlockSpec((tq, 4 * LANES), lambda b, g, i: (b * nq + i, g)),
        out_shape=jax.ShapeDtypeStruct((T_SAMPLE, ATT_QW), F32),
        scratch_shapes=[pltpu.VMEM((4, lk, LANES), BF16), pltpu.VMEM((4, lk, LANES), BF16)],
        compiler_params=_cparams(("arbitrary", "arbitrary", "arbitrary")),
        name="latent_attention",
    )(p, p, p, kx, vx)


WIN_TQ = 2 * BLK
WIN_KBLKS = WIN_TQ // BLK + 2


def _window_attn_kernel(q_ref, *refs):
    k_refs = refs[:WIN_KBLKS]
    v_refs = refs[WIN_KBLKS:2 * WIN_KBLKS]
    kx_ref, vx_ref, sink_ref, o_ref = refs[2 * WIN_KBLKS:]
    n = pl.program_id(1)
    k = jnp.concatenate([r[...] for r in k_refs] + [kx_ref[...]], axis=0)
    v = jnp.concatenate([r[...] for r in v_refs] + [vx_ref[...]], axis=0)
    nloc = WIN_KBLKS * BLK
    shape = (WIN_TQ, nloc + PAST_LEN)
    i = lax.broadcasted_iota(jnp.int32, shape, 0)
    r = lax.broadcasted_iota(jnp.int32, shape, 1)
    kpos = n * WIN_TQ - BLK + r
    local = (jnp.abs(r - BLK - i) <= WINDOW) & (kpos >= 0) & (kpos < DEC_SEQ)
    mask = local | (r >= nloc)
    _attn_core(q_ref[...], k, v, sink_ref, o_ref, mask)


def _window_attention(p, kx, vx, sink):
    nb = DEC_SEQ // BLK
    nq = DEC_SEQ // WIN_TQ
    rb0 = T_PROMPT // BLK
    qcb = (4 * RET_QW) // (4 * LANES)
    kcb = (4 * RET_QW + WIN_QW) // LANES
    vcb = kcb + 1

    def kv(cb, off):
        return pl.BlockSpec((BLK, LANES),
                            lambda b, n: (rb0 + b * nb + jnp.clip(n * (WIN_TQ // BLK) + off, 0, nb - 1), cb))

    offs = range(-1, WIN_KBLKS - 1)
    ctx = pl.BlockSpec((None, PAST_LEN, LANES), lambda b, n: (b, 0, 0))
    qspec = pl.BlockSpec((WIN_TQ, 4 * LANES), lambda b, n: (T_PROMPT // WIN_TQ + b * nq + n, qcb))
    return pl.pallas_call(
        _window_attn_kernel,
        grid=(DEC_BATCH, nq),
        in_specs=[qspec] + [kv(kcb, o) for o in offs] + [kv(vcb, o) for o in offs] + [
            ctx, ctx, pl.BlockSpec((4, 2, LANES), lambda b, n: (0, 0, 0))],
        out_specs=pl.BlockSpec((WIN_TQ, 4 * LANES), lambda b, n: (b * nq + n, 0)),
        out_shape=jax.ShapeDtypeStruct((T_SAMPLE, WIN_QW), F32),
        compiler_params=_cparams(("arbitrary", "arbitrary")),
        name="window_attention",
    )(*([p] * (1 + 2 * WIN_KBLKS)), kx, vx, sink)


def _sink_table(sink, n_groups):
    s = sink.astype(F32).reshape(n_groups, 4, 2, 1)
    return jnp.broadcast_to(s, (n_groups, 4, 2, LANES))


def _cache_rows(cache):
    b, h, l, d = cache.shape
    return cache.astype(F32).transpose(0, 2, 1, 3).reshape(b, l, h * d)


def _outproj_kernel(*refs, n_in):
    a_refs = refs[:2 * n_in]
    w_refs = refs[2 * n_in:3 * n_in]
    xp_ref, xs_ref, mod_ref, g_ref, wr_ref, xn_ref, hfp_ref, hfs_ref, lg_ref = refs[3 * n_in:]
    i = pl.program_id(0)
    y = None
    for k, w_ref in enumerate(w_refs):
        a = _pick(a_refs[2 * k], a_refs[2 * k + 1])
        t = _dot(a.astype(BF16), w_ref[...])
        y = t if y is None else y + t
    xn = _pick(xp_ref, xs_ref) + mod_ref[2:3, :] * _rms(y, g_ref[1:2, :])
    hf = _rms(xn, g_ref[2:3, :]) * (1.0 + mod_ref[4:5, :]) + mod_ref[3:4, :]
    xn_ref[...] = xn
    lg_ref[...] = _dot_nt(wr_ref[...], hf.astype(BF16))

    @pl.when(i < NPT)
    def _():
        hfp_ref[...] = hf

    @pl.when(i >= NPT)
    def _():
        hfs_ref[...] = hf.reshape(TM, D_MODEL // LANES, LANES)


def _outproj(parts, w_parts, x, mod, layer, gains, w_router_t):
    n_in = len(parts)
    in_specs = []
    args = []
    for a_p, a_s in parts:
        in_specs += _pair_specs(a_p.shape[1])
        args += [a_p, a_s]
    in_specs += [pl.BlockSpec(w.shape, lambda i: (0, 0)) for w in w_parts]
    in_specs += _pair_specs(D_MODEL) + [
        pl.BlockSpec((None, None, 6, D_MODEL), lambda i: (layer, _mod_index(i), 0, 0)),
        pl.BlockSpec((4, D_MODEL), lambda i: (0, 0)),
        pl.BlockSpec((N_EXPERTS, D_MODEL), lambda i: (0, 0)),
    ]
    row = pl.BlockSpec((TM, D_MODEL), lambda i: (i, 0))
    hf_p, _ = _pair_specs(D_MODEL)
    hf_s = pl.BlockSpec((TM, D_MODEL // LANES, LANES), lambda i: (jnp.maximum(i - NPT, 0), 0, 0))
    return pl.pallas_call(
        functools.partial(_outproj_kernel, n_in=n_in),
        grid=(T_ALL // TM,),
        in_specs=in_specs,
        out_specs=[row, hf_p, hf_s, pl.BlockSpec((N_EXPERTS, TM), lambda i: (0, i))],
        out_shape=[jax.ShapeDtypeStruct((T_ALL, D_MODEL), F32),
                   jax.ShapeDtypeStruct((T_PROMPT, D_MODEL), F32),
                   jax.ShapeDtypeStruct((T_SAMPLE, D_MODEL // LANES, LANES), F32),
                   jax.ShapeDtypeStruct((N_EXPERTS, T_ALL), F32)],
        compiler_params=_cparams(("arbitrary",)),
        name="outproj",
    )(*args, *w_parts, x[0], x[1], mod, gains, w_router_t)


TILE_SUB = D_MODEL // LANES


def _gather_kernel(idx_ref, h_ref, o_ref, buf, *, eg, cap):
    b = pl.program_id(0)
    g = pl.program_id(1)
    for e in range(eg):
        base = (b * N_EXPERTS + g * eg + e) * cap

        def body(c, carry):
            buf[c] = h_ref[idx_ref[base + c]]
            return carry

        lax.fori_loop(0, cap, body, 0, unroll=8)
        o_ref[e] = buf[...].reshape(cap, D_MODEL).astype(BF16)


def _gather(h3, idx, n_seq, seq_len, cap, eg):
    return pl.pallas_call(
        functools.partial(_gather_kernel, eg=eg, cap=cap),
        grid_spec=pltpu.PrefetchScalarGridSpec(
            num_scalar_prefetch=1,
            grid=(n_seq, N_EXPERTS // eg),
            in_specs=[pl.BlockSpec((seq_len, TILE_SUB, LANES), lambda b, g, idx: (b, 0, 0))],
            out_specs=pl.BlockSpec((eg, cap, D_MODEL), lambda b, g, idx: (g, b, 0)),
            scratch_shapes=[pltpu.VMEM((cap, TILE_SUB, LANES), F32)],
        ),
        out_shape=jax.ShapeDtypeStruct((N_EXPERTS, n_seq * cap, D_MODEL), BF16),
        compiler_params=_cparams(("arbitrary", "arbitrary")),
        name="moe_gather",
    )(idx, h3)


def _moe_kernel(xp_ref, xs_ref, wg_ref, wu_ref, wd_ref, gwp_ref, gws_ref, yp_ref, ys_ref, hid_scr, wd_scr):
    f = pl.program_id(1)
    wg = wg_ref[...].astype(BF16)
    wu = wu_ref[...].astype(BF16)
    cols = pl.ds(pl.multiple_of(f * TF, TF), TF)
    wd_scr[cols, :] = wd_ref[...].astype(BF16)
    for h, x_ref in enumerate((xp_ref, xs_ref)):
        x = x_ref[...]
        hid_scr[h, :, cols] = (_silu(_dot(x, wg)) * _dot(x, wu)).astype(BF16)

    @pl.when(f == pl.num_programs(1) - 1)
    def _():
        for h, (gw_ref, y_ref) in enumerate(((gwp_ref, yp_ref), (gws_ref, ys_ref))):
            for r in range(0, hid_scr.shape[1], MOE_ROW_TILE):
                rows = slice(r, r + MOE_ROW_TILE)
                y_ref[rows, :] = _dot(hid_scr[h, rows, :], wd_scr[...]) * gw_ref[rows, :]


MOE_ROW_TILE = 512


def _moe_ffn(xg_p, xg_s, w_gate, w_up, w_down, layer, gw_p, gw_s):
    assert ROWS_P == ROWS_S
    nf = EXPERT_FF // TF
    xspec_p = pl.BlockSpec((None, ROWS_P, D_MODEL), lambda e, f: (e, 0, 0))
    xspec_s = pl.BlockSpec((None, ROWS_S, D_MODEL), lambda e, f: (e, 0, 0))
    return pl.pallas_call(
        _moe_kernel,
        grid=(N_EXPERTS, nf),
        in_specs=[xspec_p, xspec_s,
                  pl.BlockSpec((None, None, D_MODEL, TF), lambda e, f: (layer, e, 0, f)),
                  pl.BlockSpec((None, None, D_MODEL, TF), lambda e, f: (layer, e, 0, f)),
                  pl.BlockSpec((None, None, TF, D_MODEL), lambda e, f: (layer, e, f, 0)),
                  pl.BlockSpec((None, ROWS_P, 1), lambda e, f: (e, 0, 0)),
                  pl.BlockSpec((None, ROWS_S, 1), lambda e, f: (e, 0, 0))],
        out_specs=[xspec_p, xspec_s],
        out_shape=[jax.ShapeDtypeStruct((N_EXPERTS, ROWS_P, D_MODEL), F32),
                   jax.ShapeDtypeStruct((N_EXPERTS, ROWS_S, D_MODEL), F32)],
        scratch_shapes=[pltpu.VMEM((2, ROWS_P, EXPERT_FF), BF16),
                        pltpu.VMEM((EXPERT_FF, D_MODEL), BF16)],
        compiler_params=_cparams(("arbitrary", "arbitrary")),
        name="moe_ffn",
    )(xg_p, xg_s, w_gate, w_up, w_down, gw_p, gw_s)


COMBINE_GROUP = 8


def _combine_kernel(idx_ref, y_ref, o_ref, y3, *, eg, cap):
    b = pl.program_id(0)
    g = pl.program_id(1)

    @pl.when(g == 0)
    def _():
        o_ref[...] = jnp.zeros_like(o_ref)

    for e in range(eg):
        base = (b * N_EXPERTS + g * eg + e) * cap
        y3[...] = y_ref[e].reshape(cap, TILE_SUB, LANES)

        def body(c, carry):
            c0 = c * COMBINE_GROUP
            toks = [idx_ref[base + c0 + k] for k in range(COMBINE_GROUP)]
            rows = [o_ref[t] + y3[c0 + k] for k, t in enumerate(toks)]
            for t, row in zip(toks, rows):
                o_ref[t] = row
            return carry

        lax.fori_loop(0, cap // COMBINE_GROUP, body, 0)


def _combine(y, idx, n_seq, seq_len, cap, eg):
    return pl.pallas_call(
        functools.partial(_combine_kernel, eg=eg, cap=cap),
        grid_spec=pltpu.PrefetchScalarGridSpec(
            num_scalar_prefetch=1,
            grid=(n_seq, N_EXPERTS // eg),
            in_specs=[pl.BlockSpec((eg, cap, D_MODEL), lambda b, g, idx: (g, b, 0))],
            out_specs=pl.BlockSpec((seq_len, TILE_SUB, LANES), lambda b, g, idx: (b, 0, 0)),
            scratch_shapes=[pltpu.VMEM((cap, TILE_SUB, LANES), F32)],
        ),
        out_shape=jax.ShapeDtypeStruct((n_seq * seq_len, TILE_SUB, LANES), F32),
        compiler_params=_cparams(("arbitrary", "arbitrary")),
        name="moe_combine",
    )(idx, y)


def _onehot_t(idx_ref, n_tok):
    n_slots = idx_ref.shape[-1]
    tok = lax.broadcasted_iota(jnp.int32, (n_tok, n_slots), 0)
    return jnp.where(tok == idx_ref[...], 1.0, 0.0)


def _gather_mm_kernel(idx_ref, h_ref, o_ref, *, cap):
    sel = _onehot_t(idx_ref, h_ref.shape[0]).T.astype(BF16)
    rows = _dot(sel, h_ref[...].astype(BF16)).astype(BF16)
    for e in range(N_EXPERTS):
        o_ref[e] = rows[e * cap:(e + 1) * cap]


def _gather_mm(h, idx, n_seq, seq_len, cap):
    return pl.pallas_call(
        functools.partial(_gather_mm_kernel, cap=cap),
        grid=(n_seq,),
        in_specs=[pl.BlockSpec((None, 1, N_EXPERTS * cap), lambda b: (b, 0, 0)),
                  pl.BlockSpec((seq_len, D_MODEL), lambda b: (b, 0))],
        out_specs=pl.BlockSpec((N_EXPERTS, cap, D_MODEL), lambda b: (0, b, 0)),
        out_shape=jax.ShapeDtypeStruct((N_EXPERTS, n_seq * cap, D_MODEL), BF16),
        compiler_params=_cparams(("arbitrary",)),
        name="moe_gather_mm",
    )(idx, h)


def _combine_mm_kernel(idx_ref, y_ref, o_ref):
    sel = _onehot_t(idx_ref, o_ref.shape[0]).astype(BF16)
    y = jnp.concatenate([y_ref[e] for e in range(N_EXPERTS)], axis=0)
    hi = y.astype(BF16)
    lo = (y - hi.astype(F32)).astype(BF16)
    o_ref[...] = _dot(sel, hi) + _dot(sel, lo)


def _combine_mm(y, idx, n_seq, seq_len, cap):
    return pl.pallas_call(
        _combine_mm_kernel,
        grid=(n_seq,),
        in_specs=[pl.BlockSpec((None, 1, N_EXPERTS * cap), lambda b: (b, 0, 0)),
                  pl.BlockSpec((N_EXPERTS, cap, D_MODEL), lambda b: (0, b, 0))],
        out_specs=pl.BlockSpec((seq_len, D_MODEL), lambda b: (b, 0)),
        out_shape=jax.ShapeDtypeStruct((n_seq * seq_len, D_MODEL), F32),
        compiler_params=_cparams(("arbitrary",)),
        name="moe_combine_mm",
    )(idx, y)


def _postnorm_kernel(x_ref, yp_ref, ys_ref, mod_ref, g_ref, op_ref, os_ref):
    i = pl.program_id(0)
    y = jnp.where(i < NPT, yp_ref[...], ys_ref[...].reshape(TM, D_MODEL))
    o = x_ref[...] + mod_ref[5:6, :] * _rms(y, g_ref[3:4, :])

    @pl.when(i < NPT)
    def _():
        op_ref[...] = o

    @pl.when(i >= NPT)
    def _():
        os_ref[...] = o


def _postnorm(x, y_p, y_s, mod, layer, gains):
    return pl.pallas_call(
        _postnorm_kernel,
        grid=(T_ALL // TM,),
        in_specs=[pl.BlockSpec((TM, D_MODEL), lambda i: (i, 0)), _pair_specs(D_MODEL)[0],
                  pl.BlockSpec((TM, TILE_SUB, LANES), lambda i: (jnp.maximum(i - NPT, 0), 0, 0)),
                  pl.BlockSpec((None, None, 6, D_MODEL), lambda i: (layer, _mod_index(i), 0, 0)),
            pl.BlockSpec((4, D_MODEL), lambda i: (0, 0))],
        out_specs=_pair_specs(D_MODEL),
        out_shape=[jax.ShapeDtypeStruct((T_PROMPT, D_MODEL), F32),
                   jax.ShapeDtypeStruct((T_SAMPLE, D_MODEL), F32)],
        compiler_params=_cparams(("arbitrary",)),
        name="postnorm",
    )(x, y_p, y_s, mod, gains)


def _excl_prefix(mask):
    r, n = mask.shape
    i0 = lax.broadcasted_iota(jnp.int32, (LANES, LANES), 0)
    i1 = lax.broadcasted_iota(jnp.int32, (LANES, LANES), 1)
    upper = jnp.where(i0 < i1, 1.0, 0.0).astype(BF16)
    ones = jnp.where(mask, 1.0, 0.0)
    carry = jnp.zeros((r, 1), F32)
    out = []
    for c in range(n // LANES):
        ch = ones[:, c * LANES:(c + 1) * LANES]
        out.append(_dot(ch.astype(BF16), upper) + carry)
        carry = carry + jnp.sum(ch, axis=1, keepdims=True)
    return jnp.concatenate(out, axis=1)


def _route_kernel(lg_ref, idx_ref, gw_ref, *, nb, n, cap):
    affs = []
    for s in range(nb):
        l = lg_ref[:, s * n:(s + 1) * n]
        e = jnp.exp(l - jnp.max(l, axis=0, keepdims=True))
        affs.append(e / jnp.sum(e, axis=0, keepdims=True))
    aff = affs[0] if nb == 1 else jnp.concatenate(affs, axis=0)
    rows = nb * N_EXPERTS
    bits = jnp.zeros((rows, 1), jnp.int32)
    for bit in range(30, -1, -1):
        cand = bits | (1 << bit)
        cnt = jnp.sum(jnp.where(aff >= pltpu.bitcast(cand, F32), 1.0, 0.0), axis=1, keepdims=True)
        bits = jnp.where(cnt >= cap, cand, bits)
    thr = pltpu.bitcast(bits, F32)
    gt = aff > thr
    eq = aff == thr
    need = cap - jnp.sum(jnp.where(gt, 1.0, 0.0), axis=1, keepdims=True)
    sel = gt | (eq & (_excl_prefix(eq) < need))
    pos = jnp.where(sel, _excl_prefix(sel), -1.0)
    tok = lax.broadcasted_iota(jnp.int32, (rows, n), 1).astype(F32)
    capw = max(cap, LANES)
    slot_lane = lax.broadcasted_iota(jnp.int32, (rows, capw), 1)

    def slot(s, carry):
        idx_acc, gw_acc = carry
        hit = pos == lax.convert_element_type(s, F32)
        ic = jnp.sum(jnp.where(hit, tok, 0.0), axis=1, keepdims=True)
        gc = jnp.sum(jnp.where(hit, aff, 0.0), axis=1, keepdims=True)
        here = slot_lane == s
        return jnp.where(here, ic, idx_acc), jnp.where(here, gc, gw_acc)

    zero = jnp.zeros((rows, capw), F32)
    idx_acc, gw_acc = lax.fori_loop(0, cap, slot, (zero, zero), unroll=8)
    for s in range(nb):
        r = slice(s * N_EXPERTS, (s + 1) * N_EXPERTS)
        idx_ref[s] = idx_acc[r, :cap].astype(jnp.int32)
        gw_ref[s] = gw_acc[r, :cap]


def _route(logits_t, n_seq, seq_len, cap, col0, nb):
    cb0 = col0 // (nb * seq_len)
    out = pl.BlockSpec((nb, N_EXPERTS, cap), lambda b: (b, 0, 0))
    return pl.pallas_call(
        functools.partial(_route_kernel, nb=nb, n=seq_len, cap=cap),
        grid=(n_seq // nb,),
        in_specs=[pl.BlockSpec((N_EXPERTS, nb * seq_len), lambda b: (0, cb0 + b))],
        out_specs=[out, out],
        out_shape=[jax.ShapeDtypeStruct((n_seq, N_EXPERTS, cap), jnp.int32),
                   jax.ShapeDtypeStruct((n_seq, N_EXPERTS, cap), F32)],
        compiler_params=_cparams(("arbitrary",)),
        name="route",
    )(logits_t)


def _expert_ffn(hf_p, hf_s, logits_t, w_gate, w_up, w_down, layer):
    idx_p, gw_p = _route(logits_t, BATCH, SEQ, CAP_P, 0, 8)
    idx_s, gw_s = _route(logits_t, DEC_BATCH, DEC_SEQ, CAP_S, T_PROMPT, 1)
    idx_p = idx_p.reshape(BATCH, 1, N_EXPERTS * CAP_P)
    idx_s = idx_s.reshape(DEC_BATCH * N_EXPERTS * CAP_S)
    xg_p = _gather_mm(hf_p, idx_p, BATCH, SEQ, CAP_P)
    xg_s = _gather(hf_s, idx_s, DEC_BATCH, DEC_SEQ, CAP_S, 4)
    gw_p = gw_p.transpose(1, 0, 2).reshape(N_EXPERTS, ROWS_P, 1)
    gw_s = gw_s.transpose(1, 0, 2).reshape(N_EXPERTS, ROWS_S, 1)
    y_p, y_s = _moe_ffn(xg_p, xg_s, w_gate, w_up, w_down, layer, gw_p, gw_s)
    o_p = _combine_mm(y_p, idx_p, BATCH, SEQ, CAP_P)
    o_s = _combine(y_s, idx_s, DEC_BATCH, DEC_SEQ, CAP_S, 2)
    return o_p, o_s


def _rope_tables():
    t = jnp.arange(DEC_SEQ)
    row = (t // GRID_W).astype(F32)
    col = (t % GRID_W).astype(F32)
    inv = ROPE_THETA ** (-jnp.arange(0, AXIS_DIM, 2, dtype=F32) / AXIS_DIM)
    ar = row[:, None] * inv[None]
    ac = col[:, None] * inv[None]
    ang = jnp.concatenate([ar, ar, ac, ac], axis=-1)
    ang = jnp.concatenate([ang, ang], axis=-1)
    cos = jnp.concatenate([jnp.ones((TM, LANES), F32), jnp.cos(ang)], axis=0)
    sin = jnp.concatenate([jnp.zeros((TM, LANES), F32), jnp.sin(ang)], axis=0)
    return cos, sin


def _split_kv_heads(rows, n_heads):
    return rows.reshape(BATCH, SEQ, n_heads, HEAD_DIM).transpose(0, 2, 1, 3)[:, None]


def kernel(x_prompt, x_sample, state_ret_fwd, state_ret_bwd, cache_win_k, cache_win_v, cache_attn_k, cache_attn_v, c, c_ctx, w_ada, b_ada, norm_gains, w_in_even, w_out_even, ret_decay_fwd, ret_decay_bwd, win_sink, w_in_odd, w_out_odd, q_norm, k_norm, w_router, w_gate, w_up, w_down):
    x = (x_prompt.reshape(T_PROMPT, D_MODEL), x_sample.reshape(T_SAMPLE, D_MODEL))
    cond =jnp.concatenate([c_ctx[None, :], c, jnp.zeros((8 - 1 - DEC_BATCH, D_MODEL), F32)], axis=0)
    mod = _modulation(cond, w_ada, b_ada).reshape(DEPTH, 8, 6, D_MODEL)
    cos_t, sin_t = _rope_tables()
    outs = {}
    for layer in range(DEPTH):
        j = layer // 2
        gains = norm_gains[layer]
        if layer % 2 == 0:
            p = _inproj(x, mod, layer, gains[0], w_in_even[j].astype(BF16), cos_t, sin_t)
            tabs = _retention_tables(ret_decay_fwd[j], ret_decay_bwd[j])
            zero = jnp.zeros((BATCH, RET_HEADS // 2, LANES, LANES), F32)
            ret_p, sf, sb = _retention(p, tabs, zero, zero, BATCH, SEQ, 0, 4)
            ret_s, _, _ = _retention(p, tabs, _pair_states(state_ret_fwd[:, j]),
                                     _pair_states(state_ret_bwd[:, j]), DEC_BATCH, DEC_SEQ, T_PROMPT, 2)
            sink = _sink_table(win_sink[j], 1)
            qcb = (4 * RET_QW) // (4 * LANES)
            kcb = (4 * RET_QW + WIN_QW) // LANES
            win_p = _full_attention(p, 0, qcb, p, 0, kcb, p, 0, kcb + 1,
                                    BATCH, SEQ, SEQ, SEQ, 1, sink=sink)
            win_s = _window_attention(p, _cache_rows(cache_win_k[:, j]), _cache_rows(cache_win_v[:, j]),
                                      sink[0])
            w_out = w_out_even[j].astype(BF16)
            parts = [(ret_p, ret_s), (win_p, win_s)]
            w_parts = [w_out[:RET_QW], w_out[RET_QW:]]
            outs["ret_f"] = _unpair_states(sf)[:, None]
            outs["ret_b"] = _unpair_states(sb)[:, None]
            c0 = 4 * RET_QW + WIN_QW
            outs["win_k"] = _split_kv_heads(p[:T_PROMPT, c0:c0 + WIN_KW], WIN_KV_HEADS)
            outs["win_v"] = _split_kv_heads(p[:T_PROMPT, c0 + WIN_KW:c0 + 2 * WIN_KW], WIN_KV_HEADS)
        else:
            p = _inproj(x, mod, layer, gains[0], w_in_odd[j].astype(BF16), cos_t, sin_t,
                        qn=q_norm[j], kn=k_norm[j])
            ngr = ATT_KV_HEADS // 2
            kcb = ATT_QW // LANES
            att_p = _full_attention(p, 0, 0, p, 0, kcb, p, 0, kcb + ngr,
                                    BATCH, SEQ, SEQ, SEQ, ngr)
            att_s = _latent_attention(p, _cache_rows(cache_attn_k[:, j]), _cache_rows(cache_attn_v[:, j]))
            parts = [(att_p, att_s)]
            w_parts = [w_out_odd[j].astype(BF16)]
            outs["att_k"] = _split_kv_heads(p[:T_PROMPT, ATT_QW:ATT_QW + ATT_KW], ATT_KV_HEADS)
            outs["att_v"] = _split_kv_heads(p[:T_PROMPT, ATT_QW + ATT_KW:], ATT_KV_HEADS)
        xn, hf_p, hf_s, logits_t = _outproj(parts, w_parts, x, mod, layer, gains,
                                            w_router[layer].T.astype(BF16))
        o_p, o_s = _expert_ffn(hf_p, hf_s, logits_t, w_gate, w_up, w_down, layer)
        x = _postnorm(xn, o_p, o_s, mod, layer, gains)
    y_prompt = x[0].reshape(BATCH, SEQ, D_MODEL)
    y_sample = x[1].reshape(DEC_BATCH, DEC_SEQ, D_MODEL)
    return (y_prompt, y_sample, outs["ret_f"], outs["ret_b"], outs["win_k"], outs["win_v"],
            outs["att_k"], outs["att_v"])
```

```python
import functools

import jax
import jax.numpy as jnp
import numpy as np
from jax import lax
from jax.experimental import pallas as pl
from jax.experimental.pallas import tpu as pltpu

D_MODEL = 1024
BATCH = 32
SEQ = 256
DEPTH = 2
DEC_BATCH = 2
DEC_SEQ = 4096
PAST_LEN = 512
GRID_W = 64
HEAD_DIM = 64
AXIS_DIM = HEAD_DIM // 2
ROPE_THETA = 10000.0
BLK = 128
WINDOW = 128
RET_HEADS = D_MODEL // 128
WIN_HEADS = D_MODEL // 128
WIN_KV_HEADS = WIN_HEADS // 4
ATT_HEADS = D_MODEL // HEAD_DIM
ATT_KV_HEADS = ATT_HEADS // 4
RET_QW = RET_HEADS * HEAD_DIM
WIN_QW = WIN_HEADS * HEAD_DIM
WIN_KW = WIN_KV_HEADS * HEAD_DIM
EVEN_IN = 4 * RET_QW + WIN_QW + 2 * WIN_KW
ATT_QW = ATT_HEADS * HEAD_DIM
ATT_KW = ATT_KV_HEADS * HEAD_DIM
ODD_IN = ATT_QW + 2 * ATT_KW
N_EXPERTS = 16
CAPACITY_FACTOR = 2
EXPERT_FF = ((8 * D_MODEL // 3 + 127) // 128) * 128
EPS = 1e-6
NEG_INF = -1e30
F32 = jnp.float32
BF16 = jnp.bfloat16

LANES = 128
T_PROMPT = BATCH * SEQ
T_SAMPLE = DEC_BATCH * DEC_SEQ
T_ALL = T_PROMPT + T_SAMPLE
TM = 512
CAP_P = CAPACITY_FACTOR * SEQ // N_EXPERTS
CAP_S = CAPACITY_FACTOR * DEC_SEQ // N_EXPERTS
ROWS_P = BATCH * CAP_P
ROWS_S = DEC_BATCH * CAP_S
TF = 256
VMEM_LIMIT = 56 * 1024 * 1024
QK_SCALE = HEAD_DIM ** -0.5


def _cparams(sem):
    return pltpu.CompilerParams(dimension_semantics=sem, vmem_limit_bytes=VMEM_LIMIT)


def _silu(x):
    return x * (1.0 / (1.0 + jnp.exp(-x)))


def _dot(a, b):
    return jnp.dot(a, b, preferred_element_type=F32)


def _dot_nt(a, b):
    return lax.dot_general(a, b, (((1,), (1,)), ((), ())), preferred_element_type=F32)


def _rms(x, g):
    return x * lax.rsqrt(jnp.mean(x * x, axis=-1, keepdims=True) + EPS) * g


def _pair_block_diag(shape):
    r = lax.broadcasted_iota(jnp.int32, shape, 0)
    c = lax.broadcasted_iota(jnp.int32, shape, 1)
    return (r // HEAD_DIM) == (c // HEAD_DIM)


def _head_sum(x, bd):
    hi = x.astype(BF16)
    lo = (x - hi.astype(F32)).astype(BF16)
    return _dot(hi, bd) + _dot(lo, bd)


def _rope(xb, cos, sin):
    lane = lax.broadcasted_iota(jnp.int32, xb.shape, 1)
    half = AXIS_DIM // 2
    rot = jnp.where((lane % AXIS_DIM) < half,
                    -pltpu.roll(xb, LANES - half, 1), pltpu.roll(xb, half, 1))
    return xb * cos + rot * sin


def _mod_kernel(c_ref, w_ref, b_ref, o_ref):
    o_ref[...] = _dot(_silu(c_ref[...]), w_ref[...]) + b_ref[...]


def _modulation(cond, w_ada, b_ada):
    tn = 1536
    n = 6 * D_MODEL
    return pl.pallas_call(
        _mod_kernel,
        grid=(DEPTH, n // tn),
        in_specs=[
            pl.BlockSpec((8, D_MODEL), lambda l, j: (0, 0)),
            pl.BlockSpec((None, D_MODEL, tn), lambda l, j: (l, 0, j)),
            pl.BlockSpec((None, 1, tn), lambda l, j: (l, 0, j)),
        ],
        out_specs=pl.BlockSpec((None, 8, tn), lambda l, j: (l, 0, j)),
        out_shape=jax.ShapeDtypeStruct((DEPTH, 8, n), F32),
        compiler_params=_cparams(("arbitrary", "arbitrary")),
        name="modulation",
    )(cond, w_ada, b_ada.reshape(DEPTH, 1, n))


def _mod_index(i):
    npt = T_PROMPT // TM
    return jnp.where(i < npt, 0, 1 + (i - npt) // (DEC_SEQ // TM))


def _rope_index(i):
    npt = T_PROMPT // TM
    return jnp.where(i < npt, 0, 1 + (i - npt) % (DEC_SEQ // TM))


NPT = T_PROMPT // TM


def _pair_specs(width):
    return [pl.BlockSpec((TM, width), lambda i: (jnp.minimum(i, NPT - 1), 0)),
            pl.BlockSpec((TM, width), lambda i: (jnp.maximum(i - NPT, 0), 0))]


def _pick(p_ref, s_ref):
    return jnp.where(pl.program_id(0) < NPT, p_ref[...], s_ref[...])


def _inproj_kernel(xp_ref, xs_ref, mod_ref, g_ref, w_ref, cos_ref, sin_ref, *rest, even):
    if even:
        (o_ref,) = rest
    else:
        qn_ref, kn_ref, o_ref = rest
    h = _rms(_pick(xp_ref, xs_ref), g_ref[...]) * (1.0 + mod_ref[1:2, :]) + mod_ref[0:1, :]
    h16 = h.astype(BF16)
    cos = cos_ref[...]
    sin = sin_ref[...]
    nblk = w_ref.shape[1] // LANES
    if even:
        nq = RET_QW // LANES
        rope_blocks = set(range(0, 2 * nq)) | set(range(4 * nq, 5 * nq + 1))
        scaled = set(range(nq, 2 * nq)) | set(range(4 * nq, 5 * nq))
        normed = {}
    else:
        nq = ATT_QW // LANES
        nk = ATT_KW // LANES
        rope_blocks = set(range(0, nq + nk))
        scaled = set(range(0, nq))
        normed = {b: (qn_ref if b < nq else kn_ref) for b in range(nq + nk)}
        bd = _pair_block_diag((LANES, LANES)).astype(BF16)
    p = _dot(h16, w_ref[...])
    for b in range(nblk):
        blk = p[:, b * LANES:(b + 1) * LANES]
        if b in normed:
            ms = _head_sum(blk * blk, bd) * (1.0 / HEAD_DIM)
            blk = blk * lax.rsqrt(ms + EPS) * normed[b][...]
        if b in rope_blocks:
            blk = _rope(blk, cos, sin)
        if b in scaled:
            blk = blk * QK_SCALE
        o_ref[:, b * LANES:(b + 1) * LANES] = blk


def _inproj(x, mod, layer, gain, w, cos_t, sin_t, qn=None, kn=None):
    even = qn is None
    n = w.shape[1]
    in_specs = _pair_specs(D_MODEL) + [
        pl.BlockSpec((None, None, 6, D_MODEL), lambda i: (layer, _mod_index(i), 0, 0)),
        pl.BlockSpec((1, D_MODEL), lambda i: (0, 0)),
        pl.BlockSpec((D_MODEL, n), lambda i: (0, 0)),
        pl.BlockSpec((TM, LANES), lambda i: (_rope_index(i), 0)),
        pl.BlockSpec((TM, LANES), lambda i: (_rope_index(i), 0)),
    ]
    args = [x[0], x[1], mod, gain.reshape(1, D_MODEL), w, cos_t, sin_t]
    if not even:
        in_specs += [pl.BlockSpec((1, LANES), lambda i: (0, 0))] * 2
        args += [jnp.tile(qn, 2).reshape(1, LANES), jnp.tile(kn, 2).reshape(1, LANES)]
    return pl.pallas_call(
        functools.partial(_inproj_kernel, even=even),
        grid=(T_ALL // TM,),
        in_specs=in_specs,
        out_specs=pl.BlockSpec((TM, n), lambda i: (i, 0)),
        out_shape=jax.ShapeDtypeStruct((T_ALL, n), F32),
        compiler_params=_cparams(("arbitrary",)),
        name="inproj_even" if even else "inproj_odd",
    )(*args)


RET_READOUT_ROWS = 512


def _ret_kernel(q_ref, k_ref, v_ref, g_ref, dm_ref, qd_ref, kd_ref, cm_ref, s0f_ref, s0b_ref,
                o_ref, sf_ref, sb_ref, ob_scr, *, nc, npb):
    lane = lax.broadcasted_iota(jnp.int32, (BLK, LANES), 1)
    lo = lane < HEAD_DIM
    bd = _pair_block_diag((LANES, LANES))
    bd16 = bd.astype(BF16)
    sf_ref[...] = s0f_ref[...]
    sb_ref[...] = s0b_ref[...]

    def chunk(c, j, d, s_ref, dst):
        r = pl.ds(pl.multiple_of(c * BLK, BLK), BLK)
        cols = slice(j * LANES, (j + 1) * LANES)
        qc = q_ref[r, cols]
        kc = k_ref[r, cols]
        vc = v_ref[r, cols]
        s = s_ref[j]
        a = _dot_nt(jnp.concatenate([jnp.where(lo, qc, 0.0), jnp.where(lo, 0.0, qc)], axis=0), kc)
        lhs = jnp.concatenate([a[:BLK] * dm_ref[j, d, 0], a[BLK:] * dm_ref[j, d, 1], qc * qd_ref[j, d]], axis=1)
        rhs = jnp.concatenate([jnp.where(lo, vc, 0.0), jnp.where(lo, 0.0, vc), s], axis=0)
        dst[r, cols] = _dot(lhs, rhs)
        kv = _dot((kc * kd_ref[j, d]).T, vc)
        s_ref[j] = s * cm_ref[j, d] + jnp.where(bd, kv, 0.0)

    def body(c, carry):
        for j in range(npb):
            chunk(c, j, 0, sf_ref, o_ref)
            chunk(nc - 1 - c, j, 1, sb_ref, ob_scr)
        return carry

    lax.fori_loop(0, nc, body, 0)

    rt = min(nc * BLK, RET_READOUT_ROWS)

    def readout(c, carry):
        r = pl.ds(pl.multiple_of(c * rt, rt), rt)
        for j in range(npb):
            cols = slice(j * LANES, (j + 1) * LANES)
            o = o_ref[r, cols] + ob_scr[r, cols]
            mu = _head_sum(o, bd16) * (1.0 / HEAD_DIM)
            dlt = o - mu
            var = _head_sum(dlt * dlt, bd16) * (1.0 / HEAD_DIM)
            o_ref[r, cols] = dlt * lax.rsqrt(var + EPS) * _silu(g_ref[r, cols])
        return carry

    lax.fori_loop(0, nc * BLK // rt, readout, 0)


def _retention(p, tabs, s0f, s0b, n_seq, seq_len, row0, npb):
    npair = RET_HEADS // 2
    ng = npair // npb
    rb0 = row0 // seq_len
    dm, qd, kd, cm = tabs
    w = npb * LANES

    def col(c0):
        return pl.BlockSpec((seq_len, w), lambda b, j: (rb0 + b, c0 + j))

    tab4 = pl.BlockSpec((npb, 2, LANES, LANES), lambda b, j: (j, 0, 0, 0))
    st = pl.BlockSpec((None, npb, LANES, LANES), lambda b, j: (b, j, 0, 0))
    return pl.pallas_call(
        functools.partial(_ret_kernel, nc=seq_len // BLK, npb=npb),
        grid=(n_seq, ng),
        in_specs=[col(0), col(ng), col(2 * ng), col(3 * ng),
                  pl.BlockSpec((npb, 2, 2, BLK, BLK), lambda b, j: (j, 0, 0, 0, 0)),
                  tab4, tab4, tab4, st, st],
        out_specs=[pl.BlockSpec((seq_len, w), lambda b, j: (b, j)), st, st],
        out_shape=[jax.ShapeDtypeStruct((n_seq * seq_len, RET_QW), F32),
                   jax.ShapeDtypeStruct((n_seq, npair, LANES, LANES), F32),
                   jax.ShapeDtypeStruct((n_seq, npair, LANES, LANES), F32)],
        scratch_shapes=[pltpu.VMEM((seq_len, w), F32)],
        compiler_params=_cparams(("arbitrary", "arbitrary")),
        name="retention",
    )(p, p, p, p, dm, qd, kd, cm, s0f, s0b)


def _retention_tables(decay_f, decay_b):
    npair = RET_HEADS // 2
    idx = jnp.arange(BLK, dtype=F32)
    diff = idx[:, None] - idx[None, :]

    def one(decay, backward):
        lg = jax.nn.log_sigmoid(decay.astype(F32))
        dmask = jnp.where(diff >= 0, jnp.exp(lg[:, None, None] * jnp.maximum(diff, 0.0)), 0.0)
        q_dec = jnp.exp(lg[:, None] * (idx + 1.0))
        k_dec = jnp.exp(lg[:, None] * (BLK - 1.0 - idx))
        c_dec = jnp.exp(lg * BLK)
        if backward:
            dmask = jnp.swapaxes(dmask, 1, 2)
            q_dec = q_dec[:, ::-1]
            k_dec = k_dec[:, ::-1]
        return dmask, q_dec, k_dec, c_dec

    def lanes(t):
        t = t.reshape(npair, 2, BLK)
        return jnp.repeat(jnp.swapaxes(t, 1, 2), HEAD_DIM, axis=2)

    parts = [one(decay_f, False), one(decay_b, True)]
    dm = jnp.stack([p[0].reshape(npair, 2, BLK, BLK) for p in parts], axis=1)
    qd = jnp.stack([lanes(p[1]) for p in parts], axis=1)
    kd = jnp.stack([lanes(p[2]) for p in parts], axis=1)
    bd = _pair_block_diag((LANES, LANES))
    cm = jnp.stack([jnp.where(bd[None], jnp.repeat(p[3].reshape(npair, 2), HEAD_DIM, axis=1)[:, :, None], 0.0)
                    for p in parts], axis=1)
    return dm, qd, kd, cm


def _pair_states(s):
    b = s.shape[0]
    s = s.astype(F32).reshape(b, RET_HEADS // 2, 2, HEAD_DIM, HEAD_DIM)
    z = jnp.zeros_like(s[:, :, 0])
    top = jnp.concatenate([s[:, :, 0], z], axis=-1)
    bot = jnp.concatenate([z, s[:, :, 1]], axis=-1)
    return jnp.concatenate([top, bot], axis=-2)


def _unpair_states(s):
    b = s.shape[0]
    h0 = s[:, :, :HEAD_DIM, :HEAD_DIM]
    h1 = s[:, :, HEAD_DIM:, HEAD_DIM:]
    return jnp.stack([h0, h1], axis=2).reshape(b, RET_HEADS, HEAD_DIM, HEAD_DIM)


LOG2E = 1.4426950408889634


def _kv_variants(k, v):
    lane = lax.broadcasted_iota(jnp.int32, k.shape, 1)
    lo = lane < HEAD_DIM
    ka = jnp.where(lo, k, 0.0)
    kb = jnp.where(lo, 0.0, k)
    va = jnp.where(lane == HEAD_DIM, 1.0, jnp.where(lo, v, 0.0))
    vb = jnp.where(lane == 0, 1.0, jnp.where(lo, 0.0, v))
    ks = (ka, pltpu.roll(ka, HEAD_DIM, 1), pltpu.roll(kb, HEAD_DIM, 1), kb)
    vs = (va, pltpu.roll(va, HEAD_DIM, 1), pltpu.roll(vb, HEAD_DIM, 1), vb)
    return [t.astype(BF16) for t in ks], [t.astype(BF16) for t in vs]


def _attn_core(q, k, v, sink_ref, o_ref, mask, rows=slice(None)):
    tq = q.shape[0]
    ks, vs = _kv_variants(k, v)
    lane = lax.broadcasted_iota(jnp.int32, (2 * tq, LANES), 1)
    first = lax.broadcasted_iota(jnp.int32, (2 * tq, 1), 0) < tq
    if mask is not None:
        mask = jnp.concatenate([mask, mask], axis=0)
    for half in (0, 1):
        q2 = jnp.concatenate([q[:, (2 * half) * LANES:(2 * half + 1) * LANES],
                              q[:, (2 * half + 1) * LANES:(2 * half + 2) * LANES]], axis=0)
        q2 = (q2 * LOG2E).astype(BF16)
        out = None
        for hh in (0, 1):
            var = 2 * half + hh
            s = _dot_nt(q2, ks[var])
            if mask is not None:
                s = jnp.where(mask, s, NEG_INF)
            m = jnp.max(s, axis=-1, keepdims=True)
            if sink_ref is not None:
                snk = LOG2E * jnp.where(first, sink_ref[2 * half, hh:hh + 1, 0:1],
                                        sink_ref[2 * half + 1, hh:hh + 1, 0:1])
                m = jnp.maximum(m, snk)
            acc = _dot(jnp.exp2((s - m).astype(BF16)), vs[var])
            ones_lane = HEAD_DIM if hh == 0 else 0
            den = acc[:, ones_lane:ones_lane + 1]
            if sink_ref is not None:
                den = den + jnp.exp2(snk - m)
            own = (lane < HEAD_DIM) if hh == 0 else (lane >= HEAD_DIM)
            o = jnp.where(own, acc, 0.0) * (1.0 / den)
            out = o if out is None else out + o
        o_ref[rows, (2 * half) * LANES:(2 * half + 1) * LANES] = out[:tq]
        o_ref[rows, (2 * half + 1) * LANES:(2 * half + 2) * LANES] = out[tq:]


def _full_attn_kernel(q_ref, k_ref, v_ref, *rest, has_sink):
    if has_sink:
        sink_ref, o_ref = rest
    else:
        sink_ref, (o_ref,) = None, rest
    for s in range(PROMPT_SEQS_PER_STEP):
        r = slice(s * SEQ, (s + 1) * SEQ)
        _attn_core(q_ref[r, :], k_ref[r, :], v_ref[r, :], sink_ref, o_ref, None, r)


PROMPT_SEQS_PER_STEP = 4


def _full_attention(p, q_cb0, k_cb0, v_cb0, n_groups, sink=None):
    rows = PROMPT_SEQS_PER_STEP * SEQ
    qw = 4 * LANES
    in_specs = [
        pl.BlockSpec((rows, qw), lambda b, g: (b, q_cb0 + g)),
        pl.BlockSpec((rows, LANES), lambda b, g: (b, k_cb0 + g)),
        pl.BlockSpec((rows, LANES), lambda b, g: (b, v_cb0 + g)),
    ]
    args = [p, p, p]
    if sink is not None:
        in_specs.append(pl.BlockSpec((None, 4, 2, LANES), lambda b, g: (g, 0, 0, 0)))
        args.append(sink)
    return pl.pallas_call(
        functools.partial(_full_attn_kernel, has_sink=sink is not None),
        grid=(T_PROMPT // rows, n_groups),
        in_specs=in_specs,
        out_specs=pl.BlockSpec((rows, qw), lambda b, g: (b, g)),
        out_shape=jax.ShapeDtypeStruct((T_PROMPT, n_groups * qw), F32),
        compiler_params=_cparams(("arbitrary", "arbitrary")),
        name="full_attention",
    )(*args)


def _latent_attn_kernel(q_ref, k_ref, v_ref, kx_ref, vx_ref, o_ref, kvar, vvar, *, tq, kb):
    lk = DEC_SEQ + PAST_LEN

    @pl.when(pl.program_id(2) == 0)
    def _():
        for src_k, src_v, r0, nrows in ((k_ref, v_ref, 0, DEC_SEQ), (kx_ref, vx_ref, DEC_SEQ, PAST_LEN)):
            for c in range(nrows // kb):
                src = pl.ds(c * kb, kb)
                dst = pl.ds(r0 + c * kb, kb)
                ks, vs = _kv_variants(src_k[src, :], src_v[src, :])
                for i in range(4):
                    kvar[i, dst, :] = ks[i]
                    vvar[i, dst, :] = vs[i]

    lane = lax.broadcasted_iota(jnp.int32, (2 * tq, LANES), 1)
    for half in (0, 1):
        q2 = jnp.concatenate([q_ref[:, (2 * half) * LANES:(2 * half + 1) * LANES],
                              q_ref[:, (2 * half + 1) * LANES:(2 * half + 2) * LANES]], axis=0)
        q2 = (q2 * LOG2E).astype(BF16)
        out = None
        for hh in (0, 1):
            var = 2 * half + hh
            m = jnp.full((2 * tq, 1), -jnp.inf, F32)
            acc = jnp.zeros((2 * tq, LANES), F32)
            for j in range(lk // kb):
                rows = pl.ds(j * kb, kb)
                s = _dot_nt(q2, kvar[var, rows, :])
                m_new = jnp.maximum(m, jnp.max(s, axis=-1, keepdims=True))
                e = jnp.exp2((s - m_new).astype(BF16))
                acc = jnp.exp2(m - m_new) * acc + _dot(e, vvar[var, rows, :])
                m = m_new
            own = (lane < HEAD_DIM) if hh == 0 else (lane >= HEAD_DIM)
            ones_lane = HEAD_DIM if hh == 0 else 0
            o = jnp.where(own, acc, 0.0) * (1.0 / acc[:, ones_lane:ones_lane + 1])
            out = o if out is None else out + o
        o_ref[:, (2 * half) * LANES:(2 * half + 1) * LANES] = out[:tq]
        o_ref[:, (2 * half + 1) * LANES:(2 * half + 2) * LANES] = out[tq:]


def _latent_attention(p, kx, vx):
    tq, kb = 256, 512
    nq = DEC_SEQ // tq
    ngr = ATT_KV_HEADS // 2
    kcb = ATT_QW // LANES
    rb_q = T_PROMPT // tq
    rb_k = T_PROMPT // DEC_SEQ
    lk = DEC_SEQ + PAST_LEN
    ctx = pl.BlockSpec((None, PAST_LEN, LANES), lambda b, g, i: (b, 0, g))
    return pl.pallas_call(
        functools.partial(_latent_attn_kernel, tq=tq, kb=kb),
        grid=(DEC_BATCH, ngr, nq),
        in_specs=[pl.BlockSpec((tq, 4 * LANES), lambda b, g, i: (rb_q + b * nq + i, g)),
                  pl.BlockSpec((DEC_SEQ, LANES), lambda b, g, i: (rb_k + b, kcb + g)),
                  pl.BlockSpec((DEC_SEQ, LANES), lambda b, g, i: (rb_k + b, kcb + ngr + g)),
                  ctx, ctx],
        out_specs=pl.BlockSpec((tq, 4 * LANES), lambda b, g, i: (b * nq + i, g)),
        out_shape=jax.ShapeDtypeStruct((T_SAMPLE, ATT_QW), F32),
        scratch_shapes=[pltpu.VMEM((4, lk, LANES), BF16), pltpu.VMEM((4, lk, LANES), BF16)],
        compiler_params=_cparams(("arbitrary", "arbitrary", "arbitrary")),
        name="latent_attention",
    )(p, p, p, kx, vx)


WIN_TQ = 2 * BLK
WIN_KBLKS = WIN_TQ // BLK + 2


def _window_attn_kernel(q_ref, *refs):
    k_refs = refs[:WIN_KBLKS]
    v_refs = refs[WIN_KBLKS:2 * WIN_KBLKS]
    kx_ref, vx_ref, sink_ref, o_ref = refs[2 * WIN_KBLKS:]
    n = pl.program_id(1)
    k = jnp.concatenate([r[...] for r in k_refs] + [kx_ref[...]], axis=0)
    v = jnp.concatenate([r[...] for r in v_refs] + [vx_ref[...]], axis=0)
    nloc = WIN_KBLKS * BLK
    shape = (WIN_TQ, nloc + PAST_LEN)
    i = lax.broadcasted_iota(jnp.int32, shape, 0)
    r = lax.broadcasted_iota(jnp.int32, shape, 1)
    kpos = n * WIN_TQ - BLK + r
    local = (jnp.abs(r - BLK - i) <= WINDOW) & (kpos >= 0) & (kpos < DEC_SEQ)
    mask = local | (r >= nloc)
    _attn_core(q_ref[...], k, v, sink_ref, o_ref, mask)


def _window_attention(p, kx, vx, sink):
    nb = DEC_SEQ // BLK
    nq = DEC_SEQ // WIN_TQ
    rb0 = T_PROMPT // BLK
    qcb = (4 * RET_QW) // (4 * LANES)
    kcb = (4 * RET_QW + WIN_QW) // LANES
    vcb = kcb + 1

    def kv(cb, off):
        return pl.BlockSpec((BLK, LANES),
                            lambda b, n: (rb0 + b * nb + jnp.clip(n * (WIN_TQ // BLK) + off, 0, nb - 1), cb))

    offs = range(-1, WIN_KBLKS - 1)
    ctx = pl.BlockSpec((None, PAST_LEN, LANES), lambda b, n: (b, 0, 0))
    qspec = pl.BlockSpec((WIN_TQ, 4 * LANES), lambda b, n: (T_PROMPT // WIN_TQ + b * nq + n, qcb))
    return pl.pallas_call(
        _window_attn_kernel,
        grid=(DEC_BATCH, nq),
        in_specs=[qspec] + [kv(kcb, o) for o in offs] + [kv(vcb, o) for o in offs] + [
            ctx, ctx, pl.BlockSpec((4, 2, LANES), lambda b, n: (0, 0, 0))],
        out_specs=pl.BlockSpec((WIN_TQ, 4 * LANES), lambda b, n: (b * nq + n, 0)),
        out_shape=jax.ShapeDtypeStruct((T_SAMPLE, WIN_QW), F32),
        compiler_params=_cparams(("arbitrary", "arbitrary")),
        name="window_attention",
    )(*([p] * (1 + 2 * WIN_KBLKS)), kx, vx, sink)


def _sink_table(sink, n_groups):
    s = sink.astype(F32).reshape(n_groups, 4, 2, 1)
    return jnp.broadcast_to(s, (n_groups, 4, 2, LANES))


def _cache_rows(cache):
    b, h, l, d = cache.shape
    return cache.astype(F32).transpose(0, 2, 1, 3).reshape(b, l, h * d)


def _outproj_kernel(*refs, n_in):
    a_refs = refs[:2 * n_in]
    w_refs = refs[2 * n_in:3 * n_in]
    xp_ref, xs_ref, mod_ref, g_ref, wr_ref, xn_ref, hfp_ref, hfs_ref, lg_ref = refs[3 * n_in:]
    i = pl.program_id(0)
    y = None
    for k, w_ref in enumerate(w_refs):
        a = _pick(a_refs[2 * k], a_refs[2 * k + 1])
        t = _dot(a.astype(BF16), w_ref[...])
        y = t if y is None else y + t
    xn = _pick(xp_ref, xs_ref) + mod_ref[2:3, :] * _rms(y, g_ref[1:2, :])
    hf = _rms(xn, g_ref[2:3, :]) * (1.0 + mod_ref[4:5, :]) + mod_ref[3:4, :]
    xn_ref[...] = xn
    lg_ref[...] = _dot_nt(wr_ref[...], hf.astype(BF16))

    @pl.when(i < NPT)
    def _():
        hfp_ref[...] = hf

    @pl.when(i >= NPT)
    def _():
        hfs_ref[...] = hf.reshape(TM, D_MODEL // LANES, LANES)


def _outproj(parts, w_parts, x, mod, layer, gains, w_router_t):
    n_in = len(parts)
    in_specs = []
    args = []
    for a_p, a_s in parts:
        in_specs += _pair_specs(a_p.shape[1])
        args += [a_p, a_s]
    in_specs += [pl.BlockSpec(w.shape, lambda i: (0, 0)) for w in w_parts]
    in_specs += _pair_specs(D_MODEL) + [
        pl.BlockSpec((None, None, 6, D_MODEL), lambda i: (layer, _mod_index(i), 0, 0)),
        pl.BlockSpec((4, D_MODEL), lambda i: (0, 0)),
        pl.BlockSpec((N_EXPERTS, D_MODEL), lambda i: (0, 0)),
    ]
    row = pl.BlockSpec((TM, D_MODEL), lambda i: (i, 0))
    hf_p, _ = _pair_specs(D_MODEL)
    hf_s = pl.BlockSpec((TM, D_MODEL // LANES, LANES), lambda i: (jnp.maximum(i - NPT, 0), 0, 0))
    return pl.pallas_call(
        functools.partial(_outproj_kernel, n_in=n_in),
        grid=(T_ALL // TM,),
        in_specs=in_specs,
        out_specs=[row, hf_p, hf_s, pl.BlockSpec((N_EXPERTS, TM), lambda i: (0, i))],
        out_shape=[jax.ShapeDtypeStruct((T_ALL, D_MODEL), F32),
                   jax.ShapeDtypeStruct((T_PROMPT, D_MODEL), F32),
                   jax.ShapeDtypeStruct((T_SAMPLE, D_MODEL // LANES, LANES), F32),
                   jax.ShapeDtypeStruct((N_EXPERTS, T_ALL), F32)],
        compiler_params=_cparams(("arbitrary",)),
        name="outproj",
    )(*args, *w_parts, x[0], x[1], mod, gains, w_router_t)


TILE_SUB = D_MODEL // LANES


def _gather_kernel(idx_ref, h_ref, o_ref, buf, *, eg, cap):
    b = pl.program_id(0)
    g = pl.program_id(1)
    for e in range(eg):
        base = (b * N_EXPERTS + g * eg + e) * cap

        def body(c, carry):
            buf[c] = h_ref[idx_ref[base + c]]
            return carry

        lax.fori_loop(0, cap, body, 0, unroll=8)
        o_ref[e] = buf[...].reshape(cap, D_MODEL).astype(BF16)


def _gather(h3, idx, n_seq, seq_len, cap, eg):
    return pl.pallas_call(
        functools.partial(_gather_kernel, eg=eg, cap=cap),
        grid_spec=pltpu.PrefetchScalarGridSpec(
            num_scalar_prefetch=1,
            grid=(n_seq, N_EXPERTS // eg),
            in_specs=[pl.BlockSpec((seq_len, TILE_SUB, LANES), lambda b, g, idx: (b, 0, 0))],
            out_specs=pl.BlockSpec((eg, cap, D_MODEL), lambda b, g, idx: (g, b, 0)),
            scratch_shapes=[pltpu.VMEM((cap, TILE_SUB, LANES), F32)],
        ),
        out_shape=jax.ShapeDtypeStruct((N_EXPERTS, n_seq * cap, D_MODEL), BF16),
        compiler_params=_cparams(("arbitrary", "arbitrary")),
        name="moe_gather",
    )(idx, h3)


def _moe_kernel(xp_ref, xs_ref, wg_ref, wu_ref, wd_ref, gwp_ref, gws_ref, yp_ref, ys_ref, hid_scr, wd_scr):
    f = pl.program_id(1)
    wg = wg_ref[...].astype(BF16)
    wu = wu_ref[...].astype(BF16)
    cols = pl.ds(pl.multiple_of(f * TF, TF), TF)
    wd_scr[cols, :] = wd_ref[...].astype(BF16)
    for h, x_ref in enumerate((xp_ref, xs_ref)):
        x = x_ref[...]
        hid_scr[h, :, cols] = (_silu(_dot(x, wg)) * _dot(x, wu)).astype(BF16)

    @pl.when(f == pl.num_programs(1) - 1)
    def _():
        for h, (gw_ref, y_ref) in enumerate(((gwp_ref, yp_ref), (gws_ref, ys_ref))):
            for r in range(0, hid_scr.shape[1], MOE_ROW_TILE):
                rows = slice(r, r + MOE_ROW_TILE)
                y_ref[rows, :] = _dot(hid_scr[h, rows, :], wd_scr[...]) * gw_ref[rows, :]


MOE_ROW_TILE = 512


def _moe_ffn(xg_p, xg_s, w_gate, w_up, w_down, layer, gw_p, gw_s):
    assert ROWS_P == ROWS_S
    nf = EXPERT_FF // TF
    xspec_p = pl.BlockSpec((None, ROWS_P, D_MODEL), lambda e, f: (e, 0, 0))
    xspec_s = pl.BlockSpec((None, ROWS_S, D_MODEL), lambda e, f: (e, 0, 0))
    return pl.pallas_call(
        _moe_kernel,
        grid=(N_EXPERTS, nf),
        in_specs=[xspec_p, xspec_s,
                  pl.BlockSpec((None, None, D_MODEL, TF), lambda e, f: (layer, e, 0, f)),
                  pl.BlockSpec((None, None, D_MODEL, TF), lambda e, f: (layer, e, 0, f)),
                  pl.BlockSpec((None, None, TF, D_MODEL), lambda e, f: (layer, e, f, 0)),
                  pl.BlockSpec((None, ROWS_P, 1), lambda e, f: (e, 0, 0)),
                  pl.BlockSpec((None, ROWS_S, 1), lambda e, f: (e, 0, 0))],
        out_specs=[xspec_p, xspec_s],
        out_shape=[jax.ShapeDtypeStruct((N_EXPERTS, ROWS_P, D_MODEL), F32),
                   jax.ShapeDtypeStruct((N_EXPERTS, ROWS_S, D_MODEL), F32)],
        scratch_shapes=[pltpu.VMEM((2, ROWS_P, EXPERT_FF), BF16),
                        pltpu.VMEM((EXPERT_FF, D_MODEL), BF16)],
        compiler_params=_cparams(("arbitrary", "arbitrary")),
        name="moe_ffn",
    )(xg_p, xg_s, w_gate, w_up, w_down, gw_p, gw_s)


COMBINE_GROUP = 8


def _combine_kernel(idx_ref, y_ref, o_ref, y3, *, eg, cap):
    b = pl.program_id(0)
    g = pl.program_id(1)

    @pl.when(g == 0)
    def _():
        o_ref[...] = jnp.zeros_like(o_ref)

    for e in range(eg):
        base = (b * N_EXPERTS + g * eg + e) * cap
        y3[...] = y_ref[e].reshape(cap, TILE_SUB, LANES)

        def body(c, carry):
            c0 = c * COMBINE_GROUP
            toks = [idx_ref[base + c0 + k] for k in range(COMBINE_GROUP)]
            rows = [o_ref[t] + y3[c0 + k] for k, t in enumerate(toks)]
            for t, row in zip(toks, rows):
                o_ref[t] = row
            return carry

        lax.fori_loop(0, cap // COMBINE_GROUP, body, 0)


def _combine(y, idx, n_seq, seq_len, cap, eg):
    return pl.pallas_call(
        functools.partial(_combine_kernel, eg=eg, cap=cap),
        grid_spec=pltpu.PrefetchScalarGridSpec(
            num_scalar_prefetch=1,
            grid=(n_seq, N_EXPERTS // eg),
            in_specs=[pl.BlockSpec((eg, cap, D_MODEL), lambda b, g, idx: (g, b, 0))],
            out_specs=pl.BlockSpec((seq_len, TILE_SUB, LANES), lambda b, g, idx: (b, 0, 0)),
            scratch_shapes=[pltpu.VMEM((cap, TILE_SUB, LANES), F32)],
        ),
        out_shape=jax.ShapeDtypeStruct((n_seq * seq_len, TILE_SUB, LANES), F32),
        compiler_params=_cparams(("arbitrary", "arbitrary")),
        name="moe_combine",
    )(idx, y)


def _onehot_t(idx_row, n_tok):
    tok = lax.broadcasted_iota(jnp.int32, (n_tok, idx_row.shape[-1]), 0)
    return jnp.where(tok == idx_row, 1.0, 0.0)


def _gather_mm_kernel(idx_ref, h_ref, o_ref, *, cap, seq_len):
    for s in range(PROMPT_SEQS_PER_STEP):
        sel = _onehot_t(idx_ref[s], seq_len).T.astype(BF16)
        h = h_ref[s * seq_len:(s + 1) * seq_len, :].astype(BF16)
        rows = _dot(sel, h).astype(BF16)
        for e in range(N_EXPERTS):
            o_ref[e, s * cap:(s + 1) * cap, :] = rows[e * cap:(e + 1) * cap]


def _gather_mm(h, idx, n_seq, seq_len, cap):
    ns = PROMPT_SEQS_PER_STEP
    return pl.pallas_call(
        functools.partial(_gather_mm_kernel, cap=cap, seq_len=seq_len),
        grid=(n_seq // ns,),
        in_specs=[pl.BlockSpec((ns, 1, N_EXPERTS * cap), lambda b: (b, 0, 0)),
                  pl.BlockSpec((ns * seq_len, D_MODEL), lambda b: (b, 0))],
        out_specs=pl.BlockSpec((N_EXPERTS, ns * cap, D_MODEL), lambda b: (0, b, 0)),
        out_shape=jax.ShapeDtypeStruct((N_EXPERTS, n_seq * cap, D_MODEL), BF16),
        compiler_params=_cparams(("arbitrary",)),
        name="moe_gather_mm",
    )(idx, h)


def _combine_mm_kernel(idx_ref, y_ref, o_ref, *, cap, seq_len):
    for s in range(PROMPT_SEQS_PER_STEP):
        sel = _onehot_t(idx_ref[s], seq_len).astype(BF16)
        y = jnp.concatenate([y_ref[e, s * cap:(s + 1) * cap, :] for e in range(N_EXPERTS)], axis=0)
        hi = y.astype(BF16)
        lo = (y - hi.astype(F32)).astype(BF16)
        o_ref[s * seq_len:(s + 1) * seq_len, :] = _dot(sel, hi) + _dot(sel, lo)


def _combine_mm(y, idx, n_seq, seq_len, cap):
    ns = PROMPT_SEQS_PER_STEP
    return pl.pallas_call(
        functools.partial(_combine_mm_kernel, cap=cap, seq_len=seq_len),
        grid=(n_seq // ns,),
        in_specs=[pl.BlockSpec((ns, 1, N_EXPERTS * cap), lambda b: (b, 0, 0)),
                  pl.BlockSpec((N_EXPERTS, ns * cap, D_MODEL), lambda b: (0, b, 0))],
        out_specs=pl.BlockSpec((ns * seq_len, D_MODEL), lambda b: (b, 0)),
        out_shape=jax.ShapeDtypeStruct((n_seq * seq_len, D_MODEL), F32),
        compiler_params=_cparams(("arbitrary",)),
        name="moe_combine_mm",
    )(idx, y)


def _postnorm_kernel(x_ref, yp_ref, ys_ref, mod_ref, g_ref, op_ref, os_ref):
    i = pl.program_id(0)
    y = jnp.where(i < NPT, yp_ref[...], ys_ref[...].reshape(TM, D_MODEL))
    o = x_ref[...] + mod_ref[5:6, :] * _rms(y, g_ref[3:4, :])

    @pl.when(i < NPT)
    def _():
        op_ref[...] = o

    @pl.when(i >= NPT)
    def _():
        os_ref[...] = o


def _postnorm(x, y_p, y_s, mod, layer, gains):
    return pl.pallas_call(
        _postnorm_kernel,
        grid=(T_ALL // TM,),
        in_specs=[pl.BlockSpec((TM, D_MODEL), lambda i: (i, 0)), _pair_specs(D_MODEL)[0],
                  pl.BlockSpec((TM, TILE_SUB, LANES), lambda i: (jnp.maximum(i - NPT, 0), 0, 0)),
                  pl.BlockSpec((None, None, 6, D_MODEL), lambda i: (layer, _mod_index(i), 0, 0)),
            pl.BlockSpec((4, D_MODEL), lambda i: (0, 0))],
        out_specs=_pair_specs(D_MODEL),
        out_shape=[jax.ShapeDtypeStruct((T_PROMPT, D_MODEL), F32),
                   jax.ShapeDtypeStruct((T_SAMPLE, D_MODEL), F32)],
        compiler_params=_cparams(("arbitrary",)),
        name="postnorm",
    )(x, y_p, y_s, mod, gains)


def _excl_prefix(mask):
    r, n = mask.shape
    i0 = lax.broadcasted_iota(jnp.int32, (LANES, LANES), 0)
    i1 = lax.broadcasted_iota(jnp.int32, (LANES, LANES), 1)
    upper = jnp.where(i0 < i1, 1.0, 0.0).astype(BF16)
    ones = jnp.where(mask, 1.0, 0.0)
    carry = jnp.zeros((r, 1), F32)
    out = []
    for c in range(n // LANES):
        ch = ones[:, c * LANES:(c + 1) * LANES]
        out.append(_dot(ch.astype(BF16), upper) + carry)
        carry = carry + jnp.sum(ch, axis=1, keepdims=True)
    return jnp.concatenate(out, axis=1)


def _route_kernel(lg_ref, idx_ref, gw_ref, *, nb, n, cap):
    affs = []
    for s in range(nb):
        l = lg_ref[:, s * n:(s + 1) * n]
        e = jnp.exp(l - jnp.max(l, axis=0, keepdims=True))
        affs.append(e / jnp.sum(e, axis=0, keepdims=True))
    aff = affs[0] if nb == 1 else jnp.concatenate(affs, axis=0)
    rows = nb * N_EXPERTS
    bits = jnp.zeros((rows, 1), jnp.int32)
    for bit in range(30, -1, -1):
        cand = bits | (1 << bit)
        cnt = jnp.sum(jnp.where(aff >= pltpu.bitcast(cand, F32), 1.0, 0.0), axis=1, keepdims=True)
        bits = jnp.where(cnt >= cap, cand, bits)
    thr = pltpu.bitcast(bits, F32)
    gt = aff > thr
    eq = aff == thr
    need = cap - jnp.sum(jnp.where(gt, 1.0, 0.0), axis=1, keepdims=True)
    sel = gt | (eq & (_excl_prefix(eq) < need))
    pos = jnp.where(sel, _excl_prefix(sel), -1.0)
    tok = lax.broadcasted_iota(jnp.int32, (rows, n), 1).astype(F32)
    capw = max(cap, LANES)
    slot_lane = lax.broadcasted_iota(jnp.int32, (rows, capw), 1)

    def slot(s, carry):
        idx_acc, gw_acc = carry
        hit = pos == lax.convert_element_type(s, F32)
        ic = jnp.sum(jnp.where(hit, tok, 0.0), axis=1, keepdims=True)
        gc = jnp.sum(jnp.where(hit, aff, 0.0), axis=1, keepdims=True)
        here = slot_lane == s
        return jnp.where(here, ic, idx_acc), jnp.where(here, gc, gw_acc)

    zero = jnp.zeros((rows, capw), F32)
    idx_acc, gw_acc = lax.fori_loop(0, cap, slot, (zero, zero), unroll=8)
    for s in range(nb):
        r = slice(s * N_EXPERTS, (s + 1) * N_EXPERTS)
        idx_ref[s] = idx_acc[r, :cap].astype(jnp.int32)
        gw_ref[s] = gw_acc[r, :cap]


def _route(logits_t, n_seq, seq_len, cap, col0, nb):
    cb0 = col0 // (nb * seq_len)
    out = pl.BlockSpec((nb, N_EXPERTS, cap), lambda b: (b, 0, 0))
    return pl.pallas_call(
        functools.partial(_route_kernel, nb=nb, n=seq_len, cap=cap),
        grid=(n_seq // nb,),
        in_specs=[pl.BlockSpec((N_EXPERTS, nb * seq_len), lambda b: (0, cb0 + b))],
        out_specs=[out, out],
        out_shape=[jax.ShapeDtypeStruct((n_seq, N_EXPERTS, cap), jnp.int32),
                   jax.ShapeDtypeStruct((n_seq, N_EXPERTS, cap), F32)],
        compiler_params=_cparams(("arbitrary",)),
        name="route",
    )(logits_t)


def _expert_ffn(hf_p, hf_s, logits_t, w_gate, w_up, w_down, layer):
    idx_p, gw_p = _route(logits_t, BATCH, SEQ, CAP_P, 0, 8)
    idx_s, gw_s = _route(logits_t, DEC_BATCH, DEC_SEQ, CAP_S, T_PROMPT, 1)
    idx_p = idx_p.reshape(BATCH, 1, N_EXPERTS * CAP_P)
    idx_s = idx_s.reshape(DEC_BATCH * N_EXPERTS * CAP_S)
    xg_p = _gather_mm(hf_p, idx_p, BATCH, SEQ, CAP_P)
    xg_s = _gather(hf_s, idx_s, DEC_BATCH, DEC_SEQ, CAP_S, 4)
    gw_p = gw_p.transpose(1, 0, 2).reshape(N_EXPERTS, ROWS_P, 1)
    gw_s = gw_s.transpose(1, 0, 2).reshape(N_EXPERTS, ROWS_S, 1)
    y_p, y_s = _moe_ffn(xg_p, xg_s, w_gate, w_up, w_down, layer, gw_p, gw_s)
    o_p = _combine_mm(y_p, idx_p, BATCH, SEQ, CAP_P)
    o_s = _combine(y_s, idx_s, DEC_BATCH, DEC_SEQ, CAP_S, 2)
    return o_p, o_s


def _rope_tables():
    t = np.arange(DEC_SEQ)
    row = (t // GRID_W).astype(np.float32)
    col = (t % GRID_W).astype(np.float32)
    inv = (ROPE_THETA ** (-np.arange(0, AXIS_DIM, 2, dtype=np.float32) / AXIS_DIM)).astype(np.float32)
    ar = row[:, None] * inv[None]
    ac = col[:, None] * inv[None]
    ang = np.concatenate([ar, ar, ac, ac], axis=-1)
    ang = np.concatenate([ang, ang], axis=-1)
    cos = np.concatenate([np.ones((TM, LANES), np.float32), np.cos(ang)], axis=0)
    sin = np.concatenate([np.zeros((TM, LANES), np.float32), np.sin(ang)], axis=0)
    return jnp.asarray(cos, F32), jnp.asarray(sin, F32)


def _split_kv_heads(rows, n_heads):
    return rows.reshape(BATCH, SEQ, n_heads, HEAD_DIM).transpose(0, 2, 1, 3)[:, None]


def kernel(x_prompt, x_sample, state_ret_fwd, state_ret_bwd, cache_win_k, cache_win_v, cache_attn_k, cache_attn_v, c, c_ctx, w_ada, b_ada, norm_gains, w_in_even, w_out_even, ret_decay_fwd, ret_decay_bwd, win_sink, w_in_odd, w_out_odd, q_norm, k_norm, w_router, w_gate, w_up, w_down):
    x = (x_prompt.reshape(T_PROMPT, D_MODEL), x_sample.reshape(T_SAMPLE, D_MODEL))
    cond =jnp.concatenate([c_ctx[None, :], c, jnp.zeros((8 - 1 - DEC_BATCH, D_MODEL), F32)], axis=0)
    mod = _modulation(cond, w_ada, b_ada).reshape(DEPTH, 8, 6, D_MODEL)
    cos_t, sin_t = _rope_tables()
    outs = {}
    for layer in range(DEPTH):
        j = layer // 2
        gains = norm_gains[layer]
        if layer % 2 == 0:
            p = _inproj(x, mod, layer, gains[0], w_in_even[j].astype(BF16), cos_t, sin_t)
            tabs = _retention_tables(ret_decay_fwd[j], ret_decay_bwd[j])
            zero = jnp.zeros((BATCH, RET_HEADS // 2, LANES, LANES), F32)
            ret_p, sf, sb = _retention(p, tabs, zero, zero, BATCH, SEQ, 0, 4)
            ret_s, _, _ = _retention(p, tabs, _pair_states(state_ret_fwd[:, j]),
                                     _pair_states(state_ret_bwd[:, j]), DEC_BATCH, DEC_SEQ, T_PROMPT, 2)
            sink = _sink_table(win_sink[j], 1)
            qcb = (4 * RET_QW) // (4 * LANES)
            kcb = (4 * RET_QW + WIN_QW) // LANES
            win_p = _full_attention(p, qcb, kcb, kcb + 1, 1, sink=sink)
            win_s = _window_attention(p, _cache_rows(cache_win_k[:, j]), _cache_rows(cache_win_v[:, j]),
                                      sink[0])
            w_out = w_out_even[j].astype(BF16)
            parts = [(ret_p, ret_s), (win_p, win_s)]
            w_parts = [w_out[:RET_QW], w_out[RET_QW:]]
            outs["ret_f"] = _unpair_states(sf)[:, None]
            outs["ret_b"] = _unpair_states(sb)[:, None]
            c0 = 4 * RET_QW + WIN_QW
            outs["win_k"] = _split_kv_heads(p[:T_PROMPT, c0:c0 + WIN_KW], WIN_KV_HEADS)
            outs["win_v"] = _split_kv_heads(p[:T_PROMPT, c0 + WIN_KW:c0 + 2 * WIN_KW], WIN_KV_HEADS)
        else:
            p = _inproj(x, mod, layer, gains[0], w_in_odd[j].astype(BF16), cos_t, sin_t,
                        qn=q_norm[j], kn=k_norm[j])
            ngr = ATT_KV_HEADS // 2
            kcb = ATT_QW // LANES
            att_p = _full_attention(p, 0, kcb, kcb + ngr, ngr)
            att_s = _latent_attention(p, _cache_rows(cache_attn_k[:, j]), _cache_rows(cache_attn_v[:, j]))
            parts = [(att_p, att_s)]
            w_parts = [w_out_odd[j].astype(BF16)]
            outs["att_k"] = _split_kv_heads(p[:T_PROMPT, ATT_QW:ATT_QW + ATT_KW], ATT_KV_HEADS)
            outs["att_v"] = _split_kv_heads(p[:T_PROMPT, ATT_QW + ATT_KW:], ATT_KV_HEADS)
        xn, hf_p, hf_s, logits_t = _outproj(parts, w_parts, x, mod, layer, gains,
                                            w_router[layer].T.astype(BF16))
        o_p, o_s = _expert_ffn(hf_p, hf_s, logits_t, w_gate, w_up, w_down, layer)
        x = _postnorm(xn, o_p, o_s, mod, layer, gains)
    y_prompt = x[0].reshape(BATCH, SEQ, D_MODEL)
    y_sample = x[1].reshape(DEC_BATCH, DEC_SEQ, D_MODEL)
    return (y_prompt, y_sample, outs["ret_f"], outs["ret_b"], outs["win_k"], outs["win_v"],
            outs["att_k"], outs["att_v"])
```

```python
import functools

import jax
import jax.numpy as jnp
import numpy as np
from jax import lax
from jax.experimental import pallas as pl
from jax.experimental.pallas import tpu as pltpu

D_MODEL = 1024
BATCH = 32
SEQ = 256
DEPTH = 2
DEC_BATCH = 2
DEC_SEQ = 4096
PAST_LEN = 512
GRID_W = 64
HEAD_DIM = 64
AXIS_DIM = HEAD_DIM // 2
ROPE_THETA = 10000.0
BLK = 128
WINDOW = 128
RET_HEADS = D_MODEL // 128
WIN_HEADS = D_MODEL // 128
WIN_KV_HEADS = WIN_HEADS // 4
ATT_HEADS = D_MODEL // HEAD_DIM
ATT_KV_HEADS = ATT_HEADS // 4
RET_QW = RET_HEADS * HEAD_DIM
WIN_QW = WIN_HEADS * HEAD_DIM
WIN_KW = WIN_KV_HEADS * HEAD_DIM
EVEN_IN = 4 * RET_QW + WIN_QW + 2 * WIN_KW
ATT_QW = ATT_HEADS * HEAD_DIM
ATT_KW = ATT_KV_HEADS * HEAD_DIM
ODD_IN = ATT_QW + 2 * ATT_KW
N_EXPERTS = 16
CAPACITY_FACTOR = 2
EXPERT_FF = ((8 * D_MODEL // 3 + 127) // 128) * 128
EPS = 1e-6
NEG_INF = -1e30
F32 = jnp.float32
BF16 = jnp.bfloat16

LANES = 128
T_PROMPT = BATCH * SEQ
T_SAMPLE = DEC_BATCH * DEC_SEQ
T_ALL = T_PROMPT + T_SAMPLE
TM = 512
CAP_P = CAPACITY_FACTOR * SEQ // N_EXPERTS
CAP_S = CAPACITY_FACTOR * DEC_SEQ // N_EXPERTS
ROWS_P = BATCH * CAP_P
ROWS_S = DEC_BATCH * CAP_S
TF = 256
VMEM_LIMIT = 56 * 1024 * 1024
QK_SCALE = HEAD_DIM ** -0.5


def _cparams(sem):
    return pltpu.CompilerParams(dimension_semantics=sem, vmem_limit_bytes=VMEM_LIMIT)


def _silu(x):
    return x * (1.0 / (1.0 + jnp.exp(-x)))


def _dot(a, b):
    return jnp.dot(a, b, preferred_element_type=F32)


def _dot_nt(a, b):
    return lax.dot_general(a, b, (((1,), (1,)), ((), ())), preferred_element_type=F32)


def _rms(x, g):
    return x * lax.rsqrt(jnp.mean(x * x, axis=-1, keepdims=True) + EPS) * g


def _pair_block_diag(shape):
    r = lax.broadcasted_iota(jnp.int32, shape, 0)
    c = lax.broadcasted_iota(jnp.int32, shape, 1)
    return (r // HEAD_DIM) == (c // HEAD_DIM)


def _head_sum(x, bd):
    hi = x.astype(BF16)
    lo = (x - hi.astype(F32)).astype(BF16)
    return _dot(hi, bd) + _dot(lo, bd)


def _rope(xb, cos, sin):
    lane = lax.broadcasted_iota(jnp.int32, xb.shape, 1)
    half = AXIS_DIM // 2
    rot = jnp.where((lane % AXIS_DIM) < half,
                    -pltpu.roll(xb, LANES - half, 1), pltpu.roll(xb, half, 1))
    return xb * cos + rot * sin


def _mod_kernel(c_ref, w_ref, b_ref, o_ref):
    o_ref[...] = _dot(_silu(c_ref[...]), w_ref[...]) + b_ref[...]


def _modulation(cond, w_ada, b_ada):
    tn = 1536
    n = 6 * D_MODEL
    return pl.pallas_call(
        _mod_kernel,
        grid=(DEPTH, n // tn),
        in_specs=[
            pl.BlockSpec((8, D_MODEL), lambda l, j: (0, 0)),
            pl.BlockSpec((None, D_MODEL, tn), lambda l, j: (l, 0, j)),
            pl.BlockSpec((None, 1, tn), lambda l, j: (l, 0, j)),
        ],
        out_specs=pl.BlockSpec((None, 8, tn), lambda l, j: (l, 0, j)),
        out_shape=jax.ShapeDtypeStruct((DEPTH, 8, n), F32),
        compiler_params=_cparams(("arbitrary", "arbitrary")),
        name="modulation",
    )(cond, w_ada, b_ada.reshape(DEPTH, 1, n))


def _mod_index(i):
    npt = T_PROMPT // TM
    return jnp.where(i < npt, 0, 1 + (i - npt) // (DEC_SEQ // TM))


def _rope_index(i):
    npt = T_PROMPT // TM
    return jnp.where(i < npt, 0, 1 + (i - npt) % (DEC_SEQ // TM))


NPT = T_PROMPT // TM


def _pair_specs(width):
    return [pl.BlockSpec((TM, width), lambda i: (jnp.minimum(i, NPT - 1), 0)),
            pl.BlockSpec((TM, width), lambda i: (jnp.maximum(i - NPT, 0), 0))]


def _pick(p_ref, s_ref):
    return jnp.where(pl.program_id(0) < NPT, p_ref[...], s_ref[...])


def _inproj_kernel(xp_ref, xs_ref, mod_ref, g_ref, w_ref, cos_ref, sin_ref, *rest, even):
    if even:
        (o_ref,) = rest
    else:
        qn_ref, kn_ref, o_ref = rest
    h = _rms(_pick(xp_ref, xs_ref), g_ref[...]) * (1.0 + mod_ref[1:2, :]) + mod_ref[0:1, :]
    h16 = h.astype(BF16)
    cos = cos_ref[...]
    sin = sin_ref[...]
    nblk = w_ref.shape[1] // LANES
    if even:
        nq = RET_QW // LANES
        rope_blocks = set(range(0, 2 * nq)) | set(range(4 * nq, 5 * nq + 1))
        scaled = set(range(nq, 2 * nq)) | set(range(4 * nq, 5 * nq))
        normed = {}
    else:
        nq = ATT_QW // LANES
        nk = ATT_KW // LANES
        rope_blocks = set(range(0, nq + nk))
        scaled = set(range(0, nq))
        normed = {b: (qn_ref if b < nq else kn_ref) for b in range(nq + nk)}
        bd = _pair_block_diag((LANES, LANES)).astype(BF16)
    p = _dot(h16, w_ref[...])
    for b in range(nblk):
        blk = p[:, b * LANES:(b + 1) * LANES]
        if b in normed:
            ms = _head_sum(blk * blk, bd) * (1.0 / HEAD_DIM)
            blk = blk * lax.rsqrt(ms + EPS) * normed[b][...]
        if b in rope_blocks:
            blk = _rope(blk, cos, sin)
        if b in scaled:
            blk = blk * QK_SCALE
        o_ref[:, b * LANES:(b + 1) * LANES] = blk


def _inproj(x, mod, layer, gain, w, cos_t, sin_t, qn=None, kn=None):
    even = qn is None
    n = w.shape[1]
    in_specs = _pair_specs(D_MODEL) + [
        pl.BlockSpec((None, None, 6, D_MODEL), lambda i: (layer, _mod_index(i), 0, 0)),
        pl.BlockSpec((1, D_MODEL), lambda i: (0, 0)),
        pl.BlockSpec((D_MODEL, n), lambda i: (0, 0)),
        pl.BlockSpec((TM, LANES), lambda i: (_rope_index(i), 0)),
        pl.BlockSpec((TM, LANES), lambda i: (_rope_index(i), 0)),
    ]
    args = [x[0], x[1], mod, gain.reshape(1, D_MODEL), w, cos_t, sin_t]
    if not even:
        in_specs += [pl.BlockSpec((1, LANES), lambda i: (0, 0))] * 2
        args += [jnp.tile(qn, 2).reshape(1, LANES), jnp.tile(kn, 2).reshape(1, LANES)]
    return pl.pallas_call(
        functools.partial(_inproj_kernel, even=even),
        grid=(T_ALL // TM,),
        in_specs=in_specs,
        out_specs=pl.BlockSpec((TM, n), lambda i: (i, 0)),
        out_shape=jax.ShapeDtypeStruct((T_ALL, n), F32),
        compiler_params=_cparams(("arbitrary",)),
        name="inproj_even" if even else "inproj_odd",
    )(*args)


RET_READOUT_ROWS = 512
RET_CHUNK = 2 * BLK


def _ret_kernel(q_ref, k_ref, v_ref, g_ref, dm_ref, qd_ref, kd_ref, cm_ref, s0f_ref, s0b_ref,
                o_ref, sf_ref, sb_ref, ob_scr, *, nc, npb):
    rc = RET_CHUNK
    lane = lax.broadcasted_iota(jnp.int32, (rc, LANES), 1)
    lo = lane < HEAD_DIM
    bd = _pair_block_diag((LANES, LANES))
    bd16 = bd.astype(BF16)
    sf_ref[...] = s0f_ref[...]
    sb_ref[...] = s0b_ref[...]

    def chunk(c, j, d, s_ref, dst):
        r = pl.ds(pl.multiple_of(c * rc, rc), rc)
        cols = slice(j * LANES, (j + 1) * LANES)
        qc = q_ref[r, cols]
        kc = k_ref[r, cols]
        vc = v_ref[r, cols]
        s = s_ref[j]
        a = _dot_nt(jnp.concatenate([jnp.where(lo, qc, 0.0), jnp.where(lo, 0.0, qc)], axis=0), kc)
        lhs = jnp.concatenate([a[:rc] * dm_ref[j, d, 0], a[rc:] * dm_ref[j, d, 1], qc * qd_ref[j, d]], axis=1)
        rhs = jnp.concatenate([jnp.where(lo, vc, 0.0), jnp.where(lo, 0.0, vc), s], axis=0)
        dst[r, cols] = _dot(lhs, rhs)
        kv = _dot((kc * kd_ref[j, d]).T, vc)
        s_ref[j] = s * cm_ref[j, d] + jnp.where(bd, kv, 0.0)

    def body(c, carry):
        for j in range(npb):
            chunk(c, j, 0, sf_ref, o_ref)
            chunk(nc - 1 - c, j, 1, sb_ref, ob_scr)
        return carry

    lax.fori_loop(0, nc, body, 0)

    rt = min(nc * rc, RET_READOUT_ROWS)

    def readout(c, carry):
        r = pl.ds(pl.multiple_of(c * rt, rt), rt)
        for j in range(npb):
            cols = slice(j * LANES, (j + 1) * LANES)
            o = o_ref[r, cols] + ob_scr[r, cols]
            mu = _head_sum(o, bd16) * (1.0 / HEAD_DIM)
            dlt = o - mu
            var = _head_sum(dlt * dlt, bd16) * (1.0 / HEAD_DIM)
            o_ref[r, cols] = dlt * lax.rsqrt(var + EPS) * _silu(g_ref[r, cols])
        return carry

    lax.fori_loop(0, nc * rc // rt, readout, 0)


def _retention(p, tabs, s0f, s0b, n_seq, seq_len, row0, npb):
    npair = RET_HEADS // 2
    ng = npair // npb
    rb0 = row0 // seq_len
    dm, qd, kd, cm = tabs
    w = npb * LANES

    def col(c0):
        return pl.BlockSpec((seq_len, w), lambda b, j: (rb0 + b, c0 + j))

    rc = RET_CHUNK
    tab_rows = pl.BlockSpec((npb, 2, rc, LANES), lambda b, j: (j, 0, 0, 0))
    tab_state = pl.BlockSpec((npb, 2, LANES, LANES), lambda b, j: (j, 0, 0, 0))
    st = pl.BlockSpec((None, npb, LANES, LANES), lambda b, j: (b, j, 0, 0))
    return pl.pallas_call(
        functools.partial(_ret_kernel, nc=seq_len // rc, npb=npb),
        grid=(n_seq, ng),
        in_specs=[col(0), col(ng), col(2 * ng), col(3 * ng),
                  pl.BlockSpec((npb, 2, 2, rc, rc), lambda b, j: (j, 0, 0, 0, 0)),
                  tab_rows, tab_rows, tab_state, st, st],
        out_specs=[pl.BlockSpec((seq_len, w), lambda b, j: (b, j)), st, st],
        out_shape=[jax.ShapeDtypeStruct((n_seq * seq_len, RET_QW), F32),
                   jax.ShapeDtypeStruct((n_seq, npair, LANES, LANES), F32),
                   jax.ShapeDtypeStruct((n_seq, npair, LANES, LANES), F32)],
        scratch_shapes=[pltpu.VMEM((seq_len, w), F32)],
        compiler_params=_cparams(("arbitrary", "arbitrary")),
        name="retention",
    )(p, p, p, p, dm, qd, kd, cm, s0f, s0b)


def _retention_tables(decay_f, decay_b):
    npair = RET_HEADS // 2
    rc = RET_CHUNK
    idx = jnp.arange(rc, dtype=F32)
    diff = idx[:, None] - idx[None, :]

    def one(decay, backward):
        lg = jax.nn.log_sigmoid(decay.astype(F32))
        dmask = jnp.where(diff >= 0, jnp.exp(lg[:, None, None] * jnp.maximum(diff, 0.0)), 0.0)
        q_dec = jnp.exp(lg[:, None] * (idx + 1.0))
        k_dec = jnp.exp(lg[:, None] * (rc - 1.0 - idx))
        c_dec = jnp.exp(lg * rc)
        if backward:
            dmask = jnp.swapaxes(dmask, 1, 2)
            q_dec = q_dec[:, ::-1]
            k_dec = k_dec[:, ::-1]
        return dmask, q_dec, k_dec, c_dec

    def lanes(t):
        t = t.reshape(npair, 2, rc)
        return jnp.repeat(jnp.swapaxes(t, 1, 2), HEAD_DIM, axis=2)

    parts = [one(decay_f, False), one(decay_b, True)]
    dm = jnp.stack([p[0].reshape(npair, 2, rc, rc) for p in parts], axis=1)
    qd = jnp.stack([lanes(p[1]) for p in parts], axis=1)
    kd = jnp.stack([lanes(p[2]) for p in parts], axis=1)
    bd = _pair_block_diag((LANES, LANES))
    cm = jnp.stack([jnp.where(bd[None], jnp.repeat(p[3].reshape(npair, 2), HEAD_DIM, axis=1)[:, :, None], 0.0)
                    for p in parts], axis=1)
    return dm, qd, kd, cm


def _pair_states(s):
    b = s.shape[0]
    s = s.astype(F32).reshape(b, RET_HEADS // 2, 2, HEAD_DIM, HEAD_DIM)
    z = jnp.zeros_like(s[:, :, 0])
    top = jnp.concatenate([s[:, :, 0], z], axis=-1)
    bot = jnp.concatenate([z, s[:, :, 1]], axis=-1)
    return jnp.concatenate([top, bot], axis=-2)


def _unpair_states(s):
    b = s.shape[0]
    h0 = s[:, :, :HEAD_DIM, :HEAD_DIM]
    h1 = s[:, :, HEAD_DIM:, HEAD_DIM:]
    return jnp.stack([h0, h1], axis=2).reshape(b, RET_HEADS, HEAD_DIM, HEAD_DIM)


LOG2E = 1.4426950408889634


def _kv_variants(k, v):
    lane = lax.broadcasted_iota(jnp.int32, k.shape, 1)
    lo = lane < HEAD_DIM
    ka = jnp.where(lo, k, 0.0)
    kb = jnp.where(lo, 0.0, k)
    va = jnp.where(lane == HEAD_DIM, 1.0, jnp.where(lo, v, 0.0))
    vb = jnp.where(lane == 0, 1.0, jnp.where(lo, 0.0, v))
    ks = (ka, pltpu.roll(ka, HEAD_DIM, 1), pltpu.roll(kb, HEAD_DIM, 1), kb)
    vs = (va, pltpu.roll(va, HEAD_DIM, 1), pltpu.roll(vb, HEAD_DIM, 1), vb)
    return [t.astype(BF16) for t in ks], [t.astype(BF16) for t in vs]


def _attn_core(q, k, v, sink_ref, o_ref, mask, rows=slice(None)):
    tq = q.shape[0]
    ks, vs = _kv_variants(k, v)
    lane = lax.broadcasted_iota(jnp.int32, (2 * tq, LANES), 1)
    first = lax.broadcasted_iota(jnp.int32, (2 * tq, 1), 0) < tq
    if mask is not None:
        mask = jnp.concatenate([mask, mask], axis=0)
    for half in (0, 1):
        q2 = jnp.concatenate([q[:, (2 * half) * LANES:(2 * half + 1) * LANES],
                              q[:, (2 * half + 1) * LANES:(2 * half + 2) * LANES]], axis=0)
        q2 = (q2 * LOG2E).astype(BF16)
        out = None
        for hh in (0, 1):
            var = 2 * half + hh
            s = _dot_nt(q2, ks[var])
            if mask is not None:
                s = jnp.where(mask, s, NEG_INF)
            m = jnp.max(s, axis=-1, keepdims=True)
            if sink_ref is not None:
                snk = LOG2E * jnp.where(first, sink_ref[2 * half, hh:hh + 1, 0:1],
                                        sink_ref[2 * half + 1, hh:hh + 1, 0:1])
                m = jnp.maximum(m, snk)
            acc = _dot(jnp.exp2((s - m).astype(BF16)), vs[var])
            ones_lane = HEAD_DIM if hh == 0 else 0
            den = acc[:, ones_lane:ones_lane + 1]
            if sink_ref is not None:
                den = den + jnp.exp2(snk - m)
            own = (lane < HEAD_DIM) if hh == 0 else (lane >= HEAD_DIM)
            o = jnp.where(own, acc, 0.0) * (1.0 / den)
            out = o if out is None else out + o
        o_ref[rows, (2 * half) * LANES:(2 * half + 1) * LANES] = out[:tq]
        o_ref[rows, (2 * half + 1) * LANES:(2 * half + 2) * LANES] = out[tq:]


def _full_attn_kernel(q_ref, k_ref, v_ref, *rest, has_sink):
    if has_sink:
        sink_ref, o_ref = rest
    else:
        sink_ref, (o_ref,) = None, rest
    for s in range(PROMPT_SEQS_PER_STEP):
        r = slice(s * SEQ, (s + 1) * SEQ)
        _attn_core(q_ref[r, :], k_ref[r, :], v_ref[r, :], sink_ref, o_ref, None, r)


PROMPT_SEQS_PER_STEP = 4


def _full_attention(p, q_cb0, k_cb0, v_cb0, n_groups, sink=None):
    rows = PROMPT_SEQS_PER_STEP * SEQ
    qw = 4 * LANES
    in_specs = [
        pl.BlockSpec((rows, qw), lambda b, g: (b, q_cb0 + g)),
        pl.BlockSpec((rows, LANES), lambda b, g: (b, k_cb0 + g)),
        pl.BlockSpec((rows, LANES), lambda b, g: (b, v_cb0 + g)),
    ]
    args = [p, p, p]
    if sink is not None:
        in_specs.append(pl.BlockSpec((None, 4, 2, LANES), lambda b, g: (g, 0, 0, 0)))
        args.append(sink)
    return pl.pallas_call(
        functools.partial(_full_attn_kernel, has_sink=sink is not None),
        grid=(T_PROMPT // rows, n_groups),
        in_specs=in_specs,
        out_specs=pl.BlockSpec((rows, qw), lambda b, g: (b, g)),
        out_shape=jax.ShapeDtypeStruct((T_PROMPT, n_groups * qw), F32),
        compiler_params=_cparams(("arbitrary", "arbitrary")),
        name="full_attention",
    )(*args)


def _latent_attn_kernel(q_ref, k_ref, v_ref, kx_ref, vx_ref, o_ref, kvar, vvar, *, tq, kb):
    lk = DEC_SEQ + PAST_LEN

    @pl.when(pl.program_id(2) == 0)
    def _():
        for src_k, src_v, r0, nrows in ((k_ref, v_ref, 0, DEC_SEQ), (kx_ref, vx_ref, DEC_SEQ, PAST_LEN)):
            for c in range(nrows // kb):
                src = pl.ds(c * kb, kb)
                dst = pl.ds(r0 + c * kb, kb)
                ks, vs = _kv_variants(src_k[src, :], src_v[src, :])
                for i in range(4):
                    kvar[i, dst, :] = ks[i]
                    vvar[i, dst, :] = vs[i]

    lane = lax.broadcasted_iota(jnp.int32, (2 * tq, LANES), 1)
    for half in (0, 1):
        q2 = jnp.concatenate([q_ref[:, (2 * half) * LANES:(2 * half + 1) * LANES],
                              q_ref[:, (2 * half + 1) * LANES:(2 * half + 2) * LANES]], axis=0)
        q2 = (q2 * LOG2E).astype(BF16)
        out = None
        for hh in (0, 1):
            var = 2 * half + hh
            m = jnp.full((2 * tq, 1), -jnp.inf, F32)
            acc = jnp.zeros((2 * tq, LANES), F32)
            for j in range(lk // kb):
                rows = pl.ds(j * kb, kb)
                s = _dot_nt(q2, kvar[var, rows, :])
                m_new = jnp.maximum(m, jnp.max(s, axis=-1, keepdims=True))
                e = jnp.exp2((s - m_new).astype(BF16))
                acc = jnp.exp2(m - m_new) * acc + _dot(e, vvar[var, rows, :])
                m = m_new
            own = (lane < HEAD_DIM) if hh == 0 else (lane >= HEAD_DIM)
            ones_lane = HEAD_DIM if hh == 0 else 0
            o = jnp.where(own, acc, 0.0) * (1.0 / acc[:, ones_lane:ones_lane + 1])
            out = o if out is None else out + o
        o_ref[:, (2 * half) * LANES:(2 * half + 1) * LANES] = out[:tq]
        o_ref[:, (2 * half + 1) * LANES:(2 * half + 2) * LANES] = out[tq:]


def _latent_attention(p, kx, vx):
    tq, kb = 512, 512
    nq = DEC_SEQ // tq
    ngr = ATT_KV_HEADS // 2
    kcb = ATT_QW // LANES
    rb_q = T_PROMPT // tq
    rb_k = T_PROMPT // DEC_SEQ
    lk = DEC_SEQ + PAST_LEN
    ctx = pl.BlockSpec((None, PAST_LEN, LANES), lambda b, g, i: (b, 0, g))
    return pl.pallas_call(
        functools.partial(_latent_attn_kernel, tq=tq, kb=kb),
        grid=(DEC_BATCH, ngr, nq),
        in_specs=[pl.BlockSpec((tq, 4 * LANES), lambda b, g, i: (rb_q + b * nq + i, g)),
                  pl.BlockSpec((DEC_SEQ, LANES), lambda b, g, i: (rb_k + b, kcb + g)),
                  pl.BlockSpec((DEC_SEQ, LANES), lambda b, g, i: (rb_k + b, kcb + ngr + g)),
                  ctx, ctx],
        out_specs=pl.BlockSpec((tq, 4 * LANES), lambda b, g, i: (b * nq + i, g)),
        out_shape=jax.ShapeDtypeStruct((T_SAMPLE, ATT_QW), F32),
        scratch_shapes=[pltpu.VMEM((4, lk, LANES), BF16), pltpu.VMEM((4, lk, LANES), BF16)],
        compiler_params=_cparams(("arbitrary", "arbitrary", "arbitrary")),
        name="latent_attention",
    )(p, p, p, kx, vx)


WIN_TQ = 2 * BLK
WIN_KBLKS = WIN_TQ // BLK + 2


def _window_attn_kernel(q_ref, *refs):
    k_refs = refs[:WIN_KBLKS]
    v_refs = refs[WIN_KBLKS:2 * WIN_KBLKS]
    kx_ref, vx_ref, sink_ref, o_ref = refs[2 * WIN_KBLKS:]
    n = pl.program_id(1)
    k = jnp.concatenate([r[...] for r in k_refs] + [kx_ref[...]], axis=0)
    v = jnp.concatenate([r[...] for r in v_refs] + [vx_ref[...]], axis=0)
    nloc = WIN_KBLKS * BLK
    shape = (WIN_TQ, nloc + PAST_LEN)
    i = lax.broadcasted_iota(jnp.int32, shape, 0)
    r = lax.broadcasted_iota(jnp.int32, shape, 1)
    kpos = n * WIN_TQ - BLK + r
    local = (jnp.abs(r - BLK - i) <= WINDOW) & (kpos >= 0) & (kpos < DEC_SEQ)
    mask = local | (r >= nloc)
    _attn_core(q_ref[...], k, v, sink_ref, o_ref, mask)


def _window_attention(p, kx, vx, sink):
    nb = DEC_SEQ // BLK
    nq = DEC_SEQ // WIN_TQ
    rb0 = T_PROMPT // BLK
    qcb = (4 * RET_QW) // (4 * LANES)
    kcb = (4 * RET_QW + WIN_QW) // LANES
    vcb = kcb + 1

    def kv(cb, off):
        return pl.BlockSpec((BLK, LANES),
                            lambda b, n: (rb0 + b * nb + jnp.clip(n * (WIN_TQ // BLK) + off, 0, nb - 1), cb))

    offs = range(-1, WIN_KBLKS - 1)
    ctx = pl.BlockSpec((None, PAST_LEN, LANES), lambda b, n: (b, 0, 0))
    qspec = pl.BlockSpec((WIN_TQ, 4 * LANES), lambda b, n: (T_PROMPT // WIN_TQ + b * nq + n, qcb))
    return pl.pallas_call(
        _window_attn_kernel,
        grid=(DEC_BATCH, nq),
        in_specs=[qspec] + [kv(kcb, o) for o in offs] + [kv(vcb, o) for o in offs] + [
            ctx, ctx, pl.BlockSpec((4, 2, LANES), lambda b, n: (0, 0, 0))],
        out_specs=pl.BlockSpec((WIN_TQ, 4 * LANES), lambda b, n: (b * nq + n, 0)),
        out_shape=jax.ShapeDtypeStruct((T_SAMPLE, WIN_QW), F32),
        compiler_params=_cparams(("arbitrary", "arbitrary")),
        name="window_attention",
    )(*([p] * (1 + 2 * WIN_KBLKS)), kx, vx, sink)


def _sink_table(sink, n_groups):
    s = sink.astype(F32).reshape(n_groups, 4, 2, 1)
    return jnp.broadcast_to(s, (n_groups, 4, 2, LANES))


def _cache_rows(cache):
    b, h, l, d = cache.shape
    return cache.astype(F32).transpose(0, 2, 1, 3).reshape(b, l, h * d)


def _outproj_kernel(*refs, n_in):
    a_refs = refs[:2 * n_in]
    w_refs = refs[2 * n_in:3 * n_in]
    xp_ref, xs_ref, mod_ref, g_ref, wr_ref, xn_ref, hfp_ref, hfs_ref, lg_ref = refs[3 * n_in:]
    i = pl.program_id(0)
    y = None
    for k, w_ref in enumerate(w_refs):
        a = _pick(a_refs[2 * k], a_refs[2 * k + 1])
        t = _dot(a.astype(BF16), w_ref[...])
        y = t if y is None else y + t
    xn = _pick(xp_ref, xs_ref) + mod_ref[2:3, :] * _rms(y, g_ref[1:2, :])
    hf = _rms(xn, g_ref[2:3, :]) * (1.0 + mod_ref[4:5, :]) + mod_ref[3:4, :]
    xn_ref[...] = xn
    lg_ref[...] = _dot_nt(wr_ref[...], hf.astype(BF16))

    @pl.when(i < NPT)
    def _():
        hfp_ref[...] = hf

    @pl.when(i >= NPT)
    def _():
        hfs_ref[...] = hf.reshape(TM, D_MODEL // LANES, LANES)


def _outproj(parts, w_parts, x, mod, layer, gains, w_router_t):
    n_in = len(parts)
    in_specs = []
    args = []
    for a_p, a_s in parts:
        in_specs += _pair_specs(a_p.shape[1])
        args += [a_p, a_s]
    in_specs += [pl.BlockSpec(w.shape, lambda i: (0, 0)) for w in w_parts]
    in_specs += _pair_specs(D_MODEL) + [
        pl.BlockSpec((None, None, 6, D_MODEL), lambda i: (layer, _mod_index(i), 0, 0)),
        pl.BlockSpec((4, D_MODEL), lambda i: (0, 0)),
        pl.BlockSpec((N_EXPERTS, D_MODEL), lambda i: (0, 0)),
    ]
    row = pl.BlockSpec((TM, D_MODEL), lambda i: (i, 0))
    hf_p, _ = _pair_specs(D_MODEL)
    hf_s = pl.BlockSpec((TM, D_MODEL // LANES, LANES), lambda i: (jnp.maximum(i - NPT, 0), 0, 0))
    return pl.pallas_call(
        functools.partial(_outproj_kernel, n_in=n_in),
        grid=(T_ALL // TM,),
        in_specs=in_specs,
        out_specs=[row, hf_p, hf_s, pl.BlockSpec((N_EXPERTS, TM), lambda i: (0, i))],
        out_shape=[jax.ShapeDtypeStruct((T_ALL, D_MODEL), F32),
                   jax.ShapeDtypeStruct((T_PROMPT, D_MODEL), F32),
                   jax.ShapeDtypeStruct((T_SAMPLE, D_MODEL // LANES, LANES), F32),
                   jax.ShapeDtypeStruct((N_EXPERTS, T_ALL), F32)],
        compiler_params=_cparams(("arbitrary",)),
        name="outproj",
    )(*args, *w_parts, x[0], x[1], mod, gains, w_router_t)


TILE_SUB = D_MODEL // LANES


def _gather_kernel(idx_ref, h_ref, o_ref, buf, *, eg, cap):
    b = pl.program_id(0)
    g = pl.program_id(1)
    for e in range(eg):
        base = (b * N_EXPERTS + g * eg + e) * cap

        def body(c, carry):
            buf[c] = h_ref[idx_ref[base + c]]
            return carry

        lax.fori_loop(0, cap, body, 0, unroll=8)
        o_ref[e] = buf[...].reshape(cap, D_MODEL).astype(BF16)


def _gather(h3, idx, n_seq, seq_len, cap, eg):
    return pl.pallas_call(
        functools.partial(_gather_kernel, eg=eg, cap=cap),
        grid_spec=pltpu.PrefetchScalarGridSpec(
            num_scalar_prefetch=1,
            grid=(n_seq, N_EXPERTS // eg),
            in_specs=[pl.BlockSpec((seq_len, TILE_SUB, LANES), lambda b, g, idx: (b, 0, 0))],
            out_specs=pl.BlockSpec((eg, cap, D_MODEL), lambda b, g, idx: (g, b, 0)),
            scratch_shapes=[pltpu.VMEM((cap, TILE_SUB, LANES), F32)],
        ),
        out_shape=jax.ShapeDtypeStruct((N_EXPERTS, n_seq * cap, D_MODEL), BF16),
        compiler_params=_cparams(("arbitrary", "arbitrary")),
        name="moe_gather",
    )(idx, h3)


def _moe_kernel(xp_ref, xs_ref, wg_ref, wu_ref, wd_ref, gwp_ref, gws_ref, yp_ref, ys_ref, hid_scr, wd_scr):
    f = pl.program_id(1)
    wg = wg_ref[...].astype(BF16)
    wu = wu_ref[...].astype(BF16)
    cols = pl.ds(pl.multiple_of(f * TF, TF), TF)
    wd_scr[cols, :] = wd_ref[...].astype(BF16)
    for h, x_ref in enumerate((xp_ref, xs_ref)):
        x = x_ref[...]
        hid_scr[h, :, cols] = (_silu(_dot(x, wg)) * _dot(x, wu)).astype(BF16)

    @pl.when(f == pl.num_programs(1) - 1)
    def _():
        for h, (gw_ref, y_ref) in enumerate(((gwp_ref, yp_ref), (gws_ref, ys_ref))):
            for r in range(0, hid_scr.shape[1], MOE_ROW_TILE):
                rows = slice(r, r + MOE_ROW_TILE)
                y_ref[rows, :] = _dot(hid_scr[h, rows, :], wd_scr[...]) * gw_ref[rows, :]


MOE_ROW_TILE = 512


def _moe_ffn(xg_p, xg_s, w_gate, w_up, w_down, layer, gw_p, gw_s):
    assert ROWS_P == ROWS_S
    nf = EXPERT_FF // TF
    xspec_p = pl.BlockSpec((None, ROWS_P, D_MODEL), lambda e, f: (e, 0, 0))
    xspec_s = pl.BlockSpec((None, ROWS_S, D_MODEL), lambda e, f: (e, 0, 0))
    return pl.pallas_call(
        _moe_kernel,
        grid=(N_EXPERTS, nf),
        in_specs=[xspec_p, xspec_s,
                  pl.BlockSpec((None, None, D_MODEL, TF), lambda e, f: (layer, e, 0, f)),
                  pl.BlockSpec((None, None, D_MODEL, TF), lambda e, f: (layer, e, 0, f)),
                  pl.BlockSpec((None, None, TF, D_MODEL), lambda e, f: (layer, e, f, 0)),
                  pl.BlockSpec((None, ROWS_P, 1), lambda e, f: (e, 0, 0)),
                  pl.BlockSpec((None, ROWS_S, 1), lambda e, f: (e, 0, 0))],
        out_specs=[xspec_p, xspec_s],
        out_shape=[jax.ShapeDtypeStruct((N_EXPERTS, ROWS_P, D_MODEL), F32),
                   jax.ShapeDtypeStruct((N_EXPERTS, ROWS_S, D_MODEL), F32)],
        scratch_shapes=[pltpu.VMEM((2, ROWS_P, EXPERT_FF), BF16),
                        pltpu.VMEM((EXPERT_FF, D_MODEL), BF16)],
        compiler_params=_cparams(("arbitrary", "arbitrary")),
        name="moe_ffn",
    )(xg_p, xg_s, w_gate, w_up, w_down, gw_p, gw_s)


COMBINE_GROUP = 8


def _combine_kernel(idx_ref, y_ref, o_ref, y3, *, eg, cap):
    b = pl.program_id(0)
    g = pl.program_id(1)

    @pl.when(g == 0)
    def _():
        o_ref[...] = jnp.zeros_like(o_ref)

    for e in range(eg):
        base = (b * N_EXPERTS + g * eg + e) * cap
        y3[...] = y_ref[e].reshape(cap, TILE_SUB, LANES)

        def body(c, carry):
            c0 = c * COMBINE_GROUP
            toks = [idx_ref[base + c0 + k] for k in range(COMBINE_GROUP)]
            rows = [o_ref[t] + y3[c0 + k] for k, t in enumerate(toks)]
            for t, row in zip(toks, rows):
                o_ref[t] = row
            return carry

        lax.fori_loop(0, cap // COMBINE_GROUP, body, 0)


def _combine(y, idx, n_seq, seq_len, cap, eg):
    return pl.pallas_call(
        functools.partial(_combine_kernel, eg=eg, cap=cap),
        grid_spec=pltpu.PrefetchScalarGridSpec(
            num_scalar_prefetch=1,
            grid=(n_seq, N_EXPERTS // eg),
            in_specs=[pl.BlockSpec((eg, cap, D_MODEL), lambda b, g, idx: (g, b, 0))],
            out_specs=pl.BlockSpec((seq_len, TILE_SUB, LANES), lambda b, g, idx: (b, 0, 0)),
            scratch_shapes=[pltpu.VMEM((cap, TILE_SUB, LANES), F32)],
        ),
        out_shape=jax.ShapeDtypeStruct((n_seq * seq_len, TILE_SUB, LANES), F32),
        compiler_params=_cparams(("arbitrary", "arbitrary")),
        name="moe_combine",
    )(idx, y)


def _onehot_t(idx_row, n_tok):
    tok = lax.broadcasted_iota(jnp.int32, (n_tok, idx_row.shape[-1]), 0)
    return jnp.where(tok == idx_row, 1.0, 0.0)


def _gather_mm_kernel(idx_ref, h_ref, o_ref, *, cap, seq_len):
    for s in range(PROMPT_SEQS_PER_STEP):
        sel = _onehot_t(idx_ref[s], seq_len).T.astype(BF16)
        h = h_ref[s * seq_len:(s + 1) * seq_len, :].astype(BF16)
        rows = _dot(sel, h).astype(BF16)
        for e in range(N_EXPERTS):
            o_ref[e, s * cap:(s + 1) * cap, :] = rows[e * cap:(e + 1) * cap]


def _gather_mm(h, idx, n_seq, seq_len, cap):
    ns = PROMPT_SEQS_PER_STEP
    return pl.pallas_call(
        functools.partial(_gather_mm_kernel, cap=cap, seq_len=seq_len),
        grid=(n_seq // ns,),
        in_specs=[pl.BlockSpec((ns, 1, N_EXPERTS * cap), lambda b: (b, 0, 0)),
                  pl.BlockSpec((ns * seq_len, D_MODEL), lambda b: (b, 0))],
        out_specs=pl.BlockSpec((N_EXPERTS, ns * cap, D_MODEL), lambda b: (0, b, 0)),
        out_shape=jax.ShapeDtypeStruct((N_EXPERTS, n_seq * cap, D_MODEL), BF16),
        compiler_params=_cparams(("arbitrary",)),
        name="moe_gather_mm",
    )(idx, h)


def _combine_mm_kernel(idx_ref, y_ref, o_ref, *, cap, seq_len):
    for s in range(PROMPT_SEQS_PER_STEP):
        sel = _onehot_t(idx_ref[s], seq_len).astype(BF16)
        y = jnp.concatenate([y_ref[e, s * cap:(s + 1) * cap, :] for e in range(N_EXPERTS)], axis=0)
        hi = y.astype(BF16)
        lo = (y - hi.astype(F32)).astype(BF16)
        o_ref[s * seq_len:(s + 1) * seq_len, :] = _dot(sel, hi) + _dot(sel, lo)


def _combine_mm(y, idx, n_seq, seq_len, cap):
    ns = PROMPT_SEQS_PER_STEP
    return pl.pallas_call(
        functools.partial(_combine_mm_kernel, cap=cap, seq_len=seq_len),
        grid=(n_seq // ns,),
        in_specs=[pl.BlockSpec((ns, 1, N_EXPERTS * cap), lambda b: (b, 0, 0)),
                  pl.BlockSpec((N_EXPERTS, ns * cap, D_MODEL), lambda b: (0, b, 0))],
        out_specs=pl.BlockSpec((ns * seq_len, D_MODEL), lambda b: (b, 0)),
        out_shape=jax.ShapeDtypeStruct((n_seq * seq_len, D_MODEL), F32),
        compiler_params=_cparams(("arbitrary",)),
        name="moe_combine_mm",
    )(idx, y)


def _postnorm_kernel(x_ref, yp_ref, ys_ref, mod_ref, g_ref, op_ref, os_ref):
    i = pl.program_id(0)
    y = jnp.where(i < NPT, yp_ref[...], ys_ref[...].reshape(TM, D_MODEL))
    o = x_ref[...] + mod_ref[5:6, :] * _rms(y, g_ref[3:4, :])

    @pl.when(i < NPT)
    def _():
        op_ref[...] = o

    @pl.when(i >= NPT)
    def _():
        os_ref[...] = o


def _postnorm(x, y_p, y_s, mod, layer, gains):
    return pl.pallas_call(
        _postnorm_kernel,
        grid=(T_ALL // TM,),
        in_specs=[pl.BlockSpec((TM, D_MODEL), lambda i: (i, 0)), _pair_specs(D_MODEL)[0],
                  pl.BlockSpec((TM, TILE_SUB, LANES), lambda i: (jnp.maximum(i - NPT, 0), 0, 0)),
                  pl.BlockSpec((None, None, 6, D_MODEL), lambda i: (layer, _mod_index(i), 0, 0)),
            pl.BlockSpec((4, D_MODEL), lambda i: (0, 0))],
        out_specs=_pair_specs(D_MODEL),
        out_shape=[jax.ShapeDtypeStruct((T_PROMPT, D_MODEL), F32),
                   jax.ShapeDtypeStruct((T_SAMPLE, D_MODEL), F32)],
        compiler_params=_cparams(("arbitrary",)),
        name="postnorm",
    )(x, y_p, y_s, mod, gains)


def _excl_prefix(mask):
    r, n = mask.shape
    i0 = lax.broadcasted_iota(jnp.int32, (LANES, LANES), 0)
    i1 = lax.broadcasted_iota(jnp.int32, (LANES, LANES), 1)
    upper = jnp.where(i0 < i1, 1.0, 0.0).astype(BF16)
    ones = jnp.where(mask, 1.0, 0.0)
    carry = jnp.zeros((r, 1), F32)
    out = []
    for c in range(n // LANES):
        ch = ones[:, c * LANES:(c + 1) * LANES]
        out.append(_dot(ch.astype(BF16), upper) + carry)
        carry = carry + jnp.sum(ch, axis=1, keepdims=True)
    return jnp.concatenate(out, axis=1)


def _route_kernel(lg_ref, idx_ref, gw_ref, *, nb, n, cap):
    affs = []
    for s in range(nb):
        l = lg_ref[:, s * n:(s + 1) * n]
        e = jnp.exp(l - jnp.max(l, axis=0, keepdims=True))
        affs.append(e / jnp.sum(e, axis=0, keepdims=True))
    aff = affs[0] if nb == 1 else jnp.concatenate(affs, axis=0)
    rows = nb * N_EXPERTS
    bits = jnp.zeros((rows, 1), jnp.int32)
    for bit in range(30, -1, -1):
        cand = bits | (1 << bit)
        cnt = jnp.sum(jnp.where(aff >= pltpu.bitcast(cand, F32), 1.0, 0.0), axis=1, keepdims=True)
        bits = jnp.where(cnt >= cap, cand, bits)
    thr = pltpu.bitcast(bits, F32)
    gt = aff > thr
    eq = aff == thr
    need = cap - jnp.sum(jnp.where(gt, 1.0, 0.0), axis=1, keepdims=True)
    sel = gt | (eq & (_excl_prefix(eq) < need))
    pos = jnp.where(sel, _excl_prefix(sel), -1.0)
    tok = lax.broadcasted_iota(jnp.int32, (rows, n), 1).astype(F32)
    capw = max(cap, LANES)
    slot_lane = lax.broadcasted_iota(jnp.int32, (rows, capw), 1)

    def slot(s, carry):
        idx_acc, gw_acc = carry
        hit = pos == lax.convert_element_type(s, F32)
        ic = jnp.sum(jnp.where(hit, tok, 0.0), axis=1, keepdims=True)
        gc = jnp.sum(jnp.where(hit, aff, 0.0), axis=1, keepdims=True)
        here = slot_lane == s
        return jnp.where(here, ic, idx_acc), jnp.where(here, gc, gw_acc)

    zero = jnp.zeros((rows, capw), F32)
    idx_acc, gw_acc = lax.fori_loop(0, cap, slot, (zero, zero), unroll=8)
    for s in range(nb):
        r = slice(s * N_EXPERTS, (s + 1) * N_EXPERTS)
        idx_ref[s] = idx_acc[r, :cap].astype(jnp.int32)
        gw_ref[s] = gw_acc[r, :cap]


def _route(logits_t, n_seq, seq_len, cap, col0, nb):
    cb0 = col0 // (nb * seq_len)
    out = pl.BlockSpec((nb, N_EXPERTS, cap), lambda b: (b, 0, 0))
    return pl.pallas_call(
        functools.partial(_route_kernel, nb=nb, n=seq_len, cap=cap),
        grid=(n_seq // nb,),
        in_specs=[pl.BlockSpec((N_EXPERTS, nb * seq_len), lambda b: (0, cb0 + b))],
        out_specs=[out, out],
        out_shape=[jax.ShapeDtypeStruct((n_seq, N_EXPERTS, cap), jnp.int32),
                   jax.ShapeDtypeStruct((n_seq, N_EXPERTS, cap), F32)],
        compiler_params=_cparams(("arbitrary",)),
        name="route",
    )(logits_t)


def _expert_ffn(hf_p, hf_s, logits_t, w_gate, w_up, w_down, layer):
    idx_p, gw_p = _route(logits_t, BATCH, SEQ, CAP_P, 0, 8)
    idx_s, gw_s = _route(logits_t, DEC_BATCH, DEC_SEQ, CAP_S, T_PROMPT, 1)
    idx_p = idx_p.reshape(BATCH, 1, N_EXPERTS * CAP_P)
    idx_s = idx_s.reshape(DEC_BATCH * N_EXPERTS * CAP_S)
    xg_p = _gather_mm(hf_p, idx_p, BATCH, SEQ, CAP_P)
    xg_s = _gather(hf_s, idx_s, DEC_BATCH, DEC_SEQ, CAP_S, 4)
    gw_p = gw_p.transpose(1, 0, 2).reshape(N_EXPERTS, ROWS_P, 1)
    gw_s = gw_s.transpose(1, 0, 2).reshape(N_EXPERTS, ROWS_S, 1)
    y_p, y_s = _moe_ffn(xg_p, xg_s, w_gate, w_up, w_down, layer, gw_p, gw_s)
    o_p = _combine_mm(y_p, idx_p, BATCH, SEQ, CAP_P)
    o_s = _combine(y_s, idx_s, DEC_BATCH, DEC_SEQ, CAP_S, 2)
    return o_p, o_s


def _rope_tables():
    t = np.arange(DEC_SEQ)
    row = (t // GRID_W).astype(np.float32)
    col = (t % GRID_W).astype(np.float32)
    inv = (ROPE_THETA ** (-np.arange(0, AXIS_DIM, 2, dtype=np.float32) / AXIS_DIM)).astype(np.float32)
    ar = row[:, None] * inv[None]
    ac = col[:, None] * inv[None]
    ang = np.concatenate([ar, ar, ac, ac], axis=-1)
    ang = np.concatenate([ang, ang], axis=-1)
    cos = np.concatenate([np.ones((TM, LANES), np.float32), np.cos(ang)], axis=0)
    sin = np.concatenate([np.zeros((TM, LANES), np.float32), np.sin(ang)], axis=0)
    return jnp.asarray(cos, F32), jnp.asarray(sin, F32)


def _split_kv_heads(rows, n_heads):
    return rows.reshape(BATCH, SEQ, n_heads, HEAD_DIM).transpose(0, 2, 1, 3)[:, None]


def kernel(x_prompt, x_sample, state_ret_fwd, state_ret_bwd, cache_win_k, cache_win_v, cache_attn_k, cache_attn_v, c, c_ctx, w_ada, b_ada, norm_gains, w_in_even, w_out_even, ret_decay_fwd, ret_decay_bwd, win_sink, w_in_odd, w_out_odd, q_norm, k_norm, w_router, w_gate, w_up, w_down):
    x = (x_prompt.reshape(T_PROMPT, D_MODEL), x_sample.reshape(T_SAMPLE, D_MODEL))
    cond =jnp.concatenate([c_ctx[None, :], c, jnp.zeros((8 - 1 - DEC_BATCH, D_MODEL), F32)], axis=0)
    mod = _modulation(cond, w_ada, b_ada).reshape(DEPTH, 8, 6, D_MODEL)
    cos_t, sin_t = _rope_tables()
    outs = {}
    for layer in range(DEPTH):
        j = layer // 2
        gains = norm_gains[layer]
        if layer % 2 == 0:
            p = _inproj(x, mod, layer, gains[0], w_in_even[j].astype(BF16), cos_t, sin_t)
            tabs = _retention_tables(ret_decay_fwd[j], ret_decay_bwd[j])
            zero = jnp.zeros((BATCH, RET_HEADS // 2, LANES, LANES), F32)
            ret_p, sf, sb = _retention(p, tabs, zero, zero, BATCH, SEQ, 0, 4)
            ret_s, _, _ = _retention(p, tabs, _pair_states(state_ret_fwd[:, j]),
                                     _pair_states(state_ret_bwd[:, j]), DEC_BATCH, DEC_SEQ, T_PROMPT, 2)
            sink = _sink_table(win_sink[j], 1)
            qcb = (4 * RET_QW) // (4 * LANES)
            kcb = (4 * RET_QW + WIN_QW) // LANES
            win_p = _full_attention(p, qcb, kcb, kcb + 1, 1, sink=sink)
            win_s = _window_attention(p, _cache_rows(cache_win_k[:, j]), _cache_rows(cache_win_v[:, j]),
                                      sink[0])
            w_out = w_out_even[j].astype(BF16)
            parts = [(ret_p, ret_s), (win_p, win_s)]
            w_parts = [w_out[:RET_QW], w_out[RET_QW:]]
            outs["ret_f"] = _unpair_states(sf)[:, None]
            outs["ret_b"] = _unpair_states(sb)[:, None]
            c0 = 4 * RET_QW + WIN_QW
            outs["win_k"] = _split_kv_heads(p[:T_PROMPT, c0:c0 + WIN_KW], WIN_KV_HEADS)
            outs["win_v"] = _split_kv_heads(p[:T_PROMPT, c0 + WIN_KW:c0 + 2 * WIN_KW], WIN_KV_HEADS)
        else:
            p = _inproj(x, mod, layer, gains[0], w_in_odd[j].astype(BF16), cos_t, sin_t,
                        qn=q_norm[j], kn=k_norm[j])
            ngr = ATT_KV_HEADS // 2
            kcb = ATT_QW // LANES
            att_p = _full_attention(p, 0, kcb, kcb + ngr, ngr)
            att_s = _latent_attention(p, _cache_rows(cache_attn_k[:, j]), _cache_rows(cache_attn_v[:, j]))
            parts = [(att_p, att_s)]
            w_parts = [w_out_odd[j].astype(BF16)]
            outs["att_k"] = _split_kv_heads(p[:T_PROMPT, ATT_QW:ATT_QW + ATT_KW], ATT_KV_HEADS)
            outs["att_v"] = _split_kv_heads(p[:T_PROMPT, ATT_QW + ATT_KW:], ATT_KV_HEADS)
        xn, hf_p, hf_s, logits_t = _outproj(parts, w_parts, x, mod, layer, gains,
                                            w_router[layer].T.astype(BF16))
        o_p, o_s = _expert_ffn(hf_p, hf_s, logits_t, w_gate, w_up, w_down, layer)
        x = _postnorm(xn, o_p, o_s, mod, layer, gains)
    y_prompt = x[0].reshape(BATCH, SEQ, D_MODEL)
    y_sample = x[1].reshape(DEC_BATCH, DEC_SEQ, D_MODEL)
    return (y_prompt, y_sample, outs["ret_f"], outs["ret_b"], outs["win_k"], outs["win_v"],
            outs["att_k"], outs["att_v"])
```

```python
import functools

import jax
import jax.numpy as jnp
import numpy as np
from jax import lax
from jax.experimental import pallas as pl
from jax.experimental.pallas import tpu as pltpu

D_MODEL = 1024
BATCH = 32
SEQ = 256
DEPTH = 2
DEC_BATCH = 2
DEC_SEQ = 4096
PAST_LEN = 512
GRID_W = 64
HEAD_DIM = 64
AXIS_DIM = HEAD_DIM // 2
ROPE_THETA = 10000.0
BLK = 128
WINDOW = 128
RET_HEADS = D_MODEL // 128
WIN_HEADS = D_MODEL // 128
WIN_KV_HEADS = WIN_HEADS // 4
ATT_HEADS = D_MODEL // HEAD_DIM
ATT_KV_HEADS = ATT_HEADS // 4
RET_QW = RET_HEADS * HEAD_DIM
WIN_QW = WIN_HEADS * HEAD_DIM
WIN_KW = WIN_KV_HEADS * HEAD_DIM
EVEN_IN = 4 * RET_QW + WIN_QW + 2 * WIN_KW
ATT_QW = ATT_HEADS * HEAD_DIM
ATT_KW = ATT_KV_HEADS * HEAD_DIM
ODD_IN = ATT_QW + 2 * ATT_KW
N_EXPERTS = 16
CAPACITY_FACTOR = 2
EXPERT_FF = ((8 * D_MODEL // 3 + 127) // 128) * 128
EPS = 1e-6
NEG_INF = -1e30
F32 = jnp.float32
BF16 = jnp.bfloat16

LANES = 128
T_PROMPT = BATCH * SEQ
T_SAMPLE = DEC_BATCH * DEC_SEQ
T_ALL = T_PROMPT + T_SAMPLE
TM = 512
CAP_P = CAPACITY_FACTOR * SEQ // N_EXPERTS
CAP_S = CAPACITY_FACTOR * DEC_SEQ // N_EXPERTS
ROWS_P = BATCH * CAP_P
ROWS_S = DEC_BATCH * CAP_S
TF = 256
VMEM_LIMIT = 56 * 1024 * 1024
QK_SCALE = HEAD_DIM ** -0.5


def _cparams(sem):
    return pltpu.CompilerParams(dimension_semantics=sem, vmem_limit_bytes=VMEM_LIMIT)


def _silu(x):
    return x * (1.0 / (1.0 + jnp.exp(-x)))


def _dot(a, b):
    return jnp.dot(a, b, preferred_element_type=F32)


def _dot_nt(a, b):
    return lax.dot_general(a, b, (((1,), (1,)), ((), ())), preferred_element_type=F32)


def _rms(x, g):
    return x * lax.rsqrt(jnp.mean(x * x, axis=-1, keepdims=True) + EPS) * g


def _pair_block_diag(shape):
    r = lax.broadcasted_iota(jnp.int32, shape, 0)
    c = lax.broadcasted_iota(jnp.int32, shape, 1)
    return (r // HEAD_DIM) == (c // HEAD_DIM)


def _head_sum(x, bd):
    hi = x.astype(BF16)
    lo = (x - hi.astype(F32)).astype(BF16)
    return _dot(hi, bd) + _dot(lo, bd)


def _rope(xb, cos, sin):
    lane = lax.broadcasted_iota(jnp.int32, xb.shape, 1)
    half = AXIS_DIM // 2
    rot = jnp.where((lane % AXIS_DIM) < half,
                    -pltpu.roll(xb, LANES - half, 1), pltpu.roll(xb, half, 1))
    return xb * cos + rot * sin


def _mod_kernel(c_ref, w_ref, b_ref, o_ref):
    o_ref[...] = _dot(_silu(c_ref[...]), w_ref[...]) + b_ref[...]


def _modulation(cond, w_ada, b_ada):
    tn = 1536
    n = 6 * D_MODEL
    return pl.pallas_call(
        _mod_kernel,
        grid=(DEPTH, n // tn),
        in_specs=[
            pl.BlockSpec((8, D_MODEL), lambda l, j: (0, 0)),
            pl.BlockSpec((None, D_MODEL, tn), lambda l, j: (l, 0, j)),
            pl.BlockSpec((None, 1, tn), lambda l, j: (l, 0, j)),
        ],
        out_specs=pl.BlockSpec((None, 8, tn), lambda l, j: (l, 0, j)),
        out_shape=jax.ShapeDtypeStruct((DEPTH, 8, n), F32),
        compiler_params=_cparams(("arbitrary", "arbitrary")),
        name="modulation",
    )(cond, w_ada, b_ada.reshape(DEPTH, 1, n))


def _mod_index(i):
    npt = T_PROMPT // TM
    return jnp.where(i < npt, 0, 1 + (i - npt) // (DEC_SEQ // TM))


def _rope_index(i):
    npt = T_PROMPT // TM
    return jnp.where(i < npt, 0, 1 + (i - npt) % (DEC_SEQ // TM))


NPT = T_PROMPT // TM


def _pair_specs(width):
    return [pl.BlockSpec((TM, width), lambda i: (jnp.minimum(i, NPT - 1), 0)),
            pl.BlockSpec((TM, width), lambda i: (jnp.maximum(i - NPT, 0), 0))]


def _pick(p_ref, s_ref):
    return jnp.where(pl.program_id(0) < NPT, p_ref[...], s_ref[...])


def _inproj_kernel(xp_ref, xs_ref, mod_ref, g_ref, w_ref, cos_ref, sin_ref, *rest, even):
    if even:
        o_ref, ko_ref, vo_ref = rest
    else:
        qn_ref, kn_ref, o_ref, ko_ref, vo_ref = rest
    h = _rms(_pick(xp_ref, xs_ref), g_ref[...]) * (1.0 + mod_ref[1:2, :]) + mod_ref[0:1, :]
    h16 = h.astype(BF16)
    cos = cos_ref[...]
    sin = sin_ref[...]
    nblk = w_ref.shape[1] // LANES
    if even:
        nq = RET_QW // LANES
        rope_blocks = set(range(0, 2 * nq)) | set(range(4 * nq, 5 * nq + 1))
        scaled = set(range(nq, 2 * nq)) | set(range(4 * nq, 5 * nq))
        normed = {}
        cache_out = {5 * nq: (ko_ref, 0), 5 * nq + 1: (vo_ref, 0)}
    else:
        nq = ATT_QW // LANES
        nk = ATT_KW // LANES
        rope_blocks = set(range(0, nq + nk))
        scaled = set(range(0, nq))
        normed = {b: (qn_ref if b < nq else kn_ref) for b in range(nq + nk)}
        bd = _pair_block_diag((LANES, LANES)).astype(BF16)
        cache_out = {nq + b: (ko_ref, 2 * b) for b in range(nk)}
        cache_out.update({nq + nk + b: (vo_ref, 2 * b) for b in range(nk)})
    p = _dot(h16, w_ref[...])
    for b in range(nblk):
        blk = p[:, b * LANES:(b + 1) * LANES]
        if b in normed:
            ms = _head_sum(blk * blk, bd) * (1.0 / HEAD_DIM)
            blk = blk * lax.rsqrt(ms + EPS) * normed[b][...]
        if b in rope_blocks:
            blk = _rope(blk, cos, sin)
        if b in scaled:
            blk = blk * QK_SCALE
        o_ref[:, b * LANES:(b + 1) * LANES] = blk
        if b in cache_out:
            c_ref, head0 = cache_out[b]

            @pl.when(pl.program_id(0) < NPT)
            def _():
                for s in range(TM // SEQ):
                    for hh in range(2):
                        c_ref[s, 0, head0 + hh] = blk[s * SEQ:(s + 1) * SEQ, hh * HEAD_DIM:(hh + 1) * HEAD_DIM]


def _inproj(x, mod, layer, gain, w, cos_t, sin_t, qn=None, kn=None):
    even = qn is None
    n = w.shape[1]
    in_specs = _pair_specs(D_MODEL) + [
        pl.BlockSpec((None, None, 6, D_MODEL), lambda i: (layer, _mod_index(i), 0, 0)),
        pl.BlockSpec((1, D_MODEL), lambda i: (0, 0)),
        pl.BlockSpec((D_MODEL, n), lambda i: (0, 0)),
        pl.BlockSpec((TM, LANES), lambda i: (_rope_index(i), 0)),
        pl.BlockSpec((TM, LANES), lambda i: (_rope_index(i), 0)),
    ]
    args = [x[0], x[1], mod, gain.reshape(1, D_MODEL), w, cos_t, sin_t]
    if not even:
        in_specs += [pl.BlockSpec((1, LANES), lambda i: (0, 0))] * 2
        args += [jnp.tile(qn, 2).reshape(1, LANES), jnp.tile(kn, 2).reshape(1, LANES)]
    nkv = WIN_KV_HEADS if even else ATT_KV_HEADS
    cache = pl.BlockSpec((TM // SEQ, 1, nkv, SEQ, HEAD_DIM), lambda i: (jnp.minimum(i, NPT - 1), 0, 0, 0, 0))
    cache_shape = jax.ShapeDtypeStruct((BATCH, 1, nkv, SEQ, HEAD_DIM), F32)
    return pl.pallas_call(
        functools.partial(_inproj_kernel, even=even),
        grid=(T_ALL // TM,),
        in_specs=in_specs,
        out_specs=[pl.BlockSpec((TM, n), lambda i: (i, 0)), cache, cache],
        out_shape=[jax.ShapeDtypeStruct((T_ALL, n), F32), cache_shape, cache_shape],
        compiler_params=_cparams(("arbitrary",)),
        name="inproj_even" if even else "inproj_odd",
    )(*args)


RET_READOUT_ROWS = 512
RET_CHUNK = 2 * BLK


def _ret_kernel(q_ref, k_ref, v_ref, g_ref, dm_ref, qd_ref, kd_ref, cm_ref, s0f_ref, s0b_ref,
                o_ref, sf_ref, sb_ref, ob_scr, *, nc, npb):
    rc = RET_CHUNK
    lane = lax.broadcasted_iota(jnp.int32, (rc, LANES), 1)
    lo = lane < HEAD_DIM
    bd = _pair_block_diag((LANES, LANES))
    bd16 = bd.astype(BF16)
    sf_ref[...] = s0f_ref[...]
    sb_ref[...] = s0b_ref[...]

    def chunk(c, j, d, s_ref, dst):
        r = pl.ds(pl.multiple_of(c * rc, rc), rc)
        cols = slice(j * LANES, (j + 1) * LANES)
        qc = q_ref[r, cols]
        kc = k_ref[r, cols]
        vc = v_ref[r, cols]
        s = s_ref[j]
        a = _dot_nt(jnp.concatenate([jnp.where(lo, qc, 0.0), jnp.where(lo, 0.0, qc)], axis=0), kc)
        lhs = jnp.concatenate([a[:rc] * dm_ref[j, d, 0], a[rc:] * dm_ref[j, d, 1], qc * qd_ref[j, d]], axis=1)
        rhs = jnp.concatenate([jnp.where(lo, vc, 0.0), jnp.where(lo, 0.0, vc), s], axis=0)
        dst[r, cols] = _dot(lhs, rhs)
        kv = _dot((kc * kd_ref[j, d]).T, vc)
        s_ref[j] = s * cm_ref[j, d] + jnp.where(bd, kv, 0.0)

    def body(c, carry):
        for j in range(npb):
            chunk(c, j, 0, sf_ref, o_ref)
            chunk(nc - 1 - c, j, 1, sb_ref, ob_scr)
        return carry

    lax.fori_loop(0, nc, body, 0)

    rt = min(nc * rc, RET_READOUT_ROWS)

    def readout(c, carry):
        r = pl.ds(pl.multiple_of(c * rt, rt), rt)
        for j in range(npb):
            cols = slice(j * LANES, (j + 1) * LANES)
            o = o_ref[r, cols] + ob_scr[r, cols]
            mu = _head_sum(o, bd16) * (1.0 / HEAD_DIM)
            dlt = o - mu
            var = _head_sum(dlt * dlt, bd16) * (1.0 / HEAD_DIM)
            o_ref[r, cols] = dlt * lax.rsqrt(var + EPS) * _silu(g_ref[r, cols])
        return carry

    lax.fori_loop(0, nc * rc // rt, readout, 0)


def _retention(p, tabs, s0f, s0b, n_seq, seq_len, row0, npb):
    npair = RET_HEADS // 2
    ng = npair // npb
    rb0 = row0 // seq_len
    dm, qd, kd, cm = tabs
    w = npb * LANES

    def col(c0):
        return pl.BlockSpec((seq_len, w), lambda b, j: (rb0 + b, c0 + j))

    rc = RET_CHUNK
    tab_rows = pl.BlockSpec((npb, 2, rc, LANES), lambda b, j: (j, 0, 0, 0))
    tab_state = pl.BlockSpec((npb, 2, LANES, LANES), lambda b, j: (j, 0, 0, 0))
    st = pl.BlockSpec((None, npb, LANES, LANES), lambda b, j: (b, j, 0, 0))
    return pl.pallas_call(
        functools.partial(_ret_kernel, nc=seq_len // rc, npb=npb),
        grid=(n_seq, ng),
        in_specs=[col(0), col(ng), col(2 * ng), col(3 * ng),
                  pl.BlockSpec((npb, 2, 2, rc, rc), lambda b, j: (j, 0, 0, 0, 0)),
                  tab_rows, tab_rows, tab_state, st, st],
        out_specs=[pl.BlockSpec((seq_len, w), lambda b, j: (b, j)), st, st],
        out_shape=[jax.ShapeDtypeStruct((n_seq * seq_len, RET_QW), F32),
                   jax.ShapeDtypeStruct((n_seq, npair, LANES, LANES), F32),
                   jax.ShapeDtypeStruct((n_seq, npair, LANES, LANES), F32)],
        scratch_shapes=[pltpu.VMEM((seq_len, w), F32)],
        compiler_params=_cparams(("arbitrary", "arbitrary")),
        name="retention",
    )(p, p, p, p, dm, qd, kd, cm, s0f, s0b)


def _retention_tables(decay_f, decay_b):
    npair = RET_HEADS // 2
    rc = RET_CHUNK
    idx = jnp.arange(rc, dtype=F32)
    diff = idx[:, None] - idx[None, :]

    def one(decay, backward):
        lg = jax.nn.log_sigmoid(decay.astype(F32))
        dmask = jnp.where(diff >= 0, jnp.exp(lg[:, None, None] * jnp.maximum(diff, 0.0)), 0.0)
        q_dec = jnp.exp(lg[:, None] * (idx + 1.0))
        k_dec = jnp.exp(lg[:, None] * (rc - 1.0 - idx))
        c_dec = jnp.exp(lg * rc)
        if backward:
            dmask = jnp.swapaxes(dmask, 1, 2)
            q_dec = q_dec[:, ::-1]
            k_dec = k_dec[:, ::-1]
        return dmask, q_dec, k_dec, c_dec

    def lanes(t):
        t = t.reshape(npair, 2, rc)
        return jnp.repeat(jnp.swapaxes(t, 1, 2), HEAD_DIM, axis=2)

    parts = [one(decay_f, False), one(decay_b, True)]
    dm = jnp.stack([p[0].reshape(npair, 2, rc, rc) for p in parts], axis=1)
    qd = jnp.stack([lanes(p[1]) for p in parts], axis=1)
    kd = jnp.stack([lanes(p[2]) for p in parts], axis=1)
    bd = _pair_block_diag((LANES, LANES))
    cm = jnp.stack([jnp.where(bd[None], jnp.repeat(p[3].reshape(npair, 2), HEAD_DIM, axis=1)[:, :, None], 0.0)
                    for p in parts], axis=1)
    return dm, qd, kd, cm


def _pair_states(s):
    b = s.shape[0]
    s = s.astype(F32).reshape(b, RET_HEADS // 2, 2, HEAD_DIM, HEAD_DIM)
    z = jnp.zeros_like(s[:, :, 0])
    top = jnp.concatenate([s[:, :, 0], z], axis=-1)
    bot = jnp.concatenate([z, s[:, :, 1]], axis=-1)
    return jnp.concatenate([top, bot], axis=-2)


def _unpair_states(s):
    b = s.shape[0]
    h0 = s[:, :, :HEAD_DIM, :HEAD_DIM]
    h1 = s[:, :, HEAD_DIM:, HEAD_DIM:]
    return jnp.stack([h0, h1], axis=2).reshape(b, RET_HEADS, HEAD_DIM, HEAD_DIM)


LOG2E = 1.4426950408889634


def _kv_variants(k, v):
    lane = lax.broadcasted_iota(jnp.int32, k.shape, 1)
    lo = lane < HEAD_DIM
    ka = jnp.where(lo, k, 0.0)
    kb = jnp.where(lo, 0.0, k)
    va = jnp.where(lane == HEAD_DIM, 1.0, jnp.where(lo, v, 0.0))
    vb = jnp.where(lane == 0, 1.0, jnp.where(lo, 0.0, v))
    ks = (ka, pltpu.roll(ka, HEAD_DIM, 1), pltpu.roll(kb, HEAD_DIM, 1), kb)
    vs = (va, pltpu.roll(va, HEAD_DIM, 1), pltpu.roll(vb, HEAD_DIM, 1), vb)
    return [t.astype(BF16) for t in ks], [t.astype(BF16) for t in vs]


def _attn_core(q, k, v, sink_ref, o_ref, mask, rows=slice(None)):
    tq = q.shape[0]
    ks, vs = _kv_variants(k, v)
    lane = lax.broadcasted_iota(jnp.int32, (2 * tq, LANES), 1)
    first = lax.broadcasted_iota(jnp.int32, (2 * tq, 1), 0) < tq
    if mask is not None:
        mask = jnp.concatenate([mask, mask], axis=0)
    for half in (0, 1):
        q2 = jnp.concatenate([q[:, (2 * half) * LANES:(2 * half + 1) * LANES],
                              q[:, (2 * half + 1) * LANES:(2 * half + 2) * LANES]], axis=0)
        q2 = (q2 * LOG2E).astype(BF16)
        out = None
        for hh in (0, 1):
            var = 2 * half + hh
            s = _dot_nt(q2, ks[var])
            if mask is not None:
                s = jnp.where(mask, s, NEG_INF)
            m = jnp.max(s, axis=-1, keepdims=True)
            if sink_ref is not None:
                snk = LOG2E * jnp.where(first, sink_ref[2 * half, hh:hh + 1, 0:1],
                                        sink_ref[2 * half + 1, hh:hh + 1, 0:1])
                m = jnp.maximum(m, snk)
            acc = _dot(jnp.exp2((s - m).astype(BF16)), vs[var])
            ones_lane = HEAD_DIM if hh == 0 else 0
            den = acc[:, ones_lane:ones_lane + 1]
            if sink_ref is not None:
                den = den + jnp.exp2(snk - m)
            own = (lane < HEAD_DIM) if hh == 0 else (lane >= HEAD_DIM)
            o = jnp.where(own, acc, 0.0) * (1.0 / den)
            out = o if out is None else out + o
        o_ref[rows, (2 * half) * LANES:(2 * half + 1) * LANES] = out[:tq]
        o_ref[rows, (2 * half + 1) * LANES:(2 * half + 2) * LANES] = out[tq:]


def _full_attn_kernel(q_ref, k_ref, v_ref, *rest, has_sink):
    if has_sink:
        sink_ref, o_ref = rest
    else:
        sink_ref, (o_ref,) = None, rest
    for s in range(PROMPT_SEQS_PER_STEP):
        r = slice(s * SEQ, (s + 1) * SEQ)
        _attn_core(q_ref[r, :], k_ref[r, :], v_ref[r, :], sink_ref, o_ref, None, r)


PROMPT_SEQS_PER_STEP = 4


def _full_attention(p, q_cb0, k_cb0, v_cb0, n_groups, sink=None):
    rows = PROMPT_SEQS_PER_STEP * SEQ
    qw = 4 * LANES
    in_specs = [
        pl.BlockSpec((rows, qw), lambda b, g: (b, q_cb0 + g)),
        pl.BlockSpec((rows, LANES), lambda b, g: (b, k_cb0 + g)),
        pl.BlockSpec((rows, LANES), lambda b, g: (b, v_cb0 + g)),
    ]
    args = [p, p, p]
    if sink is not None:
        in_specs.append(pl.BlockSpec((None, 4, 2, LANES), lambda b, g: (g, 0, 0, 0)))
        args.append(sink)
    return pl.pallas_call(
        functools.partial(_full_attn_kernel, has_sink=sink is not None),
        grid=(T_PROMPT // rows, n_groups),
        in_specs=in_specs,
        out_specs=pl.BlockSpec((rows, qw), lambda b, g: (b, g)),
        out_shape=jax.ShapeDtypeStruct((T_PROMPT, n_groups * qw), F32),
        compiler_params=_cparams(("arbitrary", "arbitrary")),
        name="full_attention",
    )(*args)


def _latent_attn_kernel(q_ref, k_ref, v_ref, kx_ref, vx_ref, o_ref, kvar, vvar, *, tq, kb):
    lk = DEC_SEQ + PAST_LEN

    @pl.when(pl.program_id(2) == 0)
    def _():
        for src_k, src_v, r0, nrows in ((k_ref, v_ref, 0, DEC_SEQ), (kx_ref, vx_ref, DEC_SEQ, PAST_LEN)):
            for c in range(nrows // kb):
                src = pl.ds(c * kb, kb)
                dst = pl.ds(r0 + c * kb, kb)
                ks, vs = _kv_variants(src_k[src, :], src_v[src, :])
                for i in range(4):
                    kvar[i, dst, :] = ks[i]
                    vvar[i, dst, :] = vs[i]

    lane = lax.broadcasted_iota(jnp.int32, (2 * tq, LANES), 1)
    for half in (0, 1):
        q2 = jnp.concatenate([q_ref[:, (2 * half) * LANES:(2 * half + 1) * LANES],
                              q_ref[:, (2 * half + 1) * LANES:(2 * half + 2) * LANES]], axis=0)
        q2 = (q2 * LOG2E).astype(BF16)
        out = None
        for hh in (0, 1):
            var = 2 * half + hh
            m = jnp.full((2 * tq, 1), -jnp.inf, F32)
            acc = jnp.zeros((2 * tq, LANES), F32)
            for j in range(lk // kb):
                rows = pl.ds(j * kb, kb)
                s = _dot_nt(q2, kvar[var, rows, :])
                m_new = jnp.maximum(m, jnp.max(s, axis=-1, keepdims=True))
                e = jnp.exp2((s - m_new).astype(BF16))
                acc = jnp.exp2(m - m_new) * acc + _dot(e, vvar[var, rows, :])
                m = m_new
            own = (lane < HEAD_DIM) if hh == 0 else (lane >= HEAD_DIM)
            ones_lane = HEAD_DIM if hh == 0 else 0
            o = jnp.where(own, acc, 0.0) * (1.0 / acc[:, ones_lane:ones_lane + 1])
            out = o if out is None else out + o
        o_ref[:, (2 * half) * LANES:(2 * half + 1) * LANES] = out[:tq]
        o_ref[:, (2 * half + 1) * LANES:(2 * half + 2) * LANES] = out[tq:]


def _latent_attention(p, kx, vx):
    tq, kb = 512, 512
    nq = DEC_SEQ // tq
    ngr = ATT_KV_HEADS // 2
    kcb = ATT_QW // LANES
    rb_q = T_PROMPT // tq
    rb_k = T_PROMPT // DEC_SEQ
    lk = DEC_SEQ + PAST_LEN
    ctx = pl.BlockSpec((None, PAST_LEN, LANES), lambda b, g, i: (b, 0, g))
    return pl.pallas_call(
        functools.partial(_latent_attn_kernel, tq=tq, kb=kb),
        grid=(DEC_BATCH, ngr, nq),
        in_specs=[pl.BlockSpec((tq, 4 * LANES), lambda b, g, i: (rb_q + b * nq + i, g)),
                  pl.BlockSpec((DEC_SEQ, LANES), lambda b, g, i: (rb_k + b, kcb + g)),
                  pl.BlockSpec((DEC_SEQ, LANES), lambda b, g, i: (rb_k + b, kcb + ngr + g)),
                  ctx, ctx],
        out_specs=pl.BlockSpec((tq, 4 * LANES), lambda b, g, i: (b * nq + i, g)),
        out_shape=jax.ShapeDtypeStruct((T_SAMPLE, ATT_QW), F32),
        scratch_shapes=[pltpu.VMEM((4, lk, LANES), BF16), pltpu.VMEM((4, lk, LANES), BF16)],
        compiler_params=_cparams(("arbitrary", "arbitrary", "arbitrary")),
        name="latent_attention",
    )(p, p, p, kx, vx)


WIN_TQ = 2 * BLK
WIN_KBLKS = WIN_TQ // BLK + 2


def _window_attn_kernel(q_ref, *refs):
    k_refs = refs[:WIN_KBLKS]
    v_refs = refs[WIN_KBLKS:2 * WIN_KBLKS]
    kx_ref, vx_ref, sink_ref, o_ref = refs[2 * WIN_KBLKS:]
    n = pl.program_id(1)
    k = jnp.concatenate([r[...] for r in k_refs] + [kx_ref[...]], axis=0)
    v = jnp.concatenate([r[...] for r in v_refs] + [vx_ref[...]], axis=0)
    nloc = WIN_KBLKS * BLK
    shape = (WIN_TQ, nloc + PAST_LEN)
    i = lax.broadcasted_iota(jnp.int32, shape, 0)
    r = lax.broadcasted_iota(jnp.int32, shape, 1)
    kpos = n * WIN_TQ - BLK + r
    local = (jnp.abs(r - BLK - i) <= WINDOW) & (kpos >= 0) & (kpos < DEC_SEQ)
    mask = local | (r >= nloc)
    _attn_core(q_ref[...], k, v, sink_ref, o_ref, mask)


def _window_attention(p, kx, vx, sink):
    nb = DEC_SEQ // BLK
    nq = DEC_SEQ // WIN_TQ
    rb0 = T_PROMPT // BLK
    qcb = (4 * RET_QW) // (4 * LANES)
    kcb = (4 * RET_QW + WIN_QW) // LANES
    vcb = kcb + 1

    def kv(cb, off):
        return pl.BlockSpec((BLK, LANES),
                            lambda b, n: (rb0 + b * nb + jnp.clip(n * (WIN_TQ // BLK) + off, 0, nb - 1), cb))

    offs = range(-1, WIN_KBLKS - 1)
    ctx = pl.BlockSpec((None, PAST_LEN, LANES), lambda b, n: (b, 0, 0))
    qspec = pl.BlockSpec((WIN_TQ, 4 * LANES), lambda b, n: (T_PROMPT // WIN_TQ + b * nq + n, qcb))
    return pl.pallas_call(
        _window_attn_kernel,
        grid=(DEC_BATCH, nq),
        in_specs=[qspec] + [kv(kcb, o) for o in offs] + [kv(vcb, o) for o in offs] + [
            ctx, ctx, pl.BlockSpec((4, 2, LANES), lambda b, n: (0, 0, 0))],
        out_specs=pl.BlockSpec((WIN_TQ, 4 * LANES), lambda b, n: (b * nq + n, 0)),
        out_shape=jax.ShapeDtypeStruct((T_SAMPLE, WIN_QW), F32),
        compiler_params=_cparams(("arbitrary", "arbitrary")),
        name="window_attention",
    )(*([p] * (1 + 2 * WIN_KBLKS)), kx, vx, sink)


def _sink_table(sink, n_groups):
    s = sink.astype(F32).reshape(n_groups, 4, 2, 1)
    return jnp.broadcast_to(s, (n_groups, 4, 2, LANES))


def _cache_rows(cache):
    b, h, l, d = cache.shape
    return cache.astype(F32).transpose(0, 2, 1, 3).reshape(b, l, h * d)


def _outproj_kernel(*refs, n_in):
    a_refs = refs[:2 * n_in]
    w_refs = refs[2 * n_in:3 * n_in]
    xp_ref, xs_ref, mod_ref, g_ref, wr_ref, xn_ref, hfp_ref, hfs_ref, lg_ref = refs[3 * n_in:]
    i = pl.program_id(0)
    y = None
    for k, w_ref in enumerate(w_refs):
        a = _pick(a_refs[2 * k], a_refs[2 * k + 1])
        t = _dot(a.astype(BF16), w_ref[...])
        y = t if y is None else y + t
    xn = _pick(xp_ref, xs_ref) + mod_ref[2:3, :] * _rms(y, g_ref[1:2, :])
    hf = _rms(xn, g_ref[2:3, :]) * (1.0 + mod_ref[4:5, :]) + mod_ref[3:4, :]
    xn_ref[...] = xn
    lg_ref[...] = _dot_nt(wr_ref[...], hf.astype(BF16))

    @pl.when(i < NPT)
    def _():
        hfp_ref[...] = hf

    @pl.when(i >= NPT)
    def _():
        hfs_ref[...] = hf.reshape(TM, D_MODEL // LANES, LANES)


def _outproj(parts, w_parts, x, mod, layer, gains, w_router_t):
    n_in = len(parts)
    in_specs = []
    args = []
    for a_p, a_s in parts:
        in_specs += _pair_specs(a_p.shape[1])
        args += [a_p, a_s]
    in_specs += [pl.BlockSpec(w.shape, lambda i: (0, 0)) for w in w_parts]
    in_specs += _pair_specs(D_MODEL) + [
        pl.BlockSpec((None, None, 6, D_MODEL), lambda i: (layer, _mod_index(i), 0, 0)),
        pl.BlockSpec((4, D_MODEL), lambda i: (0, 0)),
        pl.BlockSpec((N_EXPERTS, D_MODEL), lambda i: (0, 0)),
    ]
    row = pl.BlockSpec((TM, D_MODEL), lambda i: (i, 0))
    hf_p, _ = _pair_specs(D_MODEL)
    hf_s = pl.BlockSpec((TM, D_MODEL // LANES, LANES), lambda i: (jnp.maximum(i - NPT, 0), 0, 0))
    return pl.pallas_call(
        functools.partial(_outproj_kernel, n_in=n_in),
        grid=(T_ALL // TM,),
        in_specs=in_specs,
        out_specs=[row, hf_p, hf_s, pl.BlockSpec((N_EXPERTS, TM), lambda i: (0, i))],
        out_shape=[jax.ShapeDtypeStruct((T_ALL, D_MODEL), F32),
                   jax.ShapeDtypeStruct((T_PROMPT, D_MODEL), F32),
                   jax.ShapeDtypeStruct((T_SAMPLE, D_MODEL // LANES, LANES), F32),
                   jax.ShapeDtypeStruct((N_EXPERTS, T_ALL), F32)],
        compiler_params=_cparams(("arbitrary",)),
        name="outproj",
    )(*args, *w_parts, x[0], x[1], mod, gains, w_router_t)


TILE_SUB = D_MODEL // LANES


def _gather_kernel(idx_ref, h_ref, o_ref, buf, *, eg, cap):
    b = pl.program_id(0)
    g = pl.program_id(1)
    for e in range(eg):
        base = (b * N_EXPERTS + g * eg + e) * cap

        def body(c, carry):
            buf[c] = h_ref[idx_ref[base + c]]
            return carry

        lax.fori_loop(0, cap, body, 0, unroll=8)
        o_ref[e] = buf[...].reshape(cap, D_MODEL).astype(BF16)


def _gather(h3, idx, n_seq, seq_len, cap, eg):
    return pl.pallas_call(
        functools.partial(_gather_kernel, eg=eg, cap=cap),
        grid_spec=pltpu.PrefetchScalarGridSpec(
            num_scalar_prefetch=1,
            grid=(n_seq, N_EXPERTS // eg),
            in_specs=[pl.BlockSpec((seq_len, TILE_SUB, LANES), lambda b, g, idx: (b, 0, 0))],
            out_specs=pl.BlockSpec((eg, cap, D_MODEL), lambda b, g, idx: (g, b, 0)),
            scratch_shapes=[pltpu.VMEM((cap, TILE_SUB, LANES), F32)],
        ),
        out_shape=jax.ShapeDtypeStruct((N_EXPERTS, n_seq * cap, D_MODEL), BF16),
        compiler_params=_cparams(("arbitrary", "arbitrary")),
        name="moe_gather",
    )(idx, h3)


def _moe_kernel(xp_ref, xs_ref, wg_ref, wu_ref, wd_ref, gwp_ref, gws_ref, yp_ref, ys_ref, hid_scr, wd_scr):
    f = pl.program_id(1)
    wg = wg_ref[...].astype(BF16)
    wu = wu_ref[...].astype(BF16)
    cols = pl.ds(pl.multiple_of(f * TF, TF), TF)
    wd_scr[cols, :] = wd_ref[...].astype(BF16)
    for h, x_ref in enumerate((xp_ref, xs_ref)):
        x = x_ref[...]
        hid_scr[h, :, cols] = (_silu(_dot(x, wg)) * _dot(x, wu)).astype(BF16)

    @pl.when(f == pl.num_programs(1) - 1)
    def _():
        for h, (gw_ref, y_ref) in enumerate(((gwp_ref, yp_ref), (gws_ref, ys_ref))):
            for r in range(0, hid_scr.shape[1], MOE_ROW_TILE):
                rows = slice(r, r + MOE_ROW_TILE)
                y_ref[rows, :] = _dot(hid_scr[h, rows, :], wd_scr[...]) * gw_ref[rows, :]


MOE_ROW_TILE = 512


def _moe_ffn(xg_p, xg_s, w_gate, w_up, w_down, layer, gw_p, gw_s):
    assert ROWS_P == ROWS_S
    nf = EXPERT_FF // TF
    xspec_p = pl.BlockSpec((None, ROWS_P, D_MODEL), lambda e, f: (e, 0, 0))
    xspec_s = pl.BlockSpec((None, ROWS_S, D_MODEL), lambda e, f: (e, 0, 0))
    return pl.pallas_call(
        _moe_kernel,
        grid=(N_EXPERTS, nf),
        in_specs=[xspec_p, xspec_s,
                  pl.BlockSpec((None, None, D_MODEL, TF), lambda e, f: (layer, e, 0, f)),
                  pl.BlockSpec((None, None, D_MODEL, TF), lambda e, f: (layer, e, 0, f)),
                  pl.BlockSpec((None, None, TF, D_MODEL), lambda e, f: (layer, e, f, 0)),
                  pl.BlockSpec((None, ROWS_P, 1), lambda e, f: (e, 0, 0)),
                  pl.BlockSpec((None, ROWS_S, 1), lambda e, f: (e, 0, 0))],
        out_specs=[xspec_p, xspec_s],
        out_shape=[jax.ShapeDtypeStruct((N_EXPERTS, ROWS_P, D_MODEL), F32),
                   jax.ShapeDtypeStruct((N_EXPERTS, ROWS_S, D_MODEL), F32)],
        scratch_shapes=[pltpu.VMEM((2, ROWS_P, EXPERT_FF), BF16),
                        pltpu.VMEM((EXPERT_FF, D_MODEL), BF16)],
        compiler_params=_cparams(("arbitrary", "arbitrary")),
        name="moe_ffn",
    )(xg_p, xg_s, w_gate, w_up, w_down, gw_p, gw_s)


COMBINE_GROUP = 8


def _combine_kernel(idx_ref, y_ref, o_ref, y3, *, eg, cap):
    b = pl.program_id(0)
    g = pl.program_id(1)

    @pl.when(g == 0)
    def _():
        o_ref[...] = jnp.zeros_like(o_ref)

    for e in range(eg):
        base = (b * N_EXPERTS + g * eg + e) * cap
        y3[...] = y_ref[e].reshape(cap, TILE_SUB, LANES)

        def body(c, carry):
            c0 = c * COMBINE_GROUP
            toks = [idx_ref[base + c0 + k] for k in range(COMBINE_GROUP)]
            rows = [o_ref[t] + y3[c0 + k] for k, t in enumerate(toks)]
            for t, row in zip(toks, rows):
                o_ref[t] = row
            return carry

        lax.fori_loop(0, cap // COMBINE_GROUP, body, 0)


def _combine(y, idx, n_seq, seq_len, cap, eg):
    return pl.pallas_call(
        functools.partial(_combine_kernel, eg=eg, cap=cap),
        grid_spec=pltpu.PrefetchScalarGridSpec(
            num_scalar_prefetch=1,
            grid=(n_seq, N_EXPERTS // eg),
            in_specs=[pl.BlockSpec((eg, cap, D_MODEL), lambda b, g, idx: (g, b, 0))],
            out_specs=pl.BlockSpec((seq_len, TILE_SUB, LANES), lambda b, g, idx: (b, 0, 0)),
            scratch_shapes=[pltpu.VMEM((cap, TILE_SUB, LANES), F32)],
        ),
        out_shape=jax.ShapeDtypeStruct((n_seq * seq_len, TILE_SUB, LANES), F32),
        compiler_params=_cparams(("arbitrary", "arbitrary")),
        name="moe_combine",
    )(idx, y)


def _onehot_t(idx_row, n_tok):
    tok = lax.broadcasted_iota(jnp.int32, (n_tok, idx_row.shape[-1]), 0)
    return jnp.where(tok == idx_row, 1.0, 0.0)


def _gather_mm_kernel(idx_ref, h_ref, o_ref, *, cap, seq_len):
    for s in range(PROMPT_SEQS_PER_STEP):
        sel = _onehot_t(idx_ref[s], seq_len).T.astype(BF16)
        h = h_ref[s * seq_len:(s + 1) * seq_len, :].astype(BF16)
        rows = _dot(sel, h).astype(BF16)
        for e in range(N_EXPERTS):
            o_ref[e, s * cap:(s + 1) * cap, :] = rows[e * cap:(e + 1) * cap]


def _gather_mm(h, idx, n_seq, seq_len, cap):
    ns = PROMPT_SEQS_PER_STEP
    return pl.pallas_call(
        functools.partial(_gather_mm_kernel, cap=cap, seq_len=seq_len),
        grid=(n_seq // ns,),
        in_specs=[pl.BlockSpec((ns, 1, N_EXPERTS * cap), lambda b: (b, 0, 0)),
                  pl.BlockSpec((ns * seq_len, D_MODEL), lambda b: (b, 0))],
        out_specs=pl.BlockSpec((N_EXPERTS, ns * cap, D_MODEL), lambda b: (0, b, 0)),
        out_shape=jax.ShapeDtypeStruct((N_EXPERTS, n_seq * cap, D_MODEL), BF16),
        compiler_params=_cparams(("arbitrary",)),
        name="moe_gather_mm",
    )(idx, h)


def _combine_mm_kernel(idx_ref, y_ref, o_ref, *, cap, seq_len):
    for s in range(PROMPT_SEQS_PER_STEP):
        sel = _onehot_t(idx_ref[s], seq_len).astype(BF16)
        y = jnp.concatenate([y_ref[e, s * cap:(s + 1) * cap, :] for e in range(N_EXPERTS)], axis=0)
        hi = y.astype(BF16)
        lo = (y - hi.astype(F32)).astype(BF16)
        o_ref[s * seq_len:(s + 1) * seq_len, :] = _dot(sel, hi) + _dot(sel, lo)


def _combine_mm(y, idx, n_seq, seq_len, cap):
    ns = PROMPT_SEQS_PER_STEP
    return pl.pallas_call(
        functools.partial(_combine_mm_kernel, cap=cap, seq_len=seq_len),
        grid=(n_seq // ns,),
        in_specs=[pl.BlockSpec((ns, 1, N_EXPERTS * cap), lambda b: (b, 0, 0)),
                  pl.BlockSpec((N_EXPERTS, ns * cap, D_MODEL), lambda b: (0, b, 0))],
        out_specs=pl.BlockSpec((ns * seq_len, D_MODEL), lambda b: (b, 0)),
        out_shape=jax.ShapeDtypeStruct((n_seq * seq_len, D_MODEL), F32),
        compiler_params=_cparams(("arbitrary",)),
        name="moe_combine_mm",
    )(idx, y)


def _postnorm_kernel(x_ref, yp_ref, ys_ref, mod_ref, g_ref, op_ref, os_ref):
    i = pl.program_id(0)
    y = jnp.where(i < NPT, yp_ref[...], ys_ref[...].reshape(TM, D_MODEL))
    o = x_ref[...] + mod_ref[5:6, :] * _rms(y, g_ref[3:4, :])

    @pl.when(i < NPT)
    def _():
        op_ref[...] = o

    @pl.when(i >= NPT)
    def _():
        os_ref[...] = o


def _postnorm(x, y_p, y_s, mod, layer, gains):
    return pl.pallas_call(
        _postnorm_kernel,
        grid=(T_ALL // TM,),
        in_specs=[pl.BlockSpec((TM, D_MODEL), lambda i: (i, 0)), _pair_specs(D_MODEL)[0],
                  pl.BlockSpec((TM, TILE_SUB, LANES), lambda i: (jnp.maximum(i - NPT, 0), 0, 0)),
                  pl.BlockSpec((None, None, 6, D_MODEL), lambda i: (layer, _mod_index(i), 0, 0)),
            pl.BlockSpec((4, D_MODEL), lambda i: (0, 0))],
        out_specs=_pair_specs(D_MODEL),
        out_shape=[jax.ShapeDtypeStruct((T_PROMPT, D_MODEL), F32),
                   jax.ShapeDtypeStruct((T_SAMPLE, D_MODEL), F32)],
        compiler_params=_cparams(("arbitrary",)),
        name="postnorm",
    )(x, y_p, y_s, mod, gains)


def _excl_prefix(mask):
    r, n = mask.shape
    i0 = lax.broadcasted_iota(jnp.int32, (LANES, LANES), 0)
    i1 = lax.broadcasted_iota(jnp.int32, (LANES, LANES), 1)
    upper = jnp.where(i0 < i1, 1.0, 0.0).astype(BF16)
    ones = jnp.where(mask, 1.0, 0.0)
    carry = jnp.zeros((r, 1), F32)
    out = []
    for c in range(n // LANES):
        ch = ones[:, c * LANES:(c + 1) * LANES]
        out.append(_dot(ch.astype(BF16), upper) + carry)
        carry = carry + jnp.sum(ch, axis=1, keepdims=True)
    return jnp.concatenate(out, axis=1)


COMPACT_SLOTS = 256


def _compact_mxu(pos, aff, idx_ref, gw_ref, n, cap):
    sub = lax.broadcasted_iota(jnp.int32, (8, n), 0)
    tok = lax.broadcasted_iota(jnp.int32, (8, n), 1)
    base = jnp.where(sub == 0, tok >> 6, jnp.where(sub == 1, tok & 63, 0)).astype(F32)
    pad = jnp.zeros((LANES - 8, n), F32)
    slot0 = lax.broadcasted_iota(jnp.int32, (COMPACT_SLOTS, n), 0).astype(F32)
    for e in range(N_EXPERTS):
        a = aff[e:e + 1, :]
        a_hi = a.astype(BF16).astype(F32)
        rest = a - a_hi
        a_mid = rest.astype(BF16).astype(F32)
        a_lo = rest - a_mid
        pay = base + jnp.where(sub == 2, a_hi, 0.0) + jnp.where(sub == 3, a_mid, 0.0) + jnp.where(sub == 4, a_lo, 0.0)
        pay = jnp.concatenate([pay, pad], axis=0).astype(BF16)
        pos_e = pos[e:e + 1, :]
        for c in range(cap // COMPACT_SLOTS):
            onehot = jnp.where(pos_e == slot0 + float(c * COMPACT_SLOTS), 1.0, 0.0).astype(BF16)
            got = _dot_nt(onehot, pay).T
            cols = slice(c * COMPACT_SLOTS, (c + 1) * COMPACT_SLOTS)
            idx_ref[0, e:e + 1, cols] = (got[0:1] * 64.0 + got[1:2]).astype(jnp.int32)
            gw_ref[0, e:e + 1, cols] = got[2:3] + got[3:4] + got[4:5]


def _route_kernel(lg_ref, idx_ref, gw_ref, *, nb, n, cap):
    affs = []
    for s in range(nb):
        l = lg_ref[:, s * n:(s + 1) * n]
        e = jnp.exp(l - jnp.max(l, axis=0, keepdims=True))
        affs.append(e / jnp.sum(e, axis=0, keepdims=True))
    aff = affs[0] if nb == 1 else jnp.concatenate(affs, axis=0)
    rows = nb * N_EXPERTS
    bits = jnp.zeros((rows, 1), jnp.int32)
    for bit in range(30, -1, -1):
        cand = bits | (1 << bit)
        cnt = jnp.sum(jnp.where(aff >= pltpu.bitcast(cand, F32), 1.0, 0.0), axis=1, keepdims=True)
        bits = jnp.where(cnt >= cap, cand, bits)
    thr = pltpu.bitcast(bits, F32)
    gt = aff > thr
    eq = aff == thr
    need = cap - jnp.sum(jnp.where(gt, 1.0, 0.0), axis=1, keepdims=True)
    sel = gt | (eq & (_excl_prefix(eq) < need))
    pos = jnp.where(sel, _excl_prefix(sel), -1.0)
    if cap > LANES:
        _compact_mxu(pos, aff, idx_ref, gw_ref, n, cap)
        return
    tok = lax.broadcasted_iota(jnp.int32, (rows, n), 1).astype(F32)
    capw = max(cap, LANES)
    slot_lane = lax.broadcasted_iota(jnp.int32, (rows, capw), 1)

    def slot(s, carry):
        idx_acc, gw_acc = carry
        hit = pos == lax.convert_element_type(s, F32)
        ic = jnp.sum(jnp.where(hit, tok, 0.0), axis=1, keepdims=True)
        gc = jnp.sum(jnp.where(hit, aff, 0.0), axis=1, keepdims=True)
        here = slot_lane == s
        return jnp.where(here, ic, idx_acc), jnp.where(here, gc, gw_acc)

    zero = jnp.zeros((rows, capw), F32)
    idx_acc, gw_acc = lax.fori_loop(0, cap, slot, (zero, zero), unroll=8)
    for s in range(nb):
        r = slice(s * N_EXPERTS, (s + 1) * N_EXPERTS)
        idx_ref[s] = idx_acc[r, :cap].astype(jnp.int32)
        gw_ref[s] = gw_acc[r, :cap]


def _route(logits_t, n_seq, seq_len, cap, col0, nb):
    cb0 = col0 // (nb * seq_len)
    out = pl.BlockSpec((nb, N_EXPERTS, cap), lambda b: (b, 0, 0))
    return pl.pallas_call(
        functools.partial(_route_kernel, nb=nb, n=seq_len, cap=cap),
        grid=(n_seq // nb,),
        in_specs=[pl.BlockSpec((N_EXPERTS, nb * seq_len), lambda b: (0, cb0 + b))],
        out_specs=[out, out],
        out_shape=[jax.ShapeDtypeStruct((n_seq, N_EXPERTS, cap), jnp.int32),
                   jax.ShapeDtypeStruct((n_seq, N_EXPERTS, cap), F32)],
        compiler_params=_cparams(("arbitrary",)),
        name="route",
    )(logits_t)


def _expert_ffn(hf_p, hf_s, logits_t, w_gate, w_up, w_down, layer):
    idx_p, gw_p = _route(logits_t, BATCH, SEQ, CAP_P, 0, 8)
    idx_s, gw_s = _route(logits_t, DEC_BATCH, DEC_SEQ, CAP_S, T_PROMPT, 1)
    idx_p = idx_p.reshape(BATCH, 1, N_EXPERTS * CAP_P)
    idx_s = idx_s.reshape(DEC_BATCH * N_EXPERTS * CAP_S)
    xg_p = _gather_mm(hf_p, idx_p, BATCH, SEQ, CAP_P)
    xg_s = _gather(hf_s, idx_s, DEC_BATCH, DEC_SEQ, CAP_S, 4)
    gw_p = gw_p.transpose(1, 0, 2).reshape(N_EXPERTS, ROWS_P, 1)
    gw_s = gw_s.transpose(1, 0, 2).reshape(N_EXPERTS, ROWS_S, 1)
    y_p, y_s = _moe_ffn(xg_p, xg_s, w_gate, w_up, w_down, layer, gw_p, gw_s)
    o_p = _combine_mm(y_p, idx_p, BATCH, SEQ, CAP_P)
    o_s = _combine(y_s, idx_s, DEC_BATCH, DEC_SEQ, CAP_S, 2)
    return o_p, o_s


def _rope_tables():
    t = np.arange(DEC_SEQ)
    row = (t // GRID_W).astype(np.float32)
    col = (t % GRID_W).astype(np.float32)
    inv = (ROPE_THETA ** (-np.arange(0, AXIS_DIM, 2, dtype=np.float32) / AXIS_DIM)).astype(np.float32)
    ar = row[:, None] * inv[None]
    ac = col[:, None] * inv[None]
    ang = np.concatenate([ar, ar, ac, ac], axis=-1)
    ang = np.concatenate([ang, ang], axis=-1)
    cos = np.concatenate([np.ones((TM, LANES), np.float32), np.cos(ang)], axis=0)
    sin = np.concatenate([np.zeros((TM, LANES), np.float32), np.sin(ang)], axis=0)
    return jnp.asarray(cos, F32), jnp.asarray(sin, F32)


def kernel(x_prompt, x_sample, state_ret_fwd, state_ret_bwd, cache_win_k, cache_win_v, cache_attn_k, cache_attn_v, c, c_ctx, w_ada, b_ada, norm_gains, w_in_even, w_out_even, ret_decay_fwd, ret_decay_bwd, win_sink, w_in_odd, w_out_odd, q_norm, k_norm, w_router, w_gate, w_up, w_down):
    x = (x_prompt.reshape(T_PROMPT, D_MODEL), x_sample.reshape(T_SAMPLE, D_MODEL))
    cond =jnp.concatenate([c_ctx[None, :], c, jnp.zeros((8 - 1 - DEC_BATCH, D_MODEL), F32)], axis=0)
    mod = _modulation(cond, w_ada, b_ada).reshape(DEPTH, 8, 6, D_MODEL)
    cos_t, sin_t = _rope_tables()
    outs = {}
    for layer in range(DEPTH):
        j = layer // 2
        gains = norm_gains[layer]
        if layer % 2 == 0:
            p, outs["win_k"], outs["win_v"] = _inproj(x, mod, layer, gains[0], w_in_even[j].astype(BF16),
                                                      cos_t, sin_t)
            tabs = _retention_tables(ret_decay_fwd[j], ret_decay_bwd[j])
            zero = jnp.zeros((BATCH, RET_HEADS // 2, LANES, LANES), F32)
            ret_p, sf, sb = _retention(p, tabs, zero, zero, BATCH, SEQ, 0, 4)
            ret_s, _, _ = _retention(p, tabs, _pair_states(state_ret_fwd[:, j]),
                                     _pair_states(state_ret_bwd[:, j]), DEC_BATCH, DEC_SEQ, T_PROMPT, 2)
            sink = _sink_table(win_sink[j], 1)
            qcb = (4 * RET_QW) // (4 * LANES)
            kcb = (4 * RET_QW + WIN_QW) // LANES
            win_p = _full_attention(p, qcb, kcb, kcb + 1, 1, sink=sink)
            win_s = _window_attention(p, _cache_rows(cache_win_k[:, j]), _cache_rows(cache_win_v[:, j]),
                                      sink[0])
            w_out = w_out_even[j].astype(BF16)
            parts = [(ret_p, ret_s), (win_p, win_s)]
            w_parts = [w_out[:RET_QW], w_out[RET_QW:]]
            outs["ret_f"] = _unpair_states(sf)[:, None]
            outs["ret_b"] = _unpair_states(sb)[:, None]
        else:
            p, outs["att_k"], outs["att_v"] = _inproj(x, mod, layer, gains[0], w_in_odd[j].astype(BF16),
                                                      cos_t, sin_t, qn=q_norm[j], kn=k_norm[j])
            ngr = ATT_KV_HEADS // 2
            kcb = ATT_QW // LANES
            att_p = _full_attention(p, 0, kcb, kcb + ngr, ngr)
            att_s = _latent_attention(p, _cache_rows(cache_attn_k[:, j]), _cache_rows(cache_attn_v[:, j]))
            parts = [(att_p, att_s)]
            w_parts = [w_out_odd[j].astype(BF16)]
        xn, hf_p, hf_s, logits_t = _outproj(parts, w_parts, x, mod, layer, gains,
                                            w_router[layer].T.astype(BF16))
        o_p, o_s = _expert_ffn(hf_p, hf_s, logits_t, w_gate, w_up, w_down, layer)
        x = _postnorm(xn, o_p, o_s, mod, layer, gains)
    y_prompt = x[0].reshape(BATCH, SEQ, D_MODEL)
    y_sample = x[1].reshape(DEC_BATCH, DEC_SEQ, D_MODEL)
    return (y_prompt, y_sample, outs["ret_f"], outs["ret_b"], outs["win_k"], outs["win_v"],
            outs["att_k"], outs["att_v"])
```

```python
import functools

import jax
import jax.numpy as jnp
import numpy as np
from jax import lax
from jax.experimental import pallas as pl
from jax.experimental.pallas import tpu as pltpu

D_MODEL = 1024
BATCH = 32
SEQ = 256
DEPTH = 2
DEC_BATCH = 2
DEC_SEQ = 4096
PAST_LEN = 512
GRID_W = 64
HEAD_DIM = 64
AXIS_DIM = HEAD_DIM // 2
ROPE_THETA = 10000.0
BLK = 128
WINDOW = 128
RET_HEADS = D_MODEL // 128
WIN_HEADS = D_MODEL // 128
WIN_KV_HEADS = WIN_HEADS // 4
ATT_HEADS = D_MODEL // HEAD_DIM
ATT_KV_HEADS = ATT_HEADS // 4
RET_QW = RET_HEADS * HEAD_DIM
WIN_QW = WIN_HEADS * HEAD_DIM
WIN_KW = WIN_KV_HEADS * HEAD_DIM
EVEN_IN = 4 * RET_QW + WIN_QW + 2 * WIN_KW
ATT_QW = ATT_HEADS * HEAD_DIM
ATT_KW = ATT_KV_HEADS * HEAD_DIM
ODD_IN = ATT_QW + 2 * ATT_KW
N_EXPERTS = 16
CAPACITY_FACTOR = 2
EXPERT_FF = ((8 * D_MODEL // 3 + 127) // 128) * 128
EPS = 1e-6
NEG_INF = -1e30
F32 = jnp.float32
BF16 = jnp.bfloat16

LANES = 128
T_PROMPT = BATCH * SEQ
T_SAMPLE = DEC_BATCH * DEC_SEQ
T_ALL = T_PROMPT + T_SAMPLE
TM = 512
CAP_P = CAPACITY_FACTOR * SEQ // N_EXPERTS
CAP_S = CAPACITY_FACTOR * DEC_SEQ // N_EXPERTS
ROWS_P = BATCH * CAP_P
ROWS_S = DEC_BATCH * CAP_S
TF = 256
VMEM_LIMIT = 56 * 1024 * 1024
QK_SCALE = HEAD_DIM ** -0.5


def _cparams(sem):
    return pltpu.CompilerParams(dimension_semantics=sem, vmem_limit_bytes=VMEM_LIMIT)


def _silu(x):
    return x * (1.0 / (1.0 + jnp.exp(-x)))


def _dot(a, b):
    return jnp.dot(a, b, preferred_element_type=F32)


def _dot_nt(a, b):
    return lax.dot_general(a, b, (((1,), (1,)), ((), ())), preferred_element_type=F32)


def _rms(x, g):
    return x * lax.rsqrt(jnp.mean(x * x, axis=-1, keepdims=True) + EPS) * g


def _pair_block_diag(shape):
    r = lax.broadcasted_iota(jnp.int32, shape, 0)
    c = lax.broadcasted_iota(jnp.int32, shape, 1)
    return (r // HEAD_DIM) == (c // HEAD_DIM)


def _head_sum(x, bd):
    hi = x.astype(BF16)
    lo = (x - hi.astype(F32)).astype(BF16)
    return _dot(hi, bd) + _dot(lo, bd)


def _rope(xb, cos, sin):
    lane = lax.broadcasted_iota(jnp.int32, xb.shape, 1)
    half = AXIS_DIM // 2
    rot = jnp.where((lane % AXIS_DIM) < half,
                    -pltpu.roll(xb, LANES - half, 1), pltpu.roll(xb, half, 1))
    return xb * cos + rot * sin


def _mod_kernel(c_ref, w_ref, b_ref, o_ref):
    o_ref[...] = _dot(_silu(c_ref[...]), w_ref[...]) + b_ref[...]


def _modulation(cond, w_ada, b_ada):
    tn = 1536
    n = 6 * D_MODEL
    return pl.pallas_call(
        _mod_kernel,
        grid=(DEPTH, n // tn),
        in_specs=[
            pl.BlockSpec((8, D_MODEL), lambda l, j: (0, 0)),
            pl.BlockSpec((None, D_MODEL, tn), lambda l, j: (l, 0, j)),
            pl.BlockSpec((None, 1, tn), lambda l, j: (l, 0, j)),
        ],
        out_specs=pl.BlockSpec((None, 8, tn), lambda l, j: (l, 0, j)),
        out_shape=jax.ShapeDtypeStruct((DEPTH, 8, n), F32),
        compiler_params=_cparams(("arbitrary", "arbitrary")),
        name="modulation",
    )(cond, w_ada, b_ada.reshape(DEPTH, 1, n))


def _mod_index(i):
    npt = T_PROMPT // TM
    return jnp.where(i < npt, 0, 1 + (i - npt) // (DEC_SEQ // TM))


def _rope_index(i):
    npt = T_PROMPT // TM
    return jnp.where(i < npt, 0, 1 + (i - npt) % (DEC_SEQ // TM))


NPT = T_PROMPT // TM


def _pair_specs(width):
    return [pl.BlockSpec((TM, width), lambda i: (jnp.minimum(i, NPT - 1), 0)),
            pl.BlockSpec((TM, width), lambda i: (jnp.maximum(i - NPT, 0), 0))]


def _pick(p_ref, s_ref):
    return jnp.where(pl.program_id(0) < NPT, p_ref[...], s_ref[...])


def _inproj_kernel(xp_ref, xs_ref, mod_ref, g_ref, w_ref, cos_ref, sin_ref, *rest, even):
    if even:
        o_ref, ko_ref, vo_ref = rest
    else:
        qn_ref, kn_ref, o_ref, ko_ref, vo_ref = rest
    h = _rms(_pick(xp_ref, xs_ref), g_ref[...]) * (1.0 + mod_ref[1:2, :]) + mod_ref[0:1, :]
    h16 = h.astype(BF16)
    cos = cos_ref[...]
    sin = sin_ref[...]
    nblk = w_ref.shape[1] // LANES
    if even:
        nq = RET_QW // LANES
        rope_blocks = set(range(0, 2 * nq)) | set(range(4 * nq, 5 * nq + 1))
        scaled = set(range(nq, 2 * nq)) | set(range(4 * nq, 5 * nq))
        normed = {}
        cache_out = {5 * nq: (ko_ref, 0), 5 * nq + 1: (vo_ref, 0)}
    else:
        nq = ATT_QW // LANES
        nk = ATT_KW // LANES
        rope_blocks = set(range(0, nq + nk))
        scaled = set(range(0, nq))
        normed = {b: (qn_ref if b < nq else kn_ref) for b in range(nq + nk)}
        bd = _pair_block_diag((LANES, LANES)).astype(BF16)
        cache_out = {nq + b: (ko_ref, 2 * b) for b in range(nk)}
        cache_out.update({nq + nk + b: (vo_ref, 2 * b) for b in range(nk)})
    p = _dot(h16, w_ref[...])
    for b in range(nblk):
        blk = p[:, b * LANES:(b + 1) * LANES]
        if b in normed:
            ms = _head_sum(blk * blk, bd) * (1.0 / HEAD_DIM)
            blk = blk * lax.rsqrt(ms + EPS) * normed[b][...]
        if b in rope_blocks:
            blk = _rope(blk, cos, sin)
        if b in scaled:
            blk = blk * QK_SCALE
        o_ref[:, b * LANES:(b + 1) * LANES] = blk
        if b in cache_out:
            c_ref, head0 = cache_out[b]

            @pl.when(pl.program_id(0) < NPT)
            def _():
                for s in range(TM // SEQ):
                    for hh in range(2):
                        c_ref[s, 0, head0 + hh] = blk[s * SEQ:(s + 1) * SEQ, hh * HEAD_DIM:(hh + 1) * HEAD_DIM]


def _inproj(x, mod, layer, gain, w, cos_t, sin_t, qn=None, kn=None):
    even = qn is None
    n = w.shape[1]
    in_specs = _pair_specs(D_MODEL) + [
        pl.BlockSpec((None, None, 6, D_MODEL), lambda i: (layer, _mod_index(i), 0, 0)),
        pl.BlockSpec((1, D_MODEL), lambda i: (0, 0)),
        pl.BlockSpec((D_MODEL, n), lambda i: (0, 0)),
        pl.BlockSpec((TM, LANES), lambda i: (_rope_index(i), 0)),
        pl.BlockSpec((TM, LANES), lambda i: (_rope_index(i), 0)),
    ]
    args = [x[0], x[1], mod, gain.reshape(1, D_MODEL), w, cos_t, sin_t]
    if not even:
        in_specs += [pl.BlockSpec((1, LANES), lambda i: (0, 0))] * 2
        args += [jnp.tile(qn, 2).reshape(1, LANES), jnp.tile(kn, 2).reshape(1, LANES)]
    nkv = WIN_KV_HEADS if even else ATT_KV_HEADS
    cache = pl.BlockSpec((TM // SEQ, 1, nkv, SEQ, HEAD_DIM), lambda i: (jnp.minimum(i, NPT - 1), 0, 0, 0, 0))
    cache_shape = jax.ShapeDtypeStruct((BATCH, 1, nkv, SEQ, HEAD_DIM), F32)
    return pl.pallas_call(
        functools.partial(_inproj_kernel, even=even),
        grid=(T_ALL // TM,),
        in_specs=in_specs,
        out_specs=[pl.BlockSpec((TM, n), lambda i: (i, 0)), cache, cache],
        out_shape=[jax.ShapeDtypeStruct((T_ALL, n), F32), cache_shape, cache_shape],
        compiler_params=_cparams(("arbitrary",)),
        name="inproj_even" if even else "inproj_odd",
    )(*args)


RET_READOUT_ROWS = 512
RET_CHUNK = 2 * BLK


def _ret_kernel(q_ref, k_ref, v_ref, g_ref, dm_ref, qd_ref, kd_ref, cm_ref, *rest, nc, npb, zero_init):
    if zero_init:
        o_ref, sf_ref, sb_ref, of_scr, ob_scr = rest
        sf_ref[...] = jnp.zeros_like(sf_ref)
        sb_ref[...] = jnp.zeros_like(sb_ref)
    else:
        s0f_ref, s0b_ref, o_ref, sf_ref, sb_ref, of_scr, ob_scr = rest
        sf_ref[...] = s0f_ref[...]
        sb_ref[...] = s0b_ref[...]
    rc = RET_CHUNK
    lane = lax.broadcasted_iota(jnp.int32, (rc, LANES), 1)
    lo = lane < HEAD_DIM
    bd = _pair_block_diag((LANES, LANES))
    bd16 = bd.astype(BF16)

    def chunk(c, j, d, s_ref, dst):
        r = pl.ds(pl.multiple_of(c * rc, rc), rc)
        cols = slice(j * LANES, (j + 1) * LANES)
        qc = q_ref[r, cols]
        kc = k_ref[r, cols]
        vc = v_ref[r, cols]
        s = s_ref[j]
        a = _dot_nt(jnp.concatenate([jnp.where(lo, qc, 0.0), jnp.where(lo, 0.0, qc)], axis=0), kc)
        lhs = jnp.concatenate([a[:rc] * dm_ref[j, d, 0], a[rc:] * dm_ref[j, d, 1], qc * qd_ref[j, d]], axis=1)
        rhs = jnp.concatenate([jnp.where(lo, vc, 0.0), jnp.where(lo, 0.0, vc), s], axis=0)
        dst[r, cols] = _dot(lhs, rhs)
        kv = _dot((kc * kd_ref[j, d]).T, vc)
        s_ref[j] = s * cm_ref[j, d] + jnp.where(bd, kv, 0.0)

    def body(c, carry):
        for j in range(npb):
            chunk(c, j, 0, sf_ref, of_scr)
            chunk(nc - 1 - c, j, 1, sb_ref, ob_scr)
        return carry

    lax.fori_loop(0, nc, body, 0)

    rt = min(nc * rc, RET_READOUT_ROWS)

    def readout(c, carry):
        r = pl.ds(pl.multiple_of(c * rt, rt), rt)
        for j in range(npb):
            cols = slice(j * LANES, (j + 1) * LANES)
            o = of_scr[r, cols] + ob_scr[r, cols]
            mu = _head_sum(o, bd16) * (1.0 / HEAD_DIM)
            dlt = o - mu
            var = _head_sum(dlt * dlt, bd16) * (1.0 / HEAD_DIM)
            o_ref[r, cols] = (dlt * lax.rsqrt(var + EPS) * _silu(g_ref[r, cols])).astype(o_ref.dtype)
        return carry

    lax.fori_loop(0, nc * rc // rt, readout, 0)


def _retention(p, tabs, s0f, s0b, n_seq, seq_len, row0, npb):
    npair = RET_HEADS // 2
    ng = npair // npb
    rb0 = row0 // seq_len
    dm, qd, kd, cm = tabs
    w = npb * LANES

    def col(c0):
        return pl.BlockSpec((seq_len, w), lambda b, j: (rb0 + b, c0 + j))

    rc = RET_CHUNK
    tab_rows = pl.BlockSpec((npb, 2, rc, LANES), lambda b, j: (j, 0, 0, 0))
    tab_state = pl.BlockSpec((npb, 2, LANES, LANES), lambda b, j: (j, 0, 0, 0))
    st = pl.BlockSpec((None, npb, LANES, LANES), lambda b, j: (b, j, 0, 0))
    zero_init = s0f is None
    states = [] if zero_init else [s0f, s0b]
    return pl.pallas_call(
        functools.partial(_ret_kernel, nc=seq_len // rc, npb=npb, zero_init=zero_init),
        grid=(n_seq, ng),
        in_specs=[col(0), col(ng), col(2 * ng), col(3 * ng),
                  pl.BlockSpec((npb, 2, 2, rc, rc), lambda b, j: (j, 0, 0, 0, 0)),
                  tab_rows, tab_rows, tab_state] + [st] * len(states),
        out_specs=[pl.BlockSpec((seq_len, w), lambda b, j: (b, j)), st, st],
        out_shape=[jax.ShapeDtypeStruct((n_seq * seq_len, RET_QW), BF16),
                   jax.ShapeDtypeStruct((n_seq, npair, LANES, LANES), F32),
                   jax.ShapeDtypeStruct((n_seq, npair, LANES, LANES), F32)],
        scratch_shapes=[pltpu.VMEM((seq_len, w), F32), pltpu.VMEM((seq_len, w), F32)],
        compiler_params=_cparams(("arbitrary", "arbitrary")),
        name="retention",
    )(p, p, p, p, dm, qd, kd, cm, *states)


def _retention_tables(decay_f, decay_b):
    npair = RET_HEADS // 2
    rc = RET_CHUNK
    idx = jnp.arange(rc, dtype=F32)
    diff = idx[:, None] - idx[None, :]

    def one(decay, backward):
        lg = jax.nn.log_sigmoid(decay.astype(F32))
        dmask = jnp.where(diff >= 0, jnp.exp(lg[:, None, None] * jnp.maximum(diff, 0.0)), 0.0)
        q_dec = jnp.exp(lg[:, None] * (idx + 1.0))
        k_dec = jnp.exp(lg[:, None] * (rc - 1.0 - idx))
        c_dec = jnp.exp(lg * rc)
        if backward:
            dmask = jnp.swapaxes(dmask, 1, 2)
            q_dec = q_dec[:, ::-1]
            k_dec = k_dec[:, ::-1]
        return dmask, q_dec, k_dec, c_dec

    def lanes(t):
        t = t.reshape(npair, 2, rc)
        return jnp.repeat(jnp.swapaxes(t, 1, 2), HEAD_DIM, axis=2)

    parts = [one(decay_f, False), one(decay_b, True)]
    dm = jnp.stack([p[0].reshape(npair, 2, rc, rc) for p in parts], axis=1)
    qd = jnp.stack([lanes(p[1]) for p in parts], axis=1)
    kd = jnp.stack([lanes(p[2]) for p in parts], axis=1)
    bd = _pair_block_diag((LANES, LANES))
    cm = jnp.stack([jnp.where(bd[None], jnp.repeat(p[3].reshape(npair, 2), HEAD_DIM, axis=1)[:, :, None], 0.0)
                    for p in parts], axis=1)
    return dm, qd, kd, cm


def _pair_states(s):
    b = s.shape[0]
    s = s.astype(F32).reshape(b, RET_HEADS // 2, 2, HEAD_DIM, HEAD_DIM)
    z = jnp.zeros_like(s[:, :, 0])
    top = jnp.concatenate([s[:, :, 0], z], axis=-1)
    bot = jnp.concatenate([z, s[:, :, 1]], axis=-1)
    return jnp.concatenate([top, bot], axis=-2)


def _unpair_states(s):
    b = s.shape[0]
    h0 = s[:, :, :HEAD_DIM, :HEAD_DIM]
    h1 = s[:, :, HEAD_DIM:, HEAD_DIM:]
    return jnp.stack([h0, h1], axis=2).reshape(b, RET_HEADS, HEAD_DIM, HEAD_DIM)


LOG2E = 1.4426950408889634


def _kv_variants(k, v):
    lane = lax.broadcasted_iota(jnp.int32, k.shape, 1)
    lo = lane < HEAD_DIM
    ka = jnp.where(lo, k, 0.0)
    kb = jnp.where(lo, 0.0, k)
    va = jnp.where(lane == HEAD_DIM, 1.0, jnp.where(lo, v, 0.0))
    vb = jnp.where(lane == 0, 1.0, jnp.where(lo, 0.0, v))
    ks = (ka, pltpu.roll(ka, HEAD_DIM, 1), pltpu.roll(kb, HEAD_DIM, 1), kb)
    vs = (va, pltpu.roll(va, HEAD_DIM, 1), pltpu.roll(vb, HEAD_DIM, 1), vb)
    return [t.astype(BF16) for t in ks], [t.astype(BF16) for t in vs]


def _attn_core(q, k, v, sink_ref, o_ref, mask, rows=slice(None)):
    tq = q.shape[0]
    ks, vs = _kv_variants(k, v)
    lane = lax.broadcasted_iota(jnp.int32, (2 * tq, LANES), 1)
    first = lax.broadcasted_iota(jnp.int32, (2 * tq, 1), 0) < tq
    if mask is not None:
        mask = jnp.concatenate([mask, mask], axis=0)
    for half in (0, 1):
        q2 = jnp.concatenate([q[:, (2 * half) * LANES:(2 * half + 1) * LANES],
                              q[:, (2 * half + 1) * LANES:(2 * half + 2) * LANES]], axis=0)
        q2 = (q2 * LOG2E).astype(BF16)
        out = None
        for hh in (0, 1):
            var = 2 * half + hh
            s = _dot_nt(q2, ks[var])
            if mask is not None:
                s = jnp.where(mask, s, NEG_INF)
            m = jnp.max(s, axis=-1, keepdims=True)
            if sink_ref is not None:
                snk = LOG2E * jnp.where(first, sink_ref[2 * half, hh:hh + 1, 0:1],
                                        sink_ref[2 * half + 1, hh:hh + 1, 0:1])
                m = jnp.maximum(m, snk)
            acc = _dot(jnp.exp2((s - m).astype(BF16)), vs[var])
            ones_lane = HEAD_DIM if hh == 0 else 0
            den = acc[:, ones_lane:ones_lane + 1]
            if sink_ref is not None:
                den = den + jnp.exp2(snk - m)
            own = (lane < HEAD_DIM) if hh == 0 else (lane >= HEAD_DIM)
            o = jnp.where(own, acc, 0.0) * (1.0 / den)
            out = o if out is None else out + o
        out = out.astype(o_ref.dtype)
        o_ref[rows, (2 * half) * LANES:(2 * half + 1) * LANES] = out[:tq]
        o_ref[rows, (2 * half + 1) * LANES:(2 * half + 2) * LANES] = out[tq:]


def _full_attn_kernel(q_ref, k_ref, v_ref, *rest, has_sink):
    if has_sink:
        sink_ref, o_ref = rest
    else:
        sink_ref, (o_ref,) = None, rest
    for s in range(PROMPT_SEQS_PER_STEP):
        r = slice(s * SEQ, (s + 1) * SEQ)
        _attn_core(q_ref[r, :], k_ref[r, :], v_ref[r, :], sink_ref, o_ref, None, r)


PROMPT_SEQS_PER_STEP = 4


def _full_attention(p, q_cb0, k_cb0, v_cb0, n_groups, sink=None):
    rows = PROMPT_SEQS_PER_STEP * SEQ
    qw = 4 * LANES
    in_specs = [
        pl.BlockSpec((rows, qw), lambda b, g: (b, q_cb0 + g)),
        pl.BlockSpec((rows, LANES), lambda b, g: (b, k_cb0 + g)),
        pl.BlockSpec((rows, LANES), lambda b, g: (b, v_cb0 + g)),
    ]
    args = [p, p, p]
    if sink is not None:
        in_specs.append(pl.BlockSpec((None, 4, 2, LANES), lambda b, g: (g, 0, 0, 0)))
        args.append(sink)
    return pl.pallas_call(
        functools.partial(_full_attn_kernel, has_sink=sink is not None),
        grid=(T_PROMPT // rows, n_groups),
        in_specs=in_specs,
        out_specs=pl.BlockSpec((rows, qw), lambda b, g: (b, g)),
        out_shape=jax.ShapeDtypeStruct((T_PROMPT, n_groups * qw), BF16),
        compiler_params=_cparams(("arbitrary", "arbitrary")),
        name="full_attention",
    )(*args)


def _latent_attn_kernel(q_ref, k_ref, v_ref, kx_ref, vx_ref, o_ref, kvar, vvar, *, tq, kb):
    lk = DEC_SEQ + PAST_LEN

    @pl.when(pl.program_id(2) == 0)
    def _():
        for src_k, src_v, r0, nrows in ((k_ref, v_ref, 0, DEC_SEQ), (kx_ref, vx_ref, DEC_SEQ, PAST_LEN)):
            for c in range(nrows // kb):
                src = pl.ds(c * kb, kb)
                dst = pl.ds(r0 + c * kb, kb)
                ks, vs = _kv_variants(src_k[src, :], src_v[src, :])
                for i in range(4):
                    kvar[i, dst, :] = ks[i]
                    vvar[i, dst, :] = vs[i]

    lane = lax.broadcasted_iota(jnp.int32, (2 * tq, LANES), 1)
    for half in (0, 1):
        q2 = jnp.concatenate([q_ref[:, (2 * half) * LANES:(2 * half + 1) * LANES],
                              q_ref[:, (2 * half + 1) * LANES:(2 * half + 2) * LANES]], axis=0)
        q2 = (q2 * LOG2E).astype(BF16)
        out = None
        for hh in (0, 1):
            var = 2 * half + hh
            m = jnp.full((2 * tq, 1), -jnp.inf, F32)
            acc = jnp.zeros((2 * tq, LANES), F32)
            for j in range(lk // kb):
                rows = pl.ds(j * kb, kb)
                s = _dot_nt(q2, kvar[var, rows, :])
                m_new = jnp.maximum(m, jnp.max(s, axis=-1, keepdims=True))
                e = jnp.exp2((s - m_new).astype(BF16))
                acc = jnp.exp2(m - m_new) * acc + _dot(e, vvar[var, rows, :])
                m = m_new
            own = (lane < HEAD_DIM) if hh == 0 else (lane >= HEAD_DIM)
            ones_lane = HEAD_DIM if hh == 0 else 0
            o = jnp.where(own, acc, 0.0) * (1.0 / acc[:, ones_lane:ones_lane + 1])
            out = o if out is None else out + o
        out = out.astype(o_ref.dtype)
        o_ref[:, (2 * half) * LANES:(2 * half + 1) * LANES] = out[:tq]
        o_ref[:, (2 * half + 1) * LANES:(2 * half + 2) * LANES] = out[tq:]


def _latent_attention(p, kx, vx):
    tq, kb = 512, 512
    nq = DEC_SEQ // tq
    ngr = ATT_KV_HEADS // 2
    kcb = ATT_QW // LANES
    rb_q = T_PROMPT // tq
    rb_k = T_PROMPT // DEC_SEQ
    lk = DEC_SEQ + PAST_LEN
    ctx = pl.BlockSpec((None, PAST_LEN, LANES), lambda b, g, i: (b, 0, g))
    return pl.pallas_call(
        functools.partial(_latent_attn_kernel, tq=tq, kb=kb),
        grid=(DEC_BATCH, ngr, nq),
        in_specs=[pl.BlockSpec((tq, 4 * LANES), lambda b, g, i: (rb_q + b * nq + i, g)),
                  pl.BlockSpec((DEC_SEQ, LANES), lambda b, g, i: (rb_k + b, kcb + g)),
                  pl.BlockSpec((DEC_SEQ, LANES), lambda b, g, i: (rb_k + b, kcb + ngr + g)),
                  ctx, ctx],
        out_specs=pl.BlockSpec((tq, 4 * LANES), lambda b, g, i: (b * nq + i, g)),
        out_shape=jax.ShapeDtypeStruct((T_SAMPLE, ATT_QW), BF16),
        scratch_shapes=[pltpu.VMEM((4, lk, LANES), BF16), pltpu.VMEM((4, lk, LANES), BF16)],
        compiler_params=_cparams(("arbitrary", "arbitrary", "arbitrary")),
        name="latent_attention",
    )(p, p, p, kx, vx)


WIN_TQ = 2 * BLK
WIN_KBLKS = WIN_TQ // BLK + 2


def _window_attn_kernel(q_ref, *refs):
    k_refs = refs[:WIN_KBLKS]
    v_refs = refs[WIN_KBLKS:2 * WIN_KBLKS]
    kx_ref, vx_ref, sink_ref, o_ref = refs[2 * WIN_KBLKS:]
    n = pl.program_id(1)
    k = jnp.concatenate([r[...] for r in k_refs] + [kx_ref[...]], axis=0)
    v = jnp.concatenate([r[...] for r in v_refs] + [vx_ref[...]], axis=0)
    nloc = WIN_KBLKS * BLK
    shape = (WIN_TQ, nloc + PAST_LEN)
    i = lax.broadcasted_iota(jnp.int32, shape, 0)
    r = lax.broadcasted_iota(jnp.int32, shape, 1)
    kpos = n * WIN_TQ - BLK + r
    local = (jnp.abs(r - BLK - i) <= WINDOW) & (kpos >= 0) & (kpos < DEC_SEQ)
    mask = local | (r >= nloc)
    _attn_core(q_ref[...], k, v, sink_ref, o_ref, mask)


def _window_attention(p, kx, vx, sink):
    nb = DEC_SEQ // BLK
    nq = DEC_SEQ // WIN_TQ
    rb0 = T_PROMPT // BLK
    qcb = (4 * RET_QW) // (4 * LANES)
    kcb = (4 * RET_QW + WIN_QW) // LANES
    vcb = kcb + 1

    def kv(cb, off):
        return pl.BlockSpec((BLK, LANES),
                            lambda b, n: (rb0 + b * nb + jnp.clip(n * (WIN_TQ // BLK) + off, 0, nb - 1), cb))

    offs = range(-1, WIN_KBLKS - 1)
    ctx = pl.BlockSpec((None, PAST_LEN, LANES), lambda b, n: (b, 0, 0))
    qspec = pl.BlockSpec((WIN_TQ, 4 * LANES), lambda b, n: (T_PROMPT // WIN_TQ + b * nq + n, qcb))
    return pl.pallas_call(
        _window_attn_kernel,
        grid=(DEC_BATCH, nq),
        in_specs=[qspec] + [kv(kcb, o) for o in offs] + [kv(vcb, o) for o in offs] + [
            ctx, ctx, pl.BlockSpec((4, 2, LANES), lambda b, n: (0, 0, 0))],
        out_specs=pl.BlockSpec((WIN_TQ, 4 * LANES), lambda b, n: (b * nq + n, 0)),
        out_shape=jax.ShapeDtypeStruct((T_SAMPLE, WIN_QW), BF16),
        compiler_params=_cparams(("arbitrary", "arbitrary")),
        name="window_attention",
    )(*([p] * (1 + 2 * WIN_KBLKS)), kx, vx, sink)


def _sink_table(sink, n_groups):
    s = sink.astype(F32).reshape(n_groups, 4, 2, 1)
    return jnp.broadcast_to(s, (n_groups, 4, 2, LANES))


def _cache_rows(cache):
    b, h, l, d = cache.shape
    return cache.astype(F32).transpose(0, 2, 1, 3).reshape(b, l, h * d)


def _outproj_kernel(*refs, n_in):
    a_refs = refs[:2 * n_in]
    w_refs = refs[2 * n_in:3 * n_in]
    xp_ref, xs_ref, mod_ref, g_ref, wr_ref, xn_ref, hfp_ref, hfs_ref, lg_ref = refs[3 * n_in:]
    i = pl.program_id(0)
    y = None
    for k, w_ref in enumerate(w_refs):
        a = _pick(a_refs[2 * k], a_refs[2 * k + 1])
        t = _dot(a.astype(BF16), w_ref[...])
        y = t if y is None else y + t
    xn = _pick(xp_ref, xs_ref) + mod_ref[2:3, :] * _rms(y, g_ref[1:2, :])
    hf = _rms(xn, g_ref[2:3, :]) * (1.0 + mod_ref[4:5, :]) + mod_ref[3:4, :]
    xn_ref[...] = xn
    lg_ref[...] = _dot_nt(wr_ref[...], hf.astype(BF16))

    @pl.when(i < NPT)
    def _():
        hfp_ref[...] = hf.astype(hfp_ref.dtype)

    @pl.when(i >= NPT)
    def _():
        hfs_ref[...] = hf.reshape(TM, D_MODEL // LANES, LANES)


def _outproj(parts, w_parts, x, mod, layer, gains, w_router_t):
    n_in = len(parts)
    in_specs = []
    args = []
    for a_p, a_s in parts:
        in_specs += _pair_specs(a_p.shape[1])
        args += [a_p, a_s]
    in_specs += [pl.BlockSpec(w.shape, lambda i: (0, 0)) for w in w_parts]
    in_specs += _pair_specs(D_MODEL) + [
        pl.BlockSpec((None, None, 6, D_MODEL), lambda i: (layer, _mod_index(i), 0, 0)),
        pl.BlockSpec((4, D_MODEL), lambda i: (0, 0)),
        pl.BlockSpec((N_EXPERTS, D_MODEL), lambda i: (0, 0)),
    ]
    row = pl.BlockSpec((TM, D_MODEL), lambda i: (i, 0))
    hf_p, _ = _pair_specs(D_MODEL)
    hf_s = pl.BlockSpec((TM, D_MODEL // LANES, LANES), lambda i: (jnp.maximum(i - NPT, 0), 0, 0))
    return pl.pallas_call(
        functools.partial(_outproj_kernel, n_in=n_in),
        grid=(T_ALL // TM,),
        in_specs=in_specs,
        out_specs=[row, hf_p, hf_s, pl.BlockSpec((N_EXPERTS, TM), lambda i: (0, i))],
        out_shape=[jax.ShapeDtypeStruct((T_ALL, D_MODEL), F32),
                   jax.ShapeDtypeStruct((T_PROMPT, D_MODEL), BF16),
                   jax.ShapeDtypeStruct((T_SAMPLE, D_MODEL // LANES, LANES), F32),
                   jax.ShapeDtypeStruct((N_EXPERTS, T_ALL), F32)],
        compiler_params=_cparams(("arbitrary",)),
        name="outproj",
    )(*args, *w_parts, x[0], x[1], mod, gains, w_router_t)


TILE_SUB = D_MODEL // LANES


def _gather_kernel(idx_ref, h_ref, o_ref, buf, *, eg, cap):
    b = pl.program_id(0)
    g = pl.program_id(1)
    for e in range(eg):
        base = (b * N_EXPERTS + g * eg + e) * cap

        def body(c, carry):
            buf[c] = h_ref[idx_ref[base + c]]
            return carry

        lax.fori_loop(0, cap, body, 0, unroll=8)
        o_ref[e] = buf[...].reshape(cap, D_MODEL).astype(BF16)


def _gather(h3, idx, n_seq, seq_len, cap, eg):
    return pl.pallas_call(
        functools.partial(_gather_kernel, eg=eg, cap=cap),
        grid_spec=pltpu.PrefetchScalarGridSpec(
            num_scalar_prefetch=1,
            grid=(n_seq, N_EXPERTS // eg),
            in_specs=[pl.BlockSpec((seq_len, TILE_SUB, LANES), lambda b, g, idx: (b, 0, 0))],
            out_specs=pl.BlockSpec((eg, cap, D_MODEL), lambda b, g, idx: (g, b, 0)),
            scratch_shapes=[pltpu.VMEM((cap, TILE_SUB, LANES), F32)],
        ),
        out_shape=jax.ShapeDtypeStruct((N_EXPERTS, n_seq * cap, D_MODEL), BF16),
        compiler_params=_cparams(("arbitrary", "arbitrary")),
        name="moe_gather",
    )(idx, h3)


def _moe_kernel(xp_ref, xs_ref, wg_ref, wu_ref, wd_ref, gwp_ref, gws_ref, yp_ref, ys_ref, hid_scr, wd_scr):
    f = pl.program_id(1)
    wg = wg_ref[...].astype(BF16)
    wu = wu_ref[...].astype(BF16)
    cols = pl.ds(pl.multiple_of(f * TF, TF), TF)
    wd_scr[cols, :] = wd_ref[...].astype(BF16)
    for h, x_ref in enumerate((xp_ref, xs_ref)):
        x = x_ref[...]
        hid_scr[h, :, cols] = (_silu(_dot(x, wg)) * _dot(x, wu)).astype(BF16)

    @pl.when(f == pl.num_programs(1) - 1)
    def _():
        for h, (gw_ref, y_ref) in enumerate(((gwp_ref, yp_ref), (gws_ref, ys_ref))):
            gw = jnp.broadcast_to(gw_ref[...], (LANES, hid_scr.shape[1])).T[:, 0:1]
            for r in range(0, hid_scr.shape[1], MOE_ROW_TILE):
                rows = slice(r, r + MOE_ROW_TILE)
                y_ref[rows, :] = _dot(hid_scr[h, rows, :], wd_scr[...]) * gw[rows, :]


MOE_ROW_TILE = 512


def _moe_ffn(xg_p, xg_s, w_gate, w_up, w_down, layer, gw_p, gw_s):
    assert ROWS_P == ROWS_S
    nf = EXPERT_FF // TF
    xspec_p = pl.BlockSpec((None, ROWS_P, D_MODEL), lambda e, f: (e, 0, 0))
    xspec_s = pl.BlockSpec((None, ROWS_S, D_MODEL), lambda e, f: (e, 0, 0))
    return pl.pallas_call(
        _moe_kernel,
        grid=(N_EXPERTS, nf),
        in_specs=[xspec_p, xspec_s,
                  pl.BlockSpec((None, None, D_MODEL, TF), lambda e, f: (layer, e, 0, f)),
                  pl.BlockSpec((None, None, D_MODEL, TF), lambda e, f: (layer, e, 0, f)),
                  pl.BlockSpec((None, None, TF, D_MODEL), lambda e, f: (layer, e, f, 0)),
                  pl.BlockSpec((None, 1, ROWS_P), lambda e, f: (e, 0, 0)),
                  pl.BlockSpec((None, 1, ROWS_S), lambda e, f: (e, 0, 0))],
        out_specs=[xspec_p, xspec_s],
        out_shape=[jax.ShapeDtypeStruct((N_EXPERTS, ROWS_P, D_MODEL), F32),
                   jax.ShapeDtypeStruct((N_EXPERTS, ROWS_S, D_MODEL), F32)],
        scratch_shapes=[pltpu.VMEM((2, ROWS_P, EXPERT_FF), BF16),
                        pltpu.VMEM((EXPERT_FF, D_MODEL), BF16)],
        compiler_params=_cparams(("arbitrary", "arbitrary")),
        name="moe_ffn",
    )(xg_p, xg_s, w_gate, w_up, w_down, gw_p, gw_s)


COMBINE_GROUP = 8


def _combine_kernel(idx_ref, y_ref, o_ref, y3, *, eg, cap):
    b = pl.program_id(0)
    g = pl.program_id(1)

    @pl.when(g == 0)
    def _():
        o_ref[...] = jnp.zeros_like(o_ref)

    for e in range(eg):
        base = (b * N_EXPERTS + g * eg + e) * cap
        y3[...] = y_ref[e].reshape(cap, TILE_SUB, LANES)

        def body(c, carry):
            c0 = c * COMBINE_GROUP
            toks = [idx_ref[base + c0 + k] for k in range(COMBINE_GROUP)]
            rows = [o_ref[t] + y3[c0 + k] for k, t in enumerate(toks)]
            for t, row in zip(toks, rows):
                o_ref[t] = row
            return carry

        lax.fori_loop(0, cap // COMBINE_GROUP, body, 0)


def _combine(y, idx, n_seq, seq_len, cap, eg):
    return pl.pallas_call(
        functools.partial(_combine_kernel, eg=eg, cap=cap),
        grid_spec=pltpu.PrefetchScalarGridSpec(
            num_scalar_prefetch=1,
            grid=(n_seq, N_EXPERTS // eg),
            in_specs=[pl.BlockSpec((eg, cap, D_MODEL), lambda b, g, idx: (g, b, 0))],
            out_specs=pl.BlockSpec((seq_len, TILE_SUB, LANES), lambda b, g, idx: (b, 0, 0)),
            scratch_shapes=[pltpu.VMEM((cap, TILE_SUB, LANES), F32)],
        ),
        out_shape=jax.ShapeDtypeStruct((n_seq * seq_len, TILE_SUB, LANES), F32),
        compiler_params=_cparams(("arbitrary", "arbitrary")),
        name="moe_combine",
    )(idx, y)


def _onehot_t(idx_row, n_tok):
    tok = lax.broadcasted_iota(jnp.int32, (n_tok, idx_row.shape[-1]), 0)
    return jnp.where(tok == idx_row, 1.0, 0.0)


def _gather_mm_kernel(idx_ref, h_ref, o_ref, *, cap, seq_len):
    for s in range(PROMPT_SEQS_PER_STEP):
        sel = _onehot_t(idx_ref[s], seq_len).T.astype(BF16)
        h = h_ref[s * seq_len:(s + 1) * seq_len, :].astype(BF16)
        rows = _dot(sel, h).astype(BF16)
        for e in range(N_EXPERTS):
            o_ref[e, s * cap:(s + 1) * cap, :] = rows[e * cap:(e + 1) * cap]


def _gather_mm(h, idx, n_seq, seq_len, cap):
    ns = PROMPT_SEQS_PER_STEP
    return pl.pallas_call(
        functools.partial(_gather_mm_kernel, cap=cap, seq_len=seq_len),
        grid=(n_seq // ns,),
        in_specs=[pl.BlockSpec((ns, 1, N_EXPERTS * cap), lambda b: (b, 0, 0)),
                  pl.BlockSpec((ns * seq_len, D_MODEL), lambda b: (b, 0))],
        out_specs=pl.BlockSpec((N_EXPERTS, ns * cap, D_MODEL), lambda b: (0, b, 0)),
        out_shape=jax.ShapeDtypeStruct((N_EXPERTS, n_seq * cap, D_MODEL), BF16),
        compiler_params=_cparams(("arbitrary",)),
        name="moe_gather_mm",
    )(idx, h)


def _combine_mm_kernel(idx_ref, y_ref, o_ref, *, cap, seq_len):
    for s in range(PROMPT_SEQS_PER_STEP):
        sel = _onehot_t(idx_ref[s], seq_len).astype(BF16)
        y = jnp.concatenate([y_ref[e, s * cap:(s + 1) * cap, :] for e in range(N_EXPERTS)], axis=0)
        hi = y.astype(BF16)
        lo = (y - hi.astype(F32)).astype(BF16)
        o_ref[s * seq_len:(s + 1) * seq_len, :] = _dot(sel, hi) + _dot(sel, lo)


def _combine_mm(y, idx, n_seq, seq_len, cap):
    ns = PROMPT_SEQS_PER_STEP
    return pl.pallas_call(
        functools.partial(_combine_mm_kernel, cap=cap, seq_len=seq_len),
        grid=(n_seq // ns,),
        in_specs=[pl.BlockSpec((ns, 1, N_EXPERTS * cap), lambda b: (b, 0, 0)),
                  pl.BlockSpec((N_EXPERTS, ns * cap, D_MODEL), lambda b: (0, b, 0))],
        out_specs=pl.BlockSpec((ns * seq_len, D_MODEL), lambda b: (b, 0)),
        out_shape=jax.ShapeDtypeStruct((n_seq * seq_len, D_MODEL), F32),
        compiler_params=_cparams(("arbitrary",)),
        name="moe_combine_mm",
    )(idx, y)


def _postnorm_kernel(x_ref, yp_ref, ys_ref, mod_ref, g_ref, op_ref, os_ref):
    i = pl.program_id(0)
    y = jnp.where(i < NPT, yp_ref[...], ys_ref[...].reshape(TM, D_MODEL))
    o = x_ref[...] + mod_ref[5:6, :] * _rms(y, g_ref[3:4, :])

    @pl.when(i < NPT)
    def _():
        op_ref[...] = o

    @pl.when(i >= NPT)
    def _():
        os_ref[...] = o


def _postnorm(x, y_p, y_s, mod, layer, gains):
    return pl.pallas_call(
        _postnorm_kernel,
        grid=(T_ALL // TM,),
        in_specs=[pl.BlockSpec((TM, D_MODEL), lambda i: (i, 0)), _pair_specs(D_MODEL)[0],
                  pl.BlockSpec((TM, TILE_SUB, LANES), lambda i: (jnp.maximum(i - NPT, 0), 0, 0)),
                  pl.BlockSpec((None, None, 6, D_MODEL), lambda i: (layer, _mod_index(i), 0, 0)),
            pl.BlockSpec((4, D_MODEL), lambda i: (0, 0))],
        out_specs=_pair_specs(D_MODEL),
        out_shape=[jax.ShapeDtypeStruct((T_PROMPT, D_MODEL), F32),
                   jax.ShapeDtypeStruct((T_SAMPLE, D_MODEL), F32)],
        compiler_params=_cparams(("arbitrary",)),
        name="postnorm",
    )(x, y_p, y_s, mod, gains)


def _excl_prefix(mask):
    r, n = mask.shape
    i0 = lax.broadcasted_iota(jnp.int32, (LANES, LANES), 0)
    i1 = lax.broadcasted_iota(jnp.int32, (LANES, LANES), 1)
    upper = jnp.where(i0 < i1, 1.0, 0.0).astype(BF16)
    ones = jnp.where(mask, 1.0, 0.0)
    carry = jnp.zeros((r, 1), F32)
    out = []
    for c in range(n // LANES):
        ch = ones[:, c * LANES:(c + 1) * LANES]
        out.append(_dot(ch.astype(BF16), upper) + carry)
        carry = carry + jnp.sum(ch, axis=1, keepdims=True)
    return jnp.concatenate(out, axis=1)


COMPACT_SLOTS = 256


def _compact_mxu(pos, aff, idx_ref, gw_ref, n, cap):
    sub = lax.broadcasted_iota(jnp.int32, (8, n), 0)
    tok = lax.broadcasted_iota(jnp.int32, (8, n), 1)
    base = jnp.where(sub == 0, tok >> 6, jnp.where(sub == 1, tok & 63, 0)).astype(F32)
    pad = jnp.zeros((LANES - 8, n), F32)
    slot0 = lax.broadcasted_iota(jnp.int32, (COMPACT_SLOTS, n), 0).astype(F32)
    for e in range(N_EXPERTS):
        a = aff[e:e + 1, :]
        a_hi = a.astype(BF16).astype(F32)
        rest = a - a_hi
        a_mid = rest.astype(BF16).astype(F32)
        a_lo = rest - a_mid
        pay = base + jnp.where(sub == 2, a_hi, 0.0) + jnp.where(sub == 3, a_mid, 0.0) + jnp.where(sub == 4, a_lo, 0.0)
        pay = jnp.concatenate([pay, pad], axis=0).astype(BF16)
        pos_e = pos[e:e + 1, :]
        for c in range(cap // COMPACT_SLOTS):
            onehot = jnp.where(pos_e == slot0 + float(c * COMPACT_SLOTS), 1.0, 0.0).astype(BF16)
            got = _dot_nt(onehot, pay).T
            cols = slice(c * COMPACT_SLOTS, (c + 1) * COMPACT_SLOTS)
            idx_ref[0, e:e + 1, cols] = (got[0:1] * 64.0 + got[1:2]).astype(jnp.int32)
            gw_ref[0, e:e + 1, cols] = got[2:3] + got[3:4] + got[4:5]


def _route_kernel(lg_ref, idx_ref, gw_ref, *, nb, n, cap):
    affs = []
    for s in range(nb):
        l = lg_ref[:, s * n:(s + 1) * n]
        e = jnp.exp(l - jnp.max(l, axis=0, keepdims=True))
        affs.append(e / jnp.sum(e, axis=0, keepdims=True))
    aff = affs[0] if nb == 1 else jnp.concatenate(affs, axis=0)
    rows = nb * N_EXPERTS
    bits = jnp.zeros((rows, 1), jnp.int32)
    for bit in range(30, -1, -1):
        cand = bits | (1 << bit)
        cnt = jnp.sum(jnp.where(aff >= pltpu.bitcast(cand, F32), 1.0, 0.0), axis=1, keepdims=True)
        bits = jnp.where(cnt >= cap, cand, bits)
    thr = pltpu.bitcast(bits, F32)
    gt = aff > thr
    eq = aff == thr
    need = cap - jnp.sum(jnp.where(gt, 1.0, 0.0), axis=1, keepdims=True)
    sel = gt | (eq & (_excl_prefix(eq) < need))
    pos = jnp.where(sel, _excl_prefix(sel), -1.0)
    if cap > LANES:
        _compact_mxu(pos, aff, idx_ref, gw_ref, n, cap)
        return
    tok = lax.broadcasted_iota(jnp.int32, (rows, n), 1).astype(F32)
    capw = max(cap, LANES)
    slot_lane = lax.broadcasted_iota(jnp.int32, (rows, capw), 1)

    def slot(s, carry):
        idx_acc, gw_acc = carry
        hit = pos == lax.convert_element_type(s, F32)
        ic = jnp.sum(jnp.where(hit, tok, 0.0), axis=1, keepdims=True)
        gc = jnp.sum(jnp.where(hit, aff, 0.0), axis=1, keepdims=True)
        here = slot_lane == s
        return jnp.where(here, ic, idx_acc), jnp.where(here, gc, gw_acc)

    zero = jnp.zeros((rows, capw), F32)
    idx_acc, gw_acc = lax.fori_loop(0, cap, slot, (zero, zero), unroll=8)
    for s in range(nb):
        r = slice(s * N_EXPERTS, (s + 1) * N_EXPERTS)
        idx_ref[s] = idx_acc[r, :cap].astype(jnp.int32)
        gw_ref[s] = gw_acc[r, :cap]


def _route(logits_t, n_seq, seq_len, cap, col0, nb):
    cb0 = col0 // (nb * seq_len)
    out = pl.BlockSpec((nb, N_EXPERTS, cap), lambda b: (b, 0, 0))
    return pl.pallas_call(
        functools.partial(_route_kernel, nb=nb, n=seq_len, cap=cap),
        grid=(n_seq // nb,),
        in_specs=[pl.BlockSpec((N_EXPERTS, nb * seq_len), lambda b: (0, cb0 + b))],
        out_specs=[out, out],
        out_shape=[jax.ShapeDtypeStruct((n_seq, N_EXPERTS, cap), jnp.int32),
                   jax.ShapeDtypeStruct((n_seq, N_EXPERTS, cap), F32)],
        compiler_params=_cparams(("arbitrary",)),
        name="route",
    )(logits_t)


def _expert_ffn(hf_p, hf_s, logits_t, w_gate, w_up, w_down, layer):
    idx_p, gw_p = _route(logits_t, BATCH, SEQ, CAP_P, 0, 8)
    idx_s, gw_s = _route(logits_t, DEC_BATCH, DEC_SEQ, CAP_S, T_PROMPT, 1)
    idx_p = idx_p.reshape(BATCH, 1, N_EXPERTS * CAP_P)
    idx_s = idx_s.reshape(DEC_BATCH * N_EXPERTS * CAP_S)
    xg_p = _gather_mm(hf_p, idx_p, BATCH, SEQ, CAP_P)
    xg_s = _gather(hf_s, idx_s, DEC_BATCH, DEC_SEQ, CAP_S, 4)
    gw_p = gw_p.transpose(1, 0, 2).reshape(N_EXPERTS, 1, ROWS_P)
    gw_s = gw_s.transpose(1, 0, 2).reshape(N_EXPERTS, 1, ROWS_S)
    y_p, y_s = _moe_ffn(xg_p, xg_s, w_gate, w_up, w_down, layer, gw_p, gw_s)
    o_p = _combine_mm(y_p, idx_p, BATCH, SEQ, CAP_P)
    o_s = _combine(y_s, idx_s, DEC_BATCH, DEC_SEQ, CAP_S, 2)
    return o_p, o_s


def _rope_tables():
    t = np.arange(DEC_SEQ)
    row = (t // GRID_W).astype(np.float32)
    col = (t % GRID_W).astype(np.float32)
    inv = (ROPE_THETA ** (-np.arange(0, AXIS_DIM, 2, dtype=np.float32) / AXIS_DIM)).astype(np.float32)
    ar = row[:, None] * inv[None]
    ac = col[:, None] * inv[None]
    ang = np.concatenate([ar, ar, ac, ac], axis=-1)
    ang = np.concatenate([ang, ang], axis=-1)
    cos = np.concatenate([np.ones((TM, LANES), np.float32), np.cos(ang)], axis=0)
    sin = np.concatenate([np.zeros((TM, LANES), np.float32), np.sin(ang)], axis=0)
    return jnp.asarray(cos, F32), jnp.asarray(sin, F32)


def kernel(x_prompt, x_sample, state_ret_fwd, state_ret_bwd, cache_win_k, cache_win_v, cache_attn_k, cache_attn_v, c, c_ctx, w_ada, b_ada, norm_gains, w_in_even, w_out_even, ret_decay_fwd, ret_decay_bwd, win_sink, w_in_odd, w_out_odd, q_norm, k_norm, w_router, w_gate, w_up, w_down):
    x = (x_prompt.reshape(T_PROMPT, D_MODEL), x_sample.reshape(T_SAMPLE, D_MODEL))
    cond =jnp.concatenate([c_ctx[None, :], c, jnp.zeros((8 - 1 - DEC_BATCH, D_MODEL), F32)], axis=0)
    mod = _modulation(cond, w_ada, b_ada).reshape(DEPTH, 8, 6, D_MODEL)
    cos_t, sin_t = _rope_tables()
    outs = {}
    for layer in range(DEPTH):
        j = layer // 2
        gains = norm_gains[layer]
        if layer % 2 == 0:
            p, outs["win_k"], outs["win_v"] = _inproj(x, mod, layer, gains[0], w_in_even[j].astype(BF16),
                                                      cos_t, sin_t)
            tabs = _retention_tables(ret_decay_fwd[j], ret_decay_bwd[j])
            ret_p, sf, sb = _retention(p, tabs, None, None, BATCH, SEQ, 0, 4)
            ret_s, _, _ = _retention(p, tabs, _pair_states(state_ret_fwd[:, j]),
                                     _pair_states(state_ret_bwd[:, j]), DEC_BATCH, DEC_SEQ, T_PROMPT, 2)
            sink = _sink_table(win_sink[j], 1)
            qcb = (4 * RET_QW) // (4 * LANES)
            kcb = (4 * RET_QW + WIN_QW) // LANES
            win_p = _full_attention(p, qcb, kcb, kcb + 1, 1, sink=sink)
            win_s = _window_attention(p, _cache_rows(cache_win_k[:, j]), _cache_rows(cache_win_v[:, j]),
                                      sink[0])
            w_out = w_out_even[j].astype(BF16)
            parts = [(ret_p, ret_s), (win_p, win_s)]
            w_parts = [w_out[:RET_QW], w_out[RET_QW:]]
            outs["ret_f"] = _unpair_states(sf)[:, None]
            outs["ret_b"] = _unpair_states(sb)[:, None]
        else:
            p, outs["att_k"], outs["att_v"] = _inproj(x, mod, layer, gains[0], w_in_odd[j].astype(BF16),
                                                      cos_t, sin_t, qn=q_norm[j], kn=k_norm[j])
            ngr = ATT_KV_HEADS // 2
            kcb = ATT_QW // LANES
            att_p = _full_attention(p, 0, kcb, kcb + ngr, ngr)
            att_s = _latent_attention(p, _cache_rows(cache_attn_k[:, j]), _cache_rows(cache_attn_v[:, j]))
            parts = [(att_p, att_s)]
            w_parts = [w_out_odd[j].astype(BF16)]
        xn, hf_p, hf_s, logits_t = _outproj(parts, w_parts, x, mod, layer, gains,
                                            w_router[layer].T.astype(BF16))
        o_p, o_s = _expert_ffn(hf_p, hf_s, logits_t, w_gate, w_up, w_down, layer)
        x = _postnorm(xn, o_p, o_s, mod, layer, gains)
    y_prompt = x[0].reshape(BATCH, SEQ, D_MODEL)
    y_sample = x[1].reshape(DEC_BATCH, DEC_SEQ, D_MODEL)
    return (y_prompt, y_sample, outs["ret_f"], outs["ret_b"], outs["win_k"], outs["win_v"],
            outs["att_k"], outs["att_v"])
```

```python
import functools

import jax
import jax.numpy as jnp
import numpy as np
from jax import lax
from jax.experimental import pallas as pl
from jax.experimental.pallas import tpu as pltpu

D_MODEL = 1024
BATCH = 32
SEQ = 256
DEPTH = 2
DEC_BATCH = 2
DEC_SEQ = 4096
PAST_LEN = 512
GRID_W = 64
HEAD_DIM = 64
AXIS_DIM = HEAD_DIM // 2
ROPE_THETA = 10000.0
BLK = 128
WINDOW = 128
RET_HEADS = D_MODEL // 128
WIN_HEADS = D_MODEL // 128
WIN_KV_HEADS = WIN_HEADS // 4
ATT_HEADS = D_MODEL // HEAD_DIM
ATT_KV_HEADS = ATT_HEADS // 4
RET_QW = RET_HEADS * HEAD_DIM
WIN_QW = WIN_HEADS * HEAD_DIM
WIN_KW = WIN_KV_HEADS * HEAD_DIM
EVEN_IN = 4 * RET_QW + WIN_QW + 2 * WIN_KW
ATT_QW = ATT_HEADS * HEAD_DIM
ATT_KW = ATT_KV_HEADS * HEAD_DIM
ODD_IN = ATT_QW + 2 * ATT_KW
N_EXPERTS = 16
CAPACITY_FACTOR = 2
EXPERT_FF = ((8 * D_MODEL // 3 + 127) // 128) * 128
EPS = 1e-6
NEG_INF = -1e30
F32 = jnp.float32
BF16 = jnp.bfloat16

LANES = 128
T_PROMPT = BATCH * SEQ
T_SAMPLE = DEC_BATCH * DEC_SEQ
T_ALL = T_PROMPT + T_SAMPLE
TM = 512
CAP_P = CAPACITY_FACTOR * SEQ // N_EXPERTS
CAP_S = CAPACITY_FACTOR * DEC_SEQ // N_EXPERTS
ROWS_P = BATCH * CAP_P
ROWS_S = DEC_BATCH * CAP_S
TF = 256
VMEM_LIMIT = 56 * 1024 * 1024
QK_SCALE = HEAD_DIM ** -0.5


def _cparams(sem):
    return pltpu.CompilerParams(dimension_semantics=sem, vmem_limit_bytes=VMEM_LIMIT)


def _silu(x):
    return x * (1.0 / (1.0 + jnp.exp(-x)))


def _dot(a, b):
    return jnp.dot(a, b, preferred_element_type=F32)


def _dot_nt(a, b):
    return lax.dot_general(a, b, (((1,), (1,)), ((), ())), preferred_element_type=F32)


def _rms(x, g):
    return x * lax.rsqrt(jnp.mean(x * x, axis=-1, keepdims=True) + EPS) * g


def _pair_block_diag(shape):
    r = lax.broadcasted_iota(jnp.int32, shape, 0)
    c = lax.broadcasted_iota(jnp.int32, shape, 1)
    return (r // HEAD_DIM) == (c // HEAD_DIM)


def _head_sum(x, bd):
    hi = x.astype(BF16)
    lo = (x - hi.astype(F32)).astype(BF16)
    return _dot(hi, bd) + _dot(lo, bd)


def _rope(xb, cos, sin):
    lane = lax.broadcasted_iota(jnp.int32, xb.shape, 1)
    half = AXIS_DIM // 2
    rot = jnp.where((lane % AXIS_DIM) < half,
                    -pltpu.roll(xb, LANES - half, 1), pltpu.roll(xb, half, 1))
    return xb * cos + rot * sin


def _mod_kernel(c_ref, w_ref, b_ref, o_ref):
    o_ref[...] = _dot(_silu(c_ref[...]), w_ref[...]) + b_ref[...]


def _modulation(cond, w_ada, b_ada):
    tn = 1536
    n = 6 * D_MODEL
    return pl.pallas_call(
        _mod_kernel,
        grid=(DEPTH, n // tn),
        in_specs=[
            pl.BlockSpec((8, D_MODEL), lambda l, j: (0, 0)),
            pl.BlockSpec((None, D_MODEL, tn), lambda l, j: (l, 0, j)),
            pl.BlockSpec((None, 1, tn), lambda l, j: (l, 0, j)),
        ],
        out_specs=pl.BlockSpec((None, 8, tn), lambda l, j: (l, 0, j)),
        out_shape=jax.ShapeDtypeStruct((DEPTH, 8, n), F32),
        compiler_params=_cparams(("arbitrary", "arbitrary")),
        name="modulation",
    )(cond, w_ada, b_ada.reshape(DEPTH, 1, n))


def _mod_index(i):
    npt = T_PROMPT // TM
    return jnp.where(i < npt, 0, 1 + (i - npt) // (DEC_SEQ // TM))


def _rope_index(i):
    npt = T_PROMPT // TM
    return jnp.where(i < npt, 0, 1 + (i - npt) % (DEC_SEQ // TM))


NPT = T_PROMPT // TM


def _pair_specs(width):
    return [pl.BlockSpec((TM, width), lambda i: (jnp.minimum(i, NPT - 1), 0)),
            pl.BlockSpec((TM, width), lambda i: (jnp.maximum(i - NPT, 0), 0))]


def _pick(p_ref, s_ref):
    return jnp.where(pl.program_id(0) < NPT, p_ref[...], s_ref[...])


def _inproj_kernel(xp_ref, xs_ref, mod_ref, g_ref, w_ref, cos_ref, sin_ref, *rest, even):
    if even:
        o_ref, ko_ref, vo_ref = rest
    else:
        qn_ref, kn_ref, o_ref, ko_ref, vo_ref = rest
    h = _rms(_pick(xp_ref, xs_ref), g_ref[...]) * (1.0 + mod_ref[1:2, :]) + mod_ref[0:1, :]
    h16 = h.astype(BF16)
    cos = cos_ref[...]
    sin = sin_ref[...]
    nblk = w_ref.shape[1] // LANES
    if even:
        nq = RET_QW // LANES
        rope_blocks = set(range(0, 2 * nq)) | set(range(4 * nq, 5 * nq + 1))
        scaled = set(range(nq, 2 * nq)) | set(range(4 * nq, 5 * nq))
        normed = {}
        cache_out = {5 * nq: (ko_ref, 0), 5 * nq + 1: (vo_ref, 0)}
    else:
        nq = ATT_QW // LANES
        nk = ATT_KW // LANES
        rope_blocks = set(range(0, nq + nk))
        scaled = set(range(0, nq))
        normed = {b: (qn_ref if b < nq else kn_ref) for b in range(nq + nk)}
        bd = _pair_block_diag((LANES, LANES)).astype(BF16)
        cache_out = {nq + b: (ko_ref, 2 * b) for b in range(nk)}
        cache_out.update({nq + nk + b: (vo_ref, 2 * b) for b in range(nk)})
    p = _dot(h16, w_ref[...])
    for b in range(nblk):
        blk = p[:, b * LANES:(b + 1) * LANES]
        if b in normed:
            ms = _head_sum(blk * blk, bd) * (1.0 / HEAD_DIM)
            blk = blk * lax.rsqrt(ms + EPS) * normed[b][...]
        if b in rope_blocks:
            blk = _rope(blk, cos, sin)
        if b in scaled:
            blk = blk * QK_SCALE
        o_ref[:, b * LANES:(b + 1) * LANES] = blk
        if b in cache_out:
            c_ref, head0 = cache_out[b]

            @pl.when(pl.program_id(0) < NPT)
            def _():
                for s in range(TM // SEQ):
                    for hh in range(2):
                        c_ref[s, 0, head0 + hh] = blk[s * SEQ:(s + 1) * SEQ, hh * HEAD_DIM:(hh + 1) * HEAD_DIM]


def _inproj(x, mod, layer, gain, w, cos_t, sin_t, qn=None, kn=None):
    even = qn is None
    n = w.shape[1]
    in_specs = _pair_specs(D_MODEL) + [
        pl.BlockSpec((None, None, 6, D_MODEL), lambda i: (layer, _mod_index(i), 0, 0)),
        pl.BlockSpec((1, D_MODEL), lambda i: (0, 0)),
        pl.BlockSpec((D_MODEL, n), lambda i: (0, 0)),
        pl.BlockSpec((TM, LANES), lambda i: (_rope_index(i), 0)),
        pl.BlockSpec((TM, LANES), lambda i: (_rope_index(i), 0)),
    ]
    args = [x[0], x[1], mod, gain.reshape(1, D_MODEL), w, cos_t, sin_t]
    if not even:
        in_specs += [pl.BlockSpec((1, LANES), lambda i: (0, 0))] * 2
        args += [jnp.tile(qn, 2).reshape(1, LANES), jnp.tile(kn, 2).reshape(1, LANES)]
    nkv = WIN_KV_HEADS if even else ATT_KV_HEADS
    cache = pl.BlockSpec((TM // SEQ, 1, nkv, SEQ, HEAD_DIM), lambda i: (jnp.minimum(i, NPT - 1), 0, 0, 0, 0))
    cache_shape = jax.ShapeDtypeStruct((BATCH, 1, nkv, SEQ, HEAD_DIM), F32)
    return pl.pallas_call(
        functools.partial(_inproj_kernel, even=even),
        grid=(T_ALL // TM,),
        in_specs=in_specs,
        out_specs=[pl.BlockSpec((TM, n), lambda i: (i, 0)), cache, cache],
        out_shape=[jax.ShapeDtypeStruct((T_ALL, n), F32), cache_shape, cache_shape],
        compiler_params=_cparams(("arbitrary",)),
        name="inproj_even" if even else "inproj_odd",
    )(*args)


RET_READOUT_ROWS = 512
RET_CHUNK = 2 * BLK


def _ret_kernel(q_ref, k_ref, v_ref, g_ref, dm_ref, qd_ref, kd_ref, cm_ref, *rest, nc, npb, zero_init):
    if zero_init:
        o_ref, sf_ref, sb_ref, of_scr, ob_scr = rest
        sf_ref[...] = jnp.zeros_like(sf_ref)
        sb_ref[...] = jnp.zeros_like(sb_ref)
    else:
        s0f_ref, s0b_ref, o_ref, sf_ref, sb_ref, of_scr, ob_scr = rest
        sf_ref[...] = s0f_ref[...]
        sb_ref[...] = s0b_ref[...]
    rc = RET_CHUNK
    lane = lax.broadcasted_iota(jnp.int32, (rc, LANES), 1)
    lo = lane < HEAD_DIM
    bd = _pair_block_diag((LANES, LANES))
    bd16 = bd.astype(BF16)

    def chunk(c, j, d, s_ref, dst):
        r = pl.ds(pl.multiple_of(c * rc, rc), rc)
        cols = slice(j * LANES, (j + 1) * LANES)
        qc = q_ref[r, cols]
        kc = k_ref[r, cols]
        vc = v_ref[r, cols]
        s = s_ref[j]
        a = _dot_nt(jnp.concatenate([jnp.where(lo, qc, 0.0), jnp.where(lo, 0.0, qc)], axis=0), kc)
        lhs = jnp.concatenate([a[:rc] * dm_ref[j, d, 0], a[rc:] * dm_ref[j, d, 1], qc * qd_ref[j, d]], axis=1)
        rhs = jnp.concatenate([jnp.where(lo, vc, 0.0), jnp.where(lo, 0.0, vc), s], axis=0)
        dst[r, cols] = _dot(lhs, rhs)
        kv = _dot((kc * kd_ref[j, d]).T, vc)
        s_ref[j] = s * cm_ref[j, d] + jnp.where(bd, kv, 0.0)

    def body(c, carry):
        for j in range(npb):
            chunk(c, j, 0, sf_ref, of_scr)
            chunk(nc - 1 - c, j, 1, sb_ref, ob_scr)
        return carry

    lax.fori_loop(0, nc, body, 0)

    rt = min(nc * rc, RET_READOUT_ROWS)

    def readout(c, carry):
        r = pl.ds(pl.multiple_of(c * rt, rt), rt)
        for j in range(npb):
            cols = slice(j * LANES, (j + 1) * LANES)
            o = of_scr[r, cols] + ob_scr[r, cols]
            mu = _head_sum(o, bd16) * (1.0 / HEAD_DIM)
            dlt = o - mu
            var = _head_sum(dlt * dlt, bd16) * (1.0 / HEAD_DIM)
            o_ref[r, cols] = (dlt * lax.rsqrt(var + EPS) * _silu(g_ref[r, cols])).astype(o_ref.dtype)
        return carry

    lax.fori_loop(0, nc * rc // rt, readout, 0)


def _retention(p, tabs, s0f, s0b, n_seq, seq_len, row0, npb):
    npair = RET_HEADS // 2
    ng = npair // npb
    rb0 = row0 // seq_len
    dm, qd, kd, cm = tabs
    w = npb * LANES

    def col(c0):
        return pl.BlockSpec((seq_len, w), lambda b, j: (rb0 + b, c0 + j))

    rc = RET_CHUNK
    tab_rows = pl.BlockSpec((npb, 2, rc, LANES), lambda b, j: (j, 0, 0, 0))
    tab_state = pl.BlockSpec((npb, 2, LANES, LANES), lambda b, j: (j, 0, 0, 0))
    st = pl.BlockSpec((None, npb, LANES, LANES), lambda b, j: (b, j, 0, 0))
    zero_init = s0f is None
    states = [] if zero_init else [s0f, s0b]
    return pl.pallas_call(
        functools.partial(_ret_kernel, nc=seq_len // rc, npb=npb, zero_init=zero_init),
        grid=(n_seq, ng),
        in_specs=[col(0), col(ng), col(2 * ng), col(3 * ng),
                  pl.BlockSpec((npb, 2, 2, rc, rc), lambda b, j: (j, 0, 0, 0, 0)),
                  tab_rows, tab_rows, tab_state] + [st] * len(states),
        out_specs=[pl.BlockSpec((seq_len, w), lambda b, j: (b, j)), st, st],
        out_shape=[jax.ShapeDtypeStruct((n_seq * seq_len, RET_QW), BF16),
                   jax.ShapeDtypeStruct((n_seq, npair, LANES, LANES), F32),
                   jax.ShapeDtypeStruct((n_seq, npair, LANES, LANES), F32)],
        scratch_shapes=[pltpu.VMEM((seq_len, w), F32), pltpu.VMEM((seq_len, w), F32)],
        compiler_params=_cparams(("arbitrary", "arbitrary")),
        name="retention",
    )(p, p, p, p, dm, qd, kd, cm, *states)


def _retention_tables(decay_f, decay_b):
    npair = RET_HEADS // 2
    rc = RET_CHUNK
    idx = jnp.arange(rc, dtype=F32)
    diff = idx[:, None] - idx[None, :]

    def one(decay, backward):
        lg = jax.nn.log_sigmoid(decay.astype(F32))
        dmask = jnp.where(diff >= 0, jnp.exp(lg[:, None, None] * jnp.maximum(diff, 0.0)), 0.0)
        q_dec = jnp.exp(lg[:, None] * (idx + 1.0))
        k_dec = jnp.exp(lg[:, None] * (rc - 1.0 - idx))
        c_dec = jnp.exp(lg * rc)
        if backward:
            dmask = jnp.swapaxes(dmask, 1, 2)
            q_dec = q_dec[:, ::-1]
            k_dec = k_dec[:, ::-1]
        return dmask, q_dec, k_dec, c_dec

    def lanes(t):
        t = t.reshape(npair, 2, rc)
        return jnp.repeat(jnp.swapaxes(t, 1, 2), HEAD_DIM, axis=2)

    parts = [one(decay_f, False), one(decay_b, True)]
    dm = jnp.stack([p[0].reshape(npair, 2, rc, rc) for p in parts], axis=1)
    qd = jnp.stack([lanes(p[1]) for p in parts], axis=1)
    kd = jnp.stack([lanes(p[2]) for p in parts], axis=1)
    bd = _pair_block_diag((LANES, LANES))
    cm = jnp.stack([jnp.where(bd[None], jnp.repeat(p[3].reshape(npair, 2), HEAD_DIM, axis=1)[:, :, None], 0.0)
                    for p in parts], axis=1)
    return dm, qd, kd, cm


def _pair_states(s):
    b = s.shape[0]
    s = s.astype(F32).reshape(b, RET_HEADS // 2, 2, HEAD_DIM, HEAD_DIM)
    z = jnp.zeros_like(s[:, :, 0])
    top = jnp.concatenate([s[:, :, 0], z], axis=-1)
    bot = jnp.concatenate([z, s[:, :, 1]], axis=-1)
    return jnp.concatenate([top, bot], axis=-2)


def _unpair_states(s):
    b = s.shape[0]
    h0 = s[:, :, :HEAD_DIM, :HEAD_DIM]
    h1 = s[:, :, HEAD_DIM:, HEAD_DIM:]
    return jnp.stack([h0, h1], axis=2).reshape(b, RET_HEADS, HEAD_DIM, HEAD_DIM)


LOG2E = 1.4426950408889634


def _kv_variants(k, v):
    lane = lax.broadcasted_iota(jnp.int32, k.shape, 1)
    lo = lane < HEAD_DIM
    ka = jnp.where(lo, k, 0.0)
    kb = jnp.where(lo, 0.0, k)
    va = jnp.where(lane == HEAD_DIM, 1.0, jnp.where(lo, v, 0.0))
    vb = jnp.where(lane == 0, 1.0, jnp.where(lo, 0.0, v))
    ks = (ka, pltpu.roll(ka, HEAD_DIM, 1), pltpu.roll(kb, HEAD_DIM, 1), kb)
    vs = (va, pltpu.roll(va, HEAD_DIM, 1), pltpu.roll(vb, HEAD_DIM, 1), vb)
    return [t.astype(BF16) for t in ks], [t.astype(BF16) for t in vs]


def _attn_core(q, k, v, sink_ref, o_ref, mask, rows=slice(None)):
    _attn_jobs([(q, k, v, rows)], sink_ref, o_ref, mask)


def _attn_jobs(jobs, sink_ref, o_ref, mask):
    tq = jobs[0][0].shape[0]
    lane = lax.broadcasted_iota(jnp.int32, (2 * tq, LANES), 1)
    first = lax.broadcasted_iota(jnp.int32, (2 * tq, 1), 0) < tq
    if mask is not None:
        mask = jnp.concatenate([mask, mask], axis=0)
    chains = []
    for q, k, v, rows in jobs:
        ks, vs = _kv_variants(k, v)
        for half in (0, 1):
            q2 = jnp.concatenate([q[:, (2 * half) * LANES:(2 * half + 1) * LANES],
                                  q[:, (2 * half + 1) * LANES:(2 * half + 2) * LANES]], axis=0)
            q2 = (q2 * LOG2E).astype(BF16)
            for hh in (0, 1):
                chains.append(dict(q2=q2, k=ks[2 * half + hh], v=vs[2 * half + hh], half=half, hh=hh, rows=rows))
    for c in chains:
        s = _dot_nt(c["q2"], c["k"])
        c["s"] = s if mask is None else jnp.where(mask, s, NEG_INF)
    for c in chains:
        m = jnp.max(c["s"], axis=-1, keepdims=True)
        if sink_ref is not None:
            half, hh = c["half"], c["hh"]
            c["snk"] = LOG2E * jnp.where(first, sink_ref[2 * half, hh:hh + 1, 0:1],
                                         sink_ref[2 * half + 1, hh:hh + 1, 0:1])
            m = jnp.maximum(m, c["snk"])
        c["m"] = m
    for c in chains:
        c["e"] = jnp.exp2((c["s"] - c["m"]).astype(BF16))
    for c in chains:
        c["acc"] = _dot(c["e"], c["v"])
    for i in range(0, len(chains), 2):
        out = None
        for c in chains[i:i + 2]:
            hh = c["hh"]
            ones_lane = HEAD_DIM if hh == 0 else 0
            den = c["acc"][:, ones_lane:ones_lane + 1]
            if sink_ref is not None:
                den = den + jnp.exp2(c["snk"] - c["m"])
            own = (lane < HEAD_DIM) if hh == 0 else (lane >= HEAD_DIM)
            o = jnp.where(own, c["acc"], 0.0) * (1.0 / den)
            out = o if out is None else out + o
        half, rows = chains[i]["half"], chains[i]["rows"]
        out = out.astype(o_ref.dtype)
        o_ref[rows, (2 * half) * LANES:(2 * half + 1) * LANES] = out[:tq]
        o_ref[rows, (2 * half + 1) * LANES:(2 * half + 2) * LANES] = out[tq:]


def _full_attn_kernel(q_ref, k_ref, v_ref, *rest, has_sink):
    if has_sink:
        sink_ref, o_ref = rest
    else:
        sink_ref, (o_ref,) = None, rest
    jobs = []
    for s in range(PROMPT_SEQS_PER_STEP):
        r = slice(s * SEQ, (s + 1) * SEQ)
        jobs.append((q_ref[r, :], k_ref[r, :], v_ref[r, :], r))
    _attn_jobs(jobs, sink_ref, o_ref, None)


PROMPT_SEQS_PER_STEP = 4


def _full_attention(p, q_cb0, k_cb0, v_cb0, n_groups, sink=None):
    rows = PROMPT_SEQS_PER_STEP * SEQ
    qw = 4 * LANES
    in_specs = [
        pl.BlockSpec((rows, qw), lambda b, g: (b, q_cb0 + g)),
        pl.BlockSpec((rows, LANES), lambda b, g: (b, k_cb0 + g)),
        pl.BlockSpec((rows, LANES), lambda b, g: (b, v_cb0 + g)),
    ]
    args = [p, p, p]
    if sink is not None:
        in_specs.append(pl.BlockSpec((None, 4, 2, LANES), lambda b, g: (g, 0, 0, 0)))
        args.append(sink)
    return pl.pallas_call(
        functools.partial(_full_attn_kernel, has_sink=sink is not None),
        grid=(T_PROMPT // rows, n_groups),
        in_specs=in_specs,
        out_specs=pl.BlockSpec((rows, qw), lambda b, g: (b, g)),
        out_shape=jax.ShapeDtypeStruct((T_PROMPT, n_groups * qw), BF16),
        compiler_params=_cparams(("arbitrary", "arbitrary")),
        name="full_attention",
    )(*args)


def _latent_attn_kernel(q_ref, k_ref, v_ref, kx_ref, vx_ref, o_ref, kvar, vvar, *, tq, kb):
    lk = DEC_SEQ + PAST_LEN

    @pl.when(pl.program_id(2) == 0)
    def _():
        for src_k, src_v, r0, nrows in ((k_ref, v_ref, 0, DEC_SEQ), (kx_ref, vx_ref, DEC_SEQ, PAST_LEN)):
            for c in range(nrows // kb):
                src = pl.ds(c * kb, kb)
                dst = pl.ds(r0 + c * kb, kb)
                ks, vs = _kv_variants(src_k[src, :], src_v[src, :])
                for i in range(4):
                    kvar[i, dst, :] = ks[i]
                    vvar[i, dst, :] = vs[i]

    lane = lax.broadcasted_iota(jnp.int32, (2 * tq, LANES), 1)
    for half in (0, 1):
        q2 = jnp.concatenate([q_ref[:, (2 * half) * LANES:(2 * half + 1) * LANES],
                              q_ref[:, (2 * half + 1) * LANES:(2 * half + 2) * LANES]], axis=0)
        q2 = (q2 * LOG2E).astype(BF16)
        out = None
        for hh in (0, 1):
            var = 2 * half + hh
            m = jnp.full((2 * tq, 1), -jnp.inf, F32)
            acc = jnp.zeros((2 * tq, LANES), F32)
            for j in range(lk // kb):
                rows = pl.ds(j * kb, kb)
                s = _dot_nt(q2, kvar[var, rows, :])
                m_new = jnp.maximum(m, jnp.max(s, axis=-1, keepdims=True))
                e = jnp.exp2((s - m_new).astype(BF16))
                acc = jnp.exp2(m - m_new) * acc + _dot(e, vvar[var, rows, :])
                m = m_new
            own = (lane < HEAD_DIM) if hh == 0 else (lane >= HEAD_DIM)
            ones_lane = HEAD_DIM if hh == 0 else 0
            o = jnp.where(own, acc, 0.0) * (1.0 / acc[:, ones_lane:ones_lane + 1])
            out = o if out is None else out + o
        out = out.astype(o_ref.dtype)
        o_ref[:, (2 * half) * LANES:(2 * half + 1) * LANES] = out[:tq]
        o_ref[:, (2 * half + 1) * LANES:(2 * half + 2) * LANES] = out[tq:]


def _latent_attention(p, kx, vx):
    tq, kb = 512, 512
    nq = DEC_SEQ // tq
    ngr = ATT_KV_HEADS // 2
    kcb = ATT_QW // LANES
    rb_q = T_PROMPT // tq
    rb_k = T_PROMPT // DEC_SEQ
    lk = DEC_SEQ + PAST_LEN
    ctx = pl.BlockSpec((None, PAST_LEN, LANES), lambda b, g, i: (b, 0, g))
    return pl.pallas_call(
        functools.partial(_latent_attn_kernel, tq=tq, kb=kb),
        grid=(DEC_BATCH, ngr, nq),
        in_specs=[pl.BlockSpec((tq, 4 * LANES), lambda b, g, i: (rb_q + b * nq + i, g)),
                  pl.BlockSpec((DEC_SEQ, LANES), lambda b, g, i: (rb_k + b, kcb + g)),
                  pl.BlockSpec((DEC_SEQ, LANES), lambda b, g, i: (rb_k + b, kcb + ngr + g)),
                  ctx, ctx],
        out_specs=pl.BlockSpec((tq, 4 * LANES), lambda b, g, i: (b * nq + i, g)),
        out_shape=jax.ShapeDtypeStruct((T_SAMPLE, ATT_QW), BF16),
        scratch_shapes=[pltpu.VMEM((4, lk, LANES), BF16), pltpu.VMEM((4, lk, LANES), BF16)],
        compiler_params=_cparams(("arbitrary", "arbitrary", "arbitrary")),
        name="latent_attention",
    )(p, p, p, kx, vx)


WIN_TQ = 2 * BLK
WIN_KBLKS = WIN_TQ // BLK + 2


def _window_attn_kernel(q_ref, *refs):
    k_refs = refs[:WIN_KBLKS]
    v_refs = refs[WIN_KBLKS:2 * WIN_KBLKS]
    kx_ref, vx_ref, sink_ref, o_ref = refs[2 * WIN_KBLKS:]
    n = pl.program_id(1)
    k = jnp.concatenate([r[...] for r in k_refs] + [kx_ref[...]], axis=0)
    v = jnp.concatenate([r[...] for r in v_refs] + [vx_ref[...]], axis=0)
    nloc = WIN_KBLKS * BLK
    shape = (WIN_TQ, nloc + PAST_LEN)
    i = lax.broadcasted_iota(jnp.int32, shape, 0)
    r = lax.broadcasted_iota(jnp.int32, shape, 1)
    kpos = n * WIN_TQ - BLK + r
    local = (jnp.abs(r - BLK - i) <= WINDOW) & (kpos >= 0) & (kpos < DEC_SEQ)
    mask = local | (r >= nloc)
    _attn_core(q_ref[...], k, v, sink_ref, o_ref, mask)


def _window_attention(p, kx, vx, sink):
    nb = DEC_SEQ // BLK
    nq = DEC_SEQ // WIN_TQ
    rb0 = T_PROMPT // BLK
    qcb = (4 * RET_QW) // (4 * LANES)
    kcb = (4 * RET_QW + WIN_QW) // LANES
    vcb = kcb + 1

    def kv(cb, off):
        return pl.BlockSpec((BLK, LANES),
                            lambda b, n: (rb0 + b * nb + jnp.clip(n * (WIN_TQ // BLK) + off, 0, nb - 1), cb))

    offs = range(-1, WIN_KBLKS - 1)
    ctx = pl.BlockSpec((None, PAST_LEN, LANES), lambda b, n: (b, 0, 0))
    qspec = pl.BlockSpec((WIN_TQ, 4 * LANES), lambda b, n: (T_PROMPT // WIN_TQ + b * nq + n, qcb))
    return pl.pallas_call(
        _window_attn_kernel,
        grid=(DEC_BATCH, nq),
        in_specs=[qspec] + [kv(kcb, o) for o in offs] + [kv(vcb, o) for o in offs] + [
            ctx, ctx, pl.BlockSpec((4, 2, LANES), lambda b, n: (0, 0, 0))],
        out_specs=pl.BlockSpec((WIN_TQ, 4 * LANES), lambda b, n: (b * nq + n, 0)),
        out_shape=jax.ShapeDtypeStruct((T_SAMPLE, WIN_QW), BF16),
        compiler_params=_cparams(("arbitrary", "arbitrary")),
        name="window_attention",
    )(*([p] * (1 + 2 * WIN_KBLKS)), kx, vx, sink)


def _sink_table(sink, n_groups):
    s = sink.astype(F32).reshape(n_groups, 4, 2, 1)
    return jnp.broadcast_to(s, (n_groups, 4, 2, LANES))


def _cache_rows(cache):
    b, h, l, d = cache.shape
    return cache.astype(F32).transpose(0, 2, 1, 3).reshape(b, l, h * d)


def _outproj_kernel(*refs, n_in):
    a_refs = refs[:2 * n_in]
    w_refs = refs[2 * n_in:3 * n_in]
    xp_ref, xs_ref, mod_ref, g_ref, wr_ref, xn_ref, hfp_ref, hfs_ref, lg_ref = refs[3 * n_in:]
    i = pl.program_id(0)
    y = None
    for k, w_ref in enumerate(w_refs):
        a = _pick(a_refs[2 * k], a_refs[2 * k + 1])
        t = _dot(a.astype(BF16), w_ref[...])
        y = t if y is None else y + t
    xn = _pick(xp_ref, xs_ref) + mod_ref[2:3, :] * _rms(y, g_ref[1:2, :])
    hf = _rms(xn, g_ref[2:3, :]) * (1.0 + mod_ref[4:5, :]) + mod_ref[3:4, :]
    xn_ref[...] = xn
    lg_ref[...] = _dot_nt(wr_ref[...], hf.astype(BF16))

    @pl.when(i < NPT)
    def _():
        hfp_ref[...] = hf.astype(hfp_ref.dtype)

    @pl.when(i >= NPT)
    def _():
        hfs_ref[...] = hf.reshape(TM, D_MODEL // LANES, LANES)


def _outproj(parts, w_parts, x, mod, layer, gains, w_router_t):
    n_in = len(parts)
    in_specs = []
    args = []
    for a_p, a_s in parts:
        in_specs += _pair_specs(a_p.shape[1])
        args += [a_p, a_s]
    in_specs += [pl.BlockSpec(w.shape, lambda i: (0, 0)) for w in w_parts]
    in_specs += _pair_specs(D_MODEL) + [
        pl.BlockSpec((None, None, 6, D_MODEL), lambda i: (layer, _mod_index(i), 0, 0)),
        pl.BlockSpec((4, D_MODEL), lambda i: (0, 0)),
        pl.BlockSpec((N_EXPERTS, D_MODEL), lambda i: (0, 0)),
    ]
    row = pl.BlockSpec((TM, D_MODEL), lambda i: (i, 0))
    hf_p, _ = _pair_specs(D_MODEL)
    hf_s = pl.BlockSpec((TM, D_MODEL // LANES, LANES), lambda i: (jnp.maximum(i - NPT, 0), 0, 0))
    return pl.pallas_call(
        functools.partial(_outproj_kernel, n_in=n_in),
        grid=(T_ALL // TM,),
        in_specs=in_specs,
        out_specs=[row, hf_p, hf_s, pl.BlockSpec((N_EXPERTS, TM), lambda i: (0, i))],
        out_shape=[jax.ShapeDtypeStruct((T_ALL, D_MODEL), F32),
                   jax.ShapeDtypeStruct((T_PROMPT, D_MODEL), BF16),
                   jax.ShapeDtypeStruct((T_SAMPLE, D_MODEL // LANES, LANES), F32),
                   jax.ShapeDtypeStruct((N_EXPERTS, T_ALL), F32)],
        compiler_params=_cparams(("arbitrary",)),
        name="outproj",
    )(*args, *w_parts, x[0], x[1], mod, gains, w_router_t)


TILE_SUB = D_MODEL // LANES


def _gather_kernel(idx_ref, h_ref, o_ref, buf, *, eg, cap):
    b = pl.program_id(0)
    g = pl.program_id(1)
    for e in range(eg):
        base = (b * N_EXPERTS + g * eg + e) * cap

        def body(c, carry):
            buf[c] = h_ref[idx_ref[base + c]]
            return carry

        lax.fori_loop(0, cap, body, 0, unroll=8)
        o_ref[e] = buf[...].reshape(cap, D_MODEL).astype(BF16)


def _gather(h3, idx, n_seq, seq_len, cap, eg):
    return pl.pallas_call(
        functools.partial(_gather_kernel, eg=eg, cap=cap),
        grid_spec=pltpu.PrefetchScalarGridSpec(
            num_scalar_prefetch=1,
            grid=(n_seq, N_EXPERTS // eg),
            in_specs=[pl.BlockSpec((seq_len, TILE_SUB, LANES), lambda b, g, idx: (b, 0, 0))],
            out_specs=pl.BlockSpec((eg, cap, D_MODEL), lambda b, g, idx: (g, b, 0)),
            scratch_shapes=[pltpu.VMEM((cap, TILE_SUB, LANES), F32)],
        ),
        out_shape=jax.ShapeDtypeStruct((N_EXPERTS, n_seq * cap, D_MODEL), BF16),
        compiler_params=_cparams(("arbitrary", "arbitrary")),
        name="moe_gather",
    )(idx, h3)


def _moe_kernel(xp_ref, xs_ref, wg_ref, wu_ref, wd_ref, gwp_ref, gws_ref, yp_ref, ys_ref, hid_scr, wd_scr):
    f = pl.program_id(1)
    wg = wg_ref[...].astype(BF16)
    wu = wu_ref[...].astype(BF16)
    cols = pl.ds(pl.multiple_of(f * TF, TF), TF)
    wd_scr[cols, :] = wd_ref[...].astype(BF16)
    for h, x_ref in enumerate((xp_ref, xs_ref)):
        x = x_ref[...]
        hid_scr[h, :, cols] = (_silu(_dot(x, wg)) * _dot(x, wu)).astype(BF16)

    @pl.when(f == pl.num_programs(1) - 1)
    def _():
        for h, (gw_ref, y_ref) in enumerate(((gwp_ref, yp_ref), (gws_ref, ys_ref))):
            gw = jnp.broadcast_to(gw_ref[...], (LANES, hid_scr.shape[1])).T[:, 0:1]
            for r in range(0, hid_scr.shape[1], MOE_ROW_TILE):
                rows = slice(r, r + MOE_ROW_TILE)
                y_ref[rows, :] = _dot(hid_scr[h, rows, :], wd_scr[...]) * gw[rows, :]


MOE_ROW_TILE = 512


def _moe_ffn(xg_p, xg_s, w_gate, w_up, w_down, layer, gw_p, gw_s):
    assert ROWS_P == ROWS_S
    nf = EXPERT_FF // TF
    xspec_p = pl.BlockSpec((None, ROWS_P, D_MODEL), lambda e, f: (e, 0, 0))
    xspec_s = pl.BlockSpec((None, ROWS_S, D_MODEL), lambda e, f: (e, 0, 0))
    return pl.pallas_call(
        _moe_kernel,
        grid=(N_EXPERTS, nf),
        in_specs=[xspec_p, xspec_s,
                  pl.BlockSpec((None, None, D_MODEL, TF), lambda e, f: (layer, e, 0, f)),
                  pl.BlockSpec((None, None, D_MODEL, TF), lambda e, f: (layer, e, 0, f)),
                  pl.BlockSpec((None, None, TF, D_MODEL), lambda e, f: (layer, e, f, 0)),
                  pl.BlockSpec((None, 1, ROWS_P), lambda e, f: (e, 0, 0)),
                  pl.BlockSpec((None, 1, ROWS_S), lambda e, f: (e, 0, 0))],
        out_specs=[xspec_p, xspec_s],
        out_shape=[jax.ShapeDtypeStruct((N_EXPERTS, ROWS_P, D_MODEL), F32),
                   jax.ShapeDtypeStruct((N_EXPERTS, ROWS_S, D_MODEL), F32)],
        scratch_shapes=[pltpu.VMEM((2, ROWS_P, EXPERT_FF), BF16),
                        pltpu.VMEM((EXPERT_FF, D_MODEL), BF16)],
        compiler_params=_cparams(("arbitrary", "arbitrary")),
        name="moe_ffn",
    )(xg_p, xg_s, w_gate, w_up, w_down, gw_p, gw_s)


COMBINE_GROUP = 8


def _combine_kernel(idx_ref, y_ref, o_ref, y3, *, eg, cap):
    b = pl.program_id(0)
    g = pl.program_id(1)

    @pl.when(g == 0)
    def _():
        o_ref[...] = jnp.zeros_like(o_ref)

    for e in range(eg):
        base = (b * N_EXPERTS + g * eg + e) * cap
        y3[...] = y_ref[e].reshape(cap, TILE_SUB, LANES)

        def body(c, carry):
            c0 = c * COMBINE_GROUP
            toks = [idx_ref[base + c0 + k] for k in range(COMBINE_GROUP)]
            rows = [o_ref[t] + y3[c0 + k] for k, t in enumerate(toks)]
            for t, row in zip(toks, rows):
                o_ref[t] = row
            return carry

        lax.fori_loop(0, cap // COMBINE_GROUP, body, 0)


def _combine(y, idx, n_seq, seq_len, cap, eg):
    return pl.pallas_call(
        functools.partial(_combine_kernel, eg=eg, cap=cap),
        grid_spec=pltpu.PrefetchScalarGridSpec(
            num_scalar_prefetch=1,
            grid=(n_seq, N_EXPERTS // eg),
            in_specs=[pl.BlockSpec((eg, cap, D_MODEL), lambda b, g, idx: (g, b, 0))],
            out_specs=pl.BlockSpec((seq_len, TILE_SUB, LANES), lambda b, g, idx: (b, 0, 0)),
            scratch_shapes=[pltpu.VMEM((cap, TILE_SUB, LANES), F32)],
        ),
        out_shape=jax.ShapeDtypeStruct((n_seq * seq_len, TILE_SUB, LANES), F32),
        compiler_params=_cparams(("arbitrary", "arbitrary")),
        name="moe_combine",
    )(idx, y)


def _onehot_t(idx_row, n_tok):
    tok = lax.broadcasted_iota(jnp.int32, (n_tok, idx_row.shape[-1]), 0)
    return jnp.where(tok == idx_row, 1.0, 0.0)


def _gather_mm_kernel(idx_ref, h_ref, o_ref, *, cap, seq_len):
    for s in range(PROMPT_SEQS_PER_STEP):
        sel = _onehot_t(idx_ref[s], seq_len).T.astype(BF16)
        h = h_ref[s * seq_len:(s + 1) * seq_len, :].astype(BF16)
        rows = _dot(sel, h).astype(BF16)
        for e in range(N_EXPERTS):
            o_ref[e, s * cap:(s + 1) * cap, :] = rows[e * cap:(e + 1) * cap]


def _gather_mm(h, idx, n_seq, seq_len, cap):
    ns = PROMPT_SEQS_PER_STEP
    return pl.pallas_call(
        functools.partial(_gather_mm_kernel, cap=cap, seq_len=seq_len),
        grid=(n_seq // ns,),
        in_specs=[pl.BlockSpec((ns, 1, N_EXPERTS * cap), lambda b: (b, 0, 0)),
                  pl.BlockSpec((ns * seq_len, D_MODEL), lambda b: (b, 0))],
        out_specs=pl.BlockSpec((N_EXPERTS, ns * cap, D_MODEL), lambda b: (0, b, 0)),
        out_shape=jax.ShapeDtypeStruct((N_EXPERTS, n_seq * cap, D_MODEL), BF16),
        compiler_params=_cparams(("arbitrary",)),
        name="moe_gather_mm",
    )(idx, h)


def _combine_mm_kernel(idx_ref, y_ref, o_ref, *, cap, seq_len):
    for s in range(PROMPT_SEQS_PER_STEP):
        sel = _onehot_t(idx_ref[s], seq_len).astype(BF16)
        y = jnp.concatenate([y_ref[e, s * cap:(s + 1) * cap, :] for e in range(N_EXPERTS)], axis=0)
        hi = y.astype(BF16)
        lo = (y - hi.astype(F32)).astype(BF16)
        o_ref[s * seq_len:(s + 1) * seq_len, :] = _dot(sel, hi) + _dot(sel, lo)


def _combine_mm(y, idx, n_seq, seq_len, cap):
    ns = PROMPT_SEQS_PER_STEP
    return pl.pallas_call(
        functools.partial(_combine_mm_kernel, cap=cap, seq_len=seq_len),
        grid=(n_seq // ns,),
        in_specs=[pl.BlockSpec((ns, 1, N_EXPERTS * cap), lambda b: (b, 0, 0)),
                  pl.BlockSpec((N_EXPERTS, ns * cap, D_MODEL), lambda b: (0, b, 0))],
        out_specs=pl.BlockSpec((ns * seq_len, D_MODEL), lambda b: (b, 0)),
        out_shape=jax.ShapeDtypeStruct((n_seq * seq_len, D_MODEL), F32),
        compiler_params=_cparams(("arbitrary",)),
        name="moe_combine_mm",
    )(idx, y)


def _postnorm_kernel(x_ref, yp_ref, ys_ref, mod_ref, g_ref, op_ref, os_ref):
    i = pl.program_id(0)
    y = jnp.where(i < NPT, yp_ref[...], ys_ref[...].reshape(TM, D_MODEL))
    o = x_ref[...] + mod_ref[5:6, :] * _rms(y, g_ref[3:4, :])

    @pl.when(i < NPT)
    def _():
        op_ref[...] = o

    @pl.when(i >= NPT)
    def _():
        os_ref[...] = o


def _postnorm(x, y_p, y_s, mod, layer, gains):
    return pl.pallas_call(
        _postnorm_kernel,
        grid=(T_ALL // TM,),
        in_specs=[pl.BlockSpec((TM, D_MODEL), lambda i: (i, 0)), _pair_specs(D_MODEL)[0],
                  pl.BlockSpec((TM, TILE_SUB, LANES), lambda i: (jnp.maximum(i - NPT, 0), 0, 0)),
                  pl.BlockSpec((None, None, 6, D_MODEL), lambda i: (layer, _mod_index(i), 0, 0)),
            pl.BlockSpec((4, D_MODEL), lambda i: (0, 0))],
        out_specs=_pair_specs(D_MODEL),
        out_shape=[jax.ShapeDtypeStruct((T_PROMPT, D_MODEL), F32),
                   jax.ShapeDtypeStruct((T_SAMPLE, D_MODEL), F32)],
        compiler_params=_cparams(("arbitrary",)),
        name="postnorm",
    )(x, y_p, y_s, mod, gains)


def _excl_prefix(mask):
    r, n = mask.shape
    i0 = lax.broadcasted_iota(jnp.int32, (LANES, LANES), 0)
    i1 = lax.broadcasted_iota(jnp.int32, (LANES, LANES), 1)
    upper = jnp.where(i0 < i1, 1.0, 0.0).astype(BF16)
    ones = jnp.where(mask, 1.0, 0.0)
    carry = jnp.zeros((r, 1), F32)
    out = []
    for c in range(n // LANES):
        ch = ones[:, c * LANES:(c + 1) * LANES]
        out.append(_dot(ch.astype(BF16), upper) + carry)
        carry = carry + jnp.sum(ch, axis=1, keepdims=True)
    return jnp.concatenate(out, axis=1)


COMPACT_SLOTS = 256


def _compact_mxu(pos, aff, idx_ref, gw_ref, n, cap):
    sub = lax.broadcasted_iota(jnp.int32, (8, n), 0)
    tok = lax.broadcasted_iota(jnp.int32, (8, n), 1)
    base = jnp.where(sub == 0, tok >> 6, jnp.where(sub == 1, tok & 63, 0)).astype(F32)
    pad = jnp.zeros((LANES - 8, n), F32)
    slot0 = lax.broadcasted_iota(jnp.int32, (COMPACT_SLOTS, n), 0).astype(F32)
    for e in range(N_EXPERTS):
        a = aff[e:e + 1, :]
        a_hi = a.astype(BF16).astype(F32)
        rest = a - a_hi
        a_mid = rest.astype(BF16).astype(F32)
        a_lo = rest - a_mid
        pay = base + jnp.where(sub == 2, a_hi, 0.0) + jnp.where(sub == 3, a_mid, 0.0) + jnp.where(sub == 4, a_lo, 0.0)
        pay = jnp.concatenate([pay, pad], axis=0).astype(BF16)
        pos_e = pos[e:e + 1, :]
        for c in range(cap // COMPACT_SLOTS):
            onehot = jnp.where(pos_e == slot0 + float(c * COMPACT_SLOTS), 1.0, 0.0).astype(BF16)
            got = _dot_nt(onehot, pay).T
            cols = slice(c * COMPACT_SLOTS, (c + 1) * COMPACT_SLOTS)
            idx_ref[0, e:e + 1, cols] = (got[0:1] * 64.0 + got[1:2]).astype(jnp.int32)
            gw_ref[0, e:e + 1, cols] = got[2:3] + got[3:4] + got[4:5]


def _route_kernel(lg_ref, idx_ref, gw_ref, *, nb, n, cap):
    affs = []
    for s in range(nb):
        l = lg_ref[:, s * n:(s + 1) * n]
        e = jnp.exp(l - jnp.max(l, axis=0, keepdims=True))
        affs.append(e / jnp.sum(e, axis=0, keepdims=True))
    aff = affs[0] if nb == 1 else jnp.concatenate(affs, axis=0)
    rows = nb * N_EXPERTS
    bits = jnp.zeros((rows, 1), jnp.int32)
    for bit in range(30, -1, -1):
        cand = bits | (1 << bit)
        cnt = jnp.sum(jnp.where(aff >= pltpu.bitcast(cand, F32), 1.0, 0.0), axis=1, keepdims=True)
        bits = jnp.where(cnt >= cap, cand, bits)
    thr = pltpu.bitcast(bits, F32)
    gt = aff > thr
    eq = aff == thr
    need = cap - jnp.sum(jnp.where(gt, 1.0, 0.0), axis=1, keepdims=True)
    sel = gt | (eq & (_excl_prefix(eq) < need))
    pos = jnp.where(sel, _excl_prefix(sel), -1.0)
    if cap > LANES:
        _compact_mxu(pos, aff, idx_ref, gw_ref, n, cap)
        return
    tok = lax.broadcasted_iota(jnp.int32, (rows, n), 1).astype(F32)
    capw = max(cap, LANES)
    slot_lane = lax.broadcasted_iota(jnp.int32, (rows, capw), 1)

    def slot(s, carry):
        idx_acc, gw_acc = carry
        hit = pos == lax.convert_element_type(s, F32)
        ic = jnp.sum(jnp.where(hit, tok, 0.0), axis=1, keepdims=True)
        gc = jnp.sum(jnp.where(hit, aff, 0.0), axis=1, keepdims=True)
        here = slot_lane == s
        return jnp.where(here, ic, idx_acc), jnp.where(here, gc, gw_acc)

    zero = jnp.zeros((rows, capw), F32)
    idx_acc, gw_acc = lax.fori_loop(0, cap, slot, (zero, zero), unroll=8)
    for s in range(nb):
        r = slice(s * N_EXPERTS, (s + 1) * N_EXPERTS)
        idx_ref[s] = idx_acc[r, :cap].astype(jnp.int32)
        gw_ref[s] = gw_acc[r, :cap]


def _route(logits_t, n_seq, seq_len, cap, col0, nb):
    cb0 = col0 // (nb * seq_len)
    out = pl.BlockSpec((nb, N_EXPERTS, cap), lambda b: (b, 0, 0))
    return pl.pallas_call(
        functools.partial(_route_kernel, nb=nb, n=seq_len, cap=cap),
        grid=(n_seq // nb,),
        in_specs=[pl.BlockSpec((N_EXPERTS, nb * seq_len), lambda b: (0, cb0 + b))],
        out_specs=[out, out],
        out_shape=[jax.ShapeDtypeStruct((n_seq, N_EXPERTS, cap), jnp.int32),
                   jax.ShapeDtypeStruct((n_seq, N_EXPERTS, cap), F32)],
        compiler_params=_cparams(("arbitrary",)),
        name="route",
    )(logits_t)


def _expert_ffn(hf_p, hf_s, logits_t, w_gate, w_up, w_down, layer):
    idx_p, gw_p = _route(logits_t, BATCH, SEQ, CAP_P, 0, 8)
    idx_s, gw_s = _route(logits_t, DEC_BATCH, DEC_SEQ, CAP_S, T_PROMPT, 1)
    idx_p = idx_p.reshape(BATCH, 1, N_EXPERTS * CAP_P)
    idx_s = idx_s.reshape(DEC_BATCH * N_EXPERTS * CAP_S)
    xg_p = _gather_mm(hf_p, idx_p, BATCH, SEQ, CAP_P)
    xg_s = _gather(hf_s, idx_s, DEC_BATCH, DEC_SEQ, CAP_S, 4)
    gw_p = gw_p.transpose(1, 0, 2).reshape(N_EXPERTS, 1, ROWS_P)
    gw_s = gw_s.transpose(1, 0, 2).reshape(N_EXPERTS, 1, ROWS_S)
    y_p, y_s = _moe_ffn(xg_p, xg_s, w_gate, w_up, w_down, layer, gw_p, gw_s)
    o_p = _combine_mm(y_p, idx_p, BATCH, SEQ, CAP_P)
    o_s = _combine(y_s, idx_s, DEC_BATCH, DEC_SEQ, CAP_S, 2)
    return o_p, o_s


def _rope_tables():
    t = np.arange(DEC_SEQ)
    row = (t // GRID_W).astype(np.float32)
    col = (t % GRID_W).astype(np.float32)
    inv = (ROPE_THETA ** (-np.arange(0, AXIS_DIM, 2, dtype=np.float32) / AXIS_DIM)).astype(np.float32)
    ar = row[:, None] * inv[None]
    ac = col[:, None] * inv[None]
    ang = np.concatenate([ar, ar, ac, ac], axis=-1)
    ang = np.concatenate([ang, ang], axis=-1)
    cos = np.concatenate([np.ones((TM, LANES), np.float32), np.cos(ang)], axis=0)
    sin = np.concatenate([np.zeros((TM, LANES), np.float32), np.sin(ang)], axis=0)
    return jnp.asarray(cos, F32), jnp.asarray(sin, F32)


def kernel(x_prompt, x_sample, state_ret_fwd, state_ret_bwd, cache_win_k, cache_win_v, cache_attn_k, cache_attn_v, c, c_ctx, w_ada, b_ada, norm_gains, w_in_even, w_out_even, ret_decay_fwd, ret_decay_bwd, win_sink, w_in_odd, w_out_odd, q_norm, k_norm, w_router, w_gate, w_up, w_down):
    x = (x_prompt.reshape(T_PROMPT, D_MODEL), x_sample.reshape(T_SAMPLE, D_MODEL))
    cond =jnp.concatenate([c_ctx[None, :], c, jnp.zeros((8 - 1 - DEC_BATCH, D_MODEL), F32)], axis=0)
    mod = _modulation(cond, w_ada, b_ada).reshape(DEPTH, 8, 6, D_MODEL)
    cos_t, sin_t = _rope_tables()
    outs = {}
    for layer in range(DEPTH):
        j = layer // 2
        gains = norm_gains[layer]
        if layer % 2 == 0:
            p, outs["win_k"], outs["win_v"] = _inproj(x, mod, layer, gains[0], w_in_even[j].astype(BF16),
                                                      cos_t, sin_t)
            tabs = _retention_tables(ret_decay_fwd[j], ret_decay_bwd[j])
            ret_p, sf, sb = _retention(p, tabs, None, None, BATCH, SEQ, 0, 4)
            ret_s, _, _ = _retention(p, tabs, _pair_states(state_ret_fwd[:, j]),
                                     _pair_states(state_ret_bwd[:, j]), DEC_BATCH, DEC_SEQ, T_PROMPT, 2)
            sink = _sink_table(win_sink[j], 1)
            qcb = (4 * RET_QW) // (4 * LANES)
            kcb = (4 * RET_QW + WIN_QW) // LANES
            win_p = _full_attention(p, qcb, kcb, kcb + 1, 1, sink=sink)
            win_s = _window_attention(p, _cache_rows(cache_win_k[:, j]), _cache_rows(cache_win_v[:, j]),
                                      sink[0])
            w_out = w_out_even[j].astype(BF16)
            parts = [(ret_p, ret_s), (win_p, win_s)]
            w_parts = [w_out[:RET_QW], w_out[RET_QW:]]
            outs["ret_f"] = _unpair_states(sf)[:, None]
            outs["ret_b"] = _unpair_states(sb)[:, None]
        else:
            p, outs["att_k"], outs["att_v"] = _inproj(x, mod, layer, gains[0], w_in_odd[j].astype(BF16),
                                                      cos_t, sin_t, qn=q_norm[j], kn=k_norm[j])
            ngr = ATT_KV_HEADS // 2
            kcb = ATT_QW // LANES
            att_p = _full_attention(p, 0, kcb, kcb + ngr, ngr)
            att_s = _latent_attention(p, _cache_rows(cache_attn_k[:, j]), _cache_rows(cache_attn_v[:, j]))
            parts = [(att_p, att_s)]
            w_parts = [w_out_odd[j].astype(BF16)]
        xn, hf_p, hf_s, logits_t = _outproj(parts, w_parts, x, mod, layer, gains,
                                            w_router[layer].T.astype(BF16))
        o_p, o_s = _expert_ffn(hf_p, hf_s, logits_t, w_gate, w_up, w_down, layer)
        x = _postnorm(xn, o_p, o_s, mod, layer, gains)
    y_prompt = x[0].reshape(BATCH, SEQ, D_MODEL)
    y_sample = x[1].reshape(DEC_BATCH, DEC_SEQ, D_MODEL)
    return (y_prompt, y_sample, outs["ret_f"], outs["ret_b"], outs["win_k"], outs["win_v"],
            outs["att_k"], outs["att_v"])
```

```python
import functools

import jax
import jax.numpy as jnp
import numpy as np
from jax import lax
from jax.experimental import pallas as pl
from jax.experimental.pallas import tpu as pltpu

D_MODEL = 1024
BATCH = 32
SEQ = 256
DEPTH = 2
DEC_BATCH = 2
DEC_SEQ = 4096
PAST_LEN = 512
GRID_W = 64
HEAD_DIM = 64
AXIS_DIM = HEAD_DIM // 2
ROPE_THETA = 10000.0
BLK = 128
WINDOW = 128
RET_HEADS = D_MODEL // 128
WIN_HEADS = D_MODEL // 128
WIN_KV_HEADS = WIN_HEADS // 4
ATT_HEADS = D_MODEL // HEAD_DIM
ATT_KV_HEADS = ATT_HEADS // 4
RET_QW = RET_HEADS * HEAD_DIM
WIN_QW = WIN_HEADS * HEAD_DIM
WIN_KW = WIN_KV_HEADS * HEAD_DIM
EVEN_IN = 4 * RET_QW + WIN_QW + 2 * WIN_KW
ATT_QW = ATT_HEADS * HEAD_DIM
ATT_KW = ATT_KV_HEADS * HEAD_DIM
ODD_IN = ATT_QW + 2 * ATT_KW
N_EXPERTS = 16
CAPACITY_FACTOR = 2
EXPERT_FF = ((8 * D_MODEL // 3 + 127) // 128) * 128
EPS = 1e-6
NEG_INF = -1e30
F32 = jnp.float32
BF16 = jnp.bfloat16

LANES = 128
T_PROMPT = BATCH * SEQ
T_SAMPLE = DEC_BATCH * DEC_SEQ
T_ALL = T_PROMPT + T_SAMPLE
TM = 512
CAP_P = CAPACITY_FACTOR * SEQ // N_EXPERTS
CAP_S = CAPACITY_FACTOR * DEC_SEQ // N_EXPERTS
ROWS_P = BATCH * CAP_P
ROWS_S = DEC_BATCH * CAP_S
TF = 256
VMEM_LIMIT = 56 * 1024 * 1024
QK_SCALE = HEAD_DIM ** -0.5


def _cparams(sem):
    return pltpu.CompilerParams(dimension_semantics=sem, vmem_limit_bytes=VMEM_LIMIT)


def _silu(x):
    return x * (1.0 / (1.0 + jnp.exp(-x)))


def _dot(a, b):
    return jnp.dot(a, b, preferred_element_type=F32)


def _dot_nt(a, b):
    return lax.dot_general(a, b, (((1,), (1,)), ((), ())), preferred_element_type=F32)


def _rms(x, g):
    return x * lax.rsqrt(jnp.mean(x * x, axis=-1, keepdims=True) + EPS) * g


def _pair_block_diag(shape):
    r = lax.broadcasted_iota(jnp.int32, shape, 0)
    c = lax.broadcasted_iota(jnp.int32, shape, 1)
    return (r // HEAD_DIM) == (c // HEAD_DIM)


def _head_sum(x, bd):
    hi = x.astype(BF16)
    lo = (x - hi.astype(F32)).astype(BF16)
    return _dot(hi, bd) + _dot(lo, bd)


def _rope(xb, cos, sin):
    lane = lax.broadcasted_iota(jnp.int32, xb.shape, 1)
    half = AXIS_DIM // 2
    rot = jnp.where((lane % AXIS_DIM) < half,
                    -pltpu.roll(xb, LANES - half, 1), pltpu.roll(xb, half, 1))
    return xb * cos + rot * sin


def _mod_kernel(c_ref, w_ref, b_ref, o_ref):
    o_ref[...] = _dot(_silu(c_ref[...]), w_ref[...]) + b_ref[...]


def _modulation(cond, w_ada, b_ada):
    tn = 1536
    n = 6 * D_MODEL
    return pl.pallas_call(
        _mod_kernel,
        grid=(DEPTH, n // tn),
        in_specs=[
            pl.BlockSpec((8, D_MODEL), lambda l, j: (0, 0)),
            pl.BlockSpec((None, D_MODEL, tn), lambda l, j: (l, 0, j)),
            pl.BlockSpec((None, 1, tn), lambda l, j: (l, 0, j)),
        ],
        out_specs=pl.BlockSpec((None, 8, tn), lambda l, j: (l, 0, j)),
        out_shape=jax.ShapeDtypeStruct((DEPTH, 8, n), F32),
        compiler_params=_cparams(("arbitrary", "arbitrary")),
        name="modulation",
    )(cond, w_ada, b_ada.reshape(DEPTH, 1, n))


def _mod_index(i):
    npt = T_PROMPT // TM
    return jnp.where(i < npt, 0, 1 + (i - npt) // (DEC_SEQ // TM))


def _rope_index(i):
    npt = T_PROMPT // TM
    return jnp.where(i < npt, 0, 1 + (i - npt) % (DEC_SEQ // TM))


NPT = T_PROMPT // TM


def _pair_specs(width):
    return [pl.BlockSpec((TM, width), lambda i: (jnp.minimum(i, NPT - 1), 0)),
            pl.BlockSpec((TM, width), lambda i: (jnp.maximum(i - NPT, 0), 0))]


def _pick(p_ref, s_ref):
    return jnp.where(pl.program_id(0) < NPT, p_ref[...], s_ref[...])


def _inproj_kernel(xp_ref, xs_ref, mod_ref, g_ref, w_ref, cos_ref, sin_ref, *rest, even):
    if even:
        o_ref, ko_ref, vo_ref = rest
    else:
        qn_ref, kn_ref, o_ref, ko_ref, vo_ref = rest
    h = _rms(_pick(xp_ref, xs_ref), g_ref[...]) * (1.0 + mod_ref[1:2, :]) + mod_ref[0:1, :]
    h16 = h.astype(BF16)
    cos = cos_ref[...]
    sin = sin_ref[...]
    nblk = w_ref.shape[1] // LANES
    if even:
        nq = RET_QW // LANES
        rope_blocks = set(range(0, 2 * nq)) | set(range(4 * nq, 5 * nq + 1))
        scaled = set(range(nq, 2 * nq)) | set(range(4 * nq, 5 * nq))
        normed = {}
        cache_out = {5 * nq: (ko_ref, 0), 5 * nq + 1: (vo_ref, 0)}
    else:
        nq = ATT_QW // LANES
        nk = ATT_KW // LANES
        rope_blocks = set(range(0, nq + nk))
        scaled = set(range(0, nq))
        normed = {b: (qn_ref if b < nq else kn_ref) for b in range(nq + nk)}
        bd = _pair_block_diag((LANES, LANES)).astype(BF16)
        cache_out = {nq + b: (ko_ref, 2 * b) for b in range(nk)}
        cache_out.update({nq + nk + b: (vo_ref, 2 * b) for b in range(nk)})
    p = _dot(h16, w_ref[...])
    for b in range(nblk):
        blk = p[:, b * LANES:(b + 1) * LANES]
        if b in normed:
            ms = _head_sum(blk * blk, bd) * (1.0 / HEAD_DIM)
            blk = blk * lax.rsqrt(ms + EPS) * normed[b][...]
        if b in rope_blocks:
            blk = _rope(blk, cos, sin)
        if b in scaled:
            blk = blk * QK_SCALE
        o_ref[:, b * LANES:(b + 1) * LANES] = blk
        if b in cache_out:
            c_ref, head0 = cache_out[b]

            @pl.when(pl.program_id(0) < NPT)
            def _():
                for s in range(TM // SEQ):
                    for hh in range(2):
                        c_ref[s, 0, head0 + hh] = blk[s * SEQ:(s + 1) * SEQ, hh * HEAD_DIM:(hh + 1) * HEAD_DIM]


def _inproj(x, mod, layer, gain, w, cos_t, sin_t, qn=None, kn=None):
    even = qn is None
    n = w.shape[1]
    in_specs = _pair_specs(D_MODEL) + [
        pl.BlockSpec((None, None, 6, D_MODEL), lambda i: (layer, _mod_index(i), 0, 0)),
        pl.BlockSpec((1, D_MODEL), lambda i: (0, 0)),
        pl.BlockSpec((D_MODEL, n), lambda i: (0, 0)),
        pl.BlockSpec((TM, LANES), lambda i: (_rope_index(i), 0)),
        pl.BlockSpec((TM, LANES), lambda i: (_rope_index(i), 0)),
    ]
    args = [x[0], x[1], mod, gain.reshape(1, D_MODEL), w, cos_t, sin_t]
    if not even:
        in_specs += [pl.BlockSpec((1, LANES), lambda i: (0, 0))] * 2
        args += [jnp.tile(qn, 2).reshape(1, LANES), jnp.tile(kn, 2).reshape(1, LANES)]
    nkv = WIN_KV_HEADS if even else ATT_KV_HEADS
    cache = pl.BlockSpec((TM // SEQ, 1, nkv, SEQ, HEAD_DIM), lambda i: (jnp.minimum(i, NPT - 1), 0, 0, 0, 0))
    cache_shape = jax.ShapeDtypeStruct((BATCH, 1, nkv, SEQ, HEAD_DIM), F32)
    return pl.pallas_call(
        functools.partial(_inproj_kernel, even=even),
        grid=(T_ALL // TM,),
        in_specs=in_specs,
        out_specs=[pl.BlockSpec((TM, n), lambda i: (i, 0)), cache, cache],
        out_shape=[jax.ShapeDtypeStruct((T_ALL, n), F32), cache_shape, cache_shape],
        compiler_params=_cparams(("arbitrary",)),
        name="inproj_even" if even else "inproj_odd",
    )(*args)


RET_READOUT_ROWS = 512
RET_CHUNK = 2 * BLK


def _ret_kernel(q_ref, k_ref, v_ref, g_ref, dm_ref, qd_ref, kd_ref, cm_ref, *rest, nc, npb, zero_init):
    if zero_init:
        o_ref, sf_ref, sb_ref, of_scr, ob_scr = rest
        sf_ref[...] = jnp.zeros_like(sf_ref)
        sb_ref[...] = jnp.zeros_like(sb_ref)
    else:
        s0f_ref, s0b_ref, o_ref, sf_ref, sb_ref, of_scr, ob_scr = rest
        sf_ref[...] = s0f_ref[...]
        sb_ref[...] = s0b_ref[...]
    rc = RET_CHUNK
    lane = lax.broadcasted_iota(jnp.int32, (rc, LANES), 1)
    lo = lane < HEAD_DIM
    bd = _pair_block_diag((LANES, LANES))
    bd16 = bd.astype(BF16)

    def chunk(c, j, d, s_ref, dst):
        r = pl.ds(pl.multiple_of(c * rc, rc), rc)
        cols = slice(j * LANES, (j + 1) * LANES)
        qc = q_ref[r, cols]
        kc = k_ref[r, cols]
        vc = v_ref[r, cols]
        s = s_ref[j]
        v16 = vc.astype(BF16)
        a = _dot_nt(jnp.concatenate([jnp.where(lo, qc, 0.0), jnp.where(lo, 0.0, qc)], axis=0).astype(BF16),
                    kc.astype(BF16))
        lhs = jnp.concatenate([a[:rc] * dm_ref[j, d, 0], a[rc:] * dm_ref[j, d, 1], qc * qd_ref[j, d]], axis=1)
        rhs = jnp.concatenate([jnp.where(lo, v16, 0.0), jnp.where(lo, 0.0, v16), s.astype(BF16)], axis=0)
        dst[r, cols] = _dot(lhs.astype(BF16), rhs)
        kv = _dot((kc * kd_ref[j, d]).T.astype(BF16), v16)
        s_ref[j] = s * cm_ref[j, d] + jnp.where(bd, kv, 0.0)

    def body(c, carry):
        for j in range(npb):
            chunk(c, j, 0, sf_ref, of_scr)
            chunk(nc - 1 - c, j, 1, sb_ref, ob_scr)
        return carry

    lax.fori_loop(0, nc, body, 0)

    rt = min(nc * rc, RET_READOUT_ROWS)

    def readout(c, carry):
        r = pl.ds(pl.multiple_of(c * rt, rt), rt)
        for j in range(npb):
            cols = slice(j * LANES, (j + 1) * LANES)
            o = of_scr[r, cols] + ob_scr[r, cols]
            mu = _head_sum(o, bd16) * (1.0 / HEAD_DIM)
            dlt = o - mu
            var = _head_sum(dlt * dlt, bd16) * (1.0 / HEAD_DIM)
            o_ref[r, cols] = (dlt * lax.rsqrt(var + EPS) * _silu(g_ref[r, cols])).astype(o_ref.dtype)
        return carry

    lax.fori_loop(0, nc * rc // rt, readout, 0)


def _retention(p, tabs, s0f, s0b, n_seq, seq_len, row0, npb):
    npair = RET_HEADS // 2
    ng = npair // npb
    rb0 = row0 // seq_len
    dm, qd, kd, cm = tabs
    w = npb * LANES

    def col(c0):
        return pl.BlockSpec((seq_len, w), lambda b, j: (rb0 + b, c0 + j))

    rc = RET_CHUNK
    tab_rows = pl.BlockSpec((npb, 2, rc, LANES), lambda b, j: (j, 0, 0, 0))
    tab_state = pl.BlockSpec((npb, 2, LANES, LANES), lambda b, j: (j, 0, 0, 0))
    st = pl.BlockSpec((None, npb, LANES, LANES), lambda b, j: (b, j, 0, 0))
    zero_init = s0f is None
    states = [] if zero_init else [s0f, s0b]
    return pl.pallas_call(
        functools.partial(_ret_kernel, nc=seq_len // rc, npb=npb, zero_init=zero_init),
        grid=(n_seq, ng),
        in_specs=[col(0), col(ng), col(2 * ng), col(3 * ng),
                  pl.BlockSpec((npb, 2, 2, rc, rc), lambda b, j: (j, 0, 0, 0, 0)),
                  tab_rows, tab_rows, tab_state] + [st] * len(states),
        out_specs=[pl.BlockSpec((seq_len, w), lambda b, j: (b, j)), st, st],
        out_shape=[jax.ShapeDtypeStruct((n_seq * seq_len, RET_QW), BF16),
                   jax.ShapeDtypeStruct((n_seq, npair, LANES, LANES), F32),
                   jax.ShapeDtypeStruct((n_seq, npair, LANES, LANES), F32)],
        scratch_shapes=[pltpu.VMEM((seq_len, w), F32), pltpu.VMEM((seq_len, w), F32)],
        compiler_params=_cparams(("arbitrary", "arbitrary")),
        name="retention",
    )(p, p, p, p, dm, qd, kd, cm, *states)


def _retention_tables(decay_f, decay_b):
    npair = RET_HEADS // 2
    rc = RET_CHUNK
    idx = jnp.arange(rc, dtype=F32)
    diff = idx[:, None] - idx[None, :]

    def one(decay, backward):
        lg = jax.nn.log_sigmoid(decay.astype(F32))
        dmask = jnp.where(diff >= 0, jnp.exp(lg[:, None, None] * jnp.maximum(diff, 0.0)), 0.0)
        q_dec = jnp.exp(lg[:, None] * (idx + 1.0))
        k_dec = jnp.exp(lg[:, None] * (rc - 1.0 - idx))
        c_dec = jnp.exp(lg * rc)
        if backward:
            dmask = jnp.swapaxes(dmask, 1, 2)
            q_dec = q_dec[:, ::-1]
            k_dec = k_dec[:, ::-1]
        return dmask, q_dec, k_dec, c_dec

    def lanes(t):
        t = t.reshape(npair, 2, rc)
        return jnp.repeat(jnp.swapaxes(t, 1, 2), HEAD_DIM, axis=2)

    parts = [one(decay_f, False), one(decay_b, True)]
    dm = jnp.stack([p[0].reshape(npair, 2, rc, rc) for p in parts], axis=1)
    qd = jnp.stack([lanes(p[1]) for p in parts], axis=1)
    kd = jnp.stack([lanes(p[2]) for p in parts], axis=1)
    bd = _pair_block_diag((LANES, LANES))
    cm = jnp.stack([jnp.where(bd[None], jnp.repeat(p[3].reshape(npair, 2), HEAD_DIM, axis=1)[:, :, None], 0.0)
                    for p in parts], axis=1)
    return dm, qd, kd, cm


def _pair_states(s):
    b = s.shape[0]
    s = s.astype(F32).reshape(b, RET_HEADS // 2, 2, HEAD_DIM, HEAD_DIM)
    z = jnp.zeros_like(s[:, :, 0])
    top = jnp.concatenate([s[:, :, 0], z], axis=-1)
    bot = jnp.concatenate([z, s[:, :, 1]], axis=-1)
    return jnp.concatenate([top, bot], axis=-2)


def _unpair_states(s):
    b = s.shape[0]
    h0 = s[:, :, :HEAD_DIM, :HEAD_DIM]
    h1 = s[:, :, HEAD_DIM:, HEAD_DIM:]
    return jnp.stack([h0, h1], axis=2).reshape(b, RET_HEADS, HEAD_DIM, HEAD_DIM)


LOG2E = 1.4426950408889634


def _kv_variants(k, v):
    lane = lax.broadcasted_iota(jnp.int32, k.shape, 1)
    lo = lane < HEAD_DIM
    ka = jnp.where(lo, k, 0.0)
    kb = jnp.where(lo, 0.0, k)
    va = jnp.where(lane == HEAD_DIM, 1.0, jnp.where(lo, v, 0.0))
    vb = jnp.where(lane == 0, 1.0, jnp.where(lo, 0.0, v))
    ks = (ka, pltpu.roll(ka, HEAD_DIM, 1), pltpu.roll(kb, HEAD_DIM, 1), kb)
    vs = (va, pltpu.roll(va, HEAD_DIM, 1), pltpu.roll(vb, HEAD_DIM, 1), vb)
    return [t.astype(BF16) for t in ks], [t.astype(BF16) for t in vs]


def _attn_core(q, k, v, sink_ref, o_ref, mask, rows=slice(None)):
    _attn_jobs([(q, k, v, rows)], sink_ref, o_ref, mask)


def _attn_jobs(jobs, sink_ref, o_ref, mask):
    tq = jobs[0][0].shape[0]
    lane = lax.broadcasted_iota(jnp.int32, (2 * tq, LANES), 1)
    first = lax.broadcasted_iota(jnp.int32, (2 * tq, 1), 0) < tq
    if mask is not None:
        mask = jnp.concatenate([mask, mask], axis=0)
    chains = []
    for q, k, v, rows in jobs:
        ks, vs = _kv_variants(k, v)
        for half in (0, 1):
            q2 = jnp.concatenate([q[:, (2 * half) * LANES:(2 * half + 1) * LANES],
                                  q[:, (2 * half + 1) * LANES:(2 * half + 2) * LANES]], axis=0)
            q2 = (q2 * LOG2E).astype(BF16)
            for hh in (0, 1):
                chains.append(dict(q2=q2, k=ks[2 * half + hh], v=vs[2 * half + hh], half=half, hh=hh, rows=rows))
    for c in chains:
        s = _dot_nt(c["q2"], c["k"])
        c["s"] = s if mask is None else jnp.where(mask, s, NEG_INF)
    for c in chains:
        m = jnp.max(c["s"], axis=-1, keepdims=True)
        if sink_ref is not None:
            half, hh = c["half"], c["hh"]
            c["snk"] = LOG2E * jnp.where(first, sink_ref[2 * half, hh:hh + 1, 0:1],
                                         sink_ref[2 * half + 1, hh:hh + 1, 0:1])
            m = jnp.maximum(m, c["snk"])
        c["m"] = m
    for c in chains:
        c["e"] = jnp.exp2((c["s"] - c["m"]).astype(BF16))
    for c in chains:
        c["acc"] = _dot(c["e"], c["v"])
    for i in range(0, len(chains), 2):
        out = None
        for c in chains[i:i + 2]:
            hh = c["hh"]
            ones_lane = HEAD_DIM if hh == 0 else 0
            den = c["acc"][:, ones_lane:ones_lane + 1]
            if sink_ref is not None:
                den = den + jnp.exp2(c["snk"] - c["m"])
            own = (lane < HEAD_DIM) if hh == 0 else (lane >= HEAD_DIM)
            o = jnp.where(own, c["acc"], 0.0) * (1.0 / den)
            out = o if out is None else out + o
        half, rows = chains[i]["half"], chains[i]["rows"]
        out = out.astype(o_ref.dtype)
        o_ref[rows, (2 * half) * LANES:(2 * half + 1) * LANES] = out[:tq]
        o_ref[rows, (2 * half + 1) * LANES:(2 * half + 2) * LANES] = out[tq:]


def _full_attn_kernel(q_ref, k_ref, v_ref, *rest, has_sink):
    if has_sink:
        sink_ref, o_ref = rest
    else:
        sink_ref, (o_ref,) = None, rest
    jobs = []
    for s in range(PROMPT_SEQS_PER_STEP):
        r = slice(s * SEQ, (s + 1) * SEQ)
        jobs.append((q_ref[r, :], k_ref[r, :], v_ref[r, :], r))
    _attn_jobs(jobs, sink_ref, o_ref, None)


PROMPT_SEQS_PER_STEP = 4


def _full_attention(p, q_cb0, k_cb0, v_cb0, n_groups, sink=None):
    rows = PROMPT_SEQS_PER_STEP * SEQ
    qw = 4 * LANES
    in_specs = [
        pl.BlockSpec((rows, qw), lambda b, g: (b, q_cb0 + g)),
        pl.BlockSpec((rows, LANES), lambda b, g: (b, k_cb0 + g)),
        pl.BlockSpec((rows, LANES), lambda b, g: (b, v_cb0 + g)),
    ]
    args = [p, p, p]
    if sink is not None:
        in_specs.append(pl.BlockSpec((None, 4, 2, LANES), lambda b, g: (g, 0, 0, 0)))
        args.append(sink)
    return pl.pallas_call(
        functools.partial(_full_attn_kernel, has_sink=sink is not None),
        grid=(T_PROMPT // rows, n_groups),
        in_specs=in_specs,
        out_specs=pl.BlockSpec((rows, qw), lambda b, g: (b, g)),
        out_shape=jax.ShapeDtypeStruct((T_PROMPT, n_groups * qw), BF16),
        compiler_params=_cparams(("arbitrary", "arbitrary")),
        name="full_attention",
    )(*args)


def _latent_attn_kernel(q_ref, k_ref, v_ref, kx_ref, vx_ref, o_ref, kvar, vvar, *, tq, kb):
    lk = DEC_SEQ + PAST_LEN

    @pl.when(pl.program_id(2) == 0)
    def _():
        for src_k, src_v, r0, nrows in ((k_ref, v_ref, 0, DEC_SEQ), (kx_ref, vx_ref, DEC_SEQ, PAST_LEN)):
            for c in range(nrows // kb):
                src = pl.ds(c * kb, kb)
                dst = pl.ds(r0 + c * kb, kb)
                ks, vs = _kv_variants(src_k[src, :], src_v[src, :])
                for i in range(4):
                    kvar[i, dst, :] = ks[i]
                    vvar[i, dst, :] = vs[i]

    lane = lax.broadcasted_iota(jnp.int32, (2 * tq, LANES), 1)
    for half in (0, 1):
        q2 = jnp.concatenate([q_ref[:, (2 * half) * LANES:(2 * half + 1) * LANES],
                              q_ref[:, (2 * half + 1) * LANES:(2 * half + 2) * LANES]], axis=0)
        q2 = (q2 * LOG2E).astype(BF16)
        out = None
        for hh in (0, 1):
            var = 2 * half + hh
            m = jnp.full((2 * tq, 1), -jnp.inf, F32)
            acc = jnp.zeros((2 * tq, LANES), F32)
            for j in range(lk // kb):
                rows = pl.ds(j * kb, kb)
                s = _dot_nt(q2, kvar[var, rows, :])
                m_new = jnp.maximum(m, jnp.max(s, axis=-1, keepdims=True))
                e = jnp.exp2((s - m_new).astype(BF16))
                acc = jnp.exp2(m - m_new) * acc + _dot(e, vvar[var, rows, :])
                m = m_new
            own = (lane < HEAD_DIM) if hh == 0 else (lane >= HEAD_DIM)
            ones_lane = HEAD_DIM if hh == 0 else 0
            o = jnp.where(own, acc, 0.0) * (1.0 / acc[:, ones_lane:ones_lane + 1])
            out = o if out is None else out + o
        out = out.astype(o_ref.dtype)
        o_ref[:, (2 * half) * LANES:(2 * half + 1) * LANES] = out[:tq]
        o_ref[:, (2 * half + 1) * LANES:(2 * half + 2) * LANES] = out[tq:]


def _latent_attention(p, kx, vx):
    tq, kb = 512, 512
    nq = DEC_SEQ // tq
    ngr = ATT_KV_HEADS // 2
    kcb = ATT_QW // LANES
    rb_q = T_PROMPT // tq
    rb_k = T_PROMPT // DEC_SEQ
    lk = DEC_SEQ + PAST_LEN
    ctx = pl.BlockSpec((None, PAST_LEN, LANES), lambda b, g, i: (b, 0, g))
    return pl.pallas_call(
        functools.partial(_latent_attn_kernel, tq=tq, kb=kb),
        grid=(DEC_BATCH, ngr, nq),
        in_specs=[pl.BlockSpec((tq, 4 * LANES), lambda b, g, i: (rb_q + b * nq + i, g)),
                  pl.BlockSpec((DEC_SEQ, LANES), lambda b, g, i: (rb_k + b, kcb + g)),
                  pl.BlockSpec((DEC_SEQ, LANES), lambda b, g, i: (rb_k + b, kcb + ngr + g)),
                  ctx, ctx],
        out_specs=pl.BlockSpec((tq, 4 * LANES), lambda b, g, i: (b * nq + i, g)),
        out_shape=jax.ShapeDtypeStruct((T_SAMPLE, ATT_QW), BF16),
        scratch_shapes=[pltpu.VMEM((4, lk, LANES), BF16), pltpu.VMEM((4, lk, LANES), BF16)],
        compiler_params=_cparams(("arbitrary", "arbitrary", "arbitrary")),
        name="latent_attention",
    )(p, p, p, kx, vx)


WIN_TQ = 2 * BLK
WIN_KBLKS = WIN_TQ // BLK + 2


def _window_attn_kernel(q_ref, *refs):
    k_refs = refs[:WIN_KBLKS]
    v_refs = refs[WIN_KBLKS:2 * WIN_KBLKS]
    kx_ref, vx_ref, sink_ref, o_ref = refs[2 * WIN_KBLKS:]
    n = pl.program_id(1)
    k = jnp.concatenate([r[...] for r in k_refs] + [kx_ref[...]], axis=0)
    v = jnp.concatenate([r[...] for r in v_refs] + [vx_ref[...]], axis=0)
    nloc = WIN_KBLKS * BLK
    shape = (WIN_TQ, nloc + PAST_LEN)
    i = lax.broadcasted_iota(jnp.int32, shape, 0)
    r = lax.broadcasted_iota(jnp.int32, shape, 1)
    kpos = n * WIN_TQ - BLK + r
    local = (jnp.abs(r - BLK - i) <= WINDOW) & (kpos >= 0) & (kpos < DEC_SEQ)
    mask = local | (r >= nloc)
    _attn_core(q_ref[...], k, v, sink_ref, o_ref, mask)


def _window_attention(p, kx, vx, sink):
    nb = DEC_SEQ // BLK
    nq = DEC_SEQ // WIN_TQ
    rb0 = T_PROMPT // BLK
    qcb = (4 * RET_QW) // (4 * LANES)
    kcb = (4 * RET_QW + WIN_QW) // LANES
    vcb = kcb + 1

    def kv(cb, off):
        return pl.BlockSpec((BLK, LANES),
                            lambda b, n: (rb0 + b * nb + jnp.clip(n * (WIN_TQ // BLK) + off, 0, nb - 1), cb))

    offs = range(-1, WIN_KBLKS - 1)
    ctx = pl.BlockSpec((None, PAST_LEN, LANES), lambda b, n: (b, 0, 0))
    qspec = pl.BlockSpec((WIN_TQ, 4 * LANES), lambda b, n: (T_PROMPT // WIN_TQ + b * nq + n, qcb))
    return pl.pallas_call(
        _window_attn_kernel,
        grid=(DEC_BATCH, nq),
        in_specs=[qspec] + [kv(kcb, o) for o in offs] + [kv(vcb, o) for o in offs] + [
            ctx, ctx, pl.BlockSpec((4, 2, LANES), lambda b, n: (0, 0, 0))],
        out_specs=pl.BlockSpec((WIN_TQ, 4 * LANES), lambda b, n: (b * nq + n, 0)),
        out_shape=jax.ShapeDtypeStruct((T_SAMPLE, WIN_QW), BF16),
        compiler_params=_cparams(("arbitrary", "arbitrary")),
        name="window_attention",
    )(*([p] * (1 + 2 * WIN_KBLKS)), kx, vx, sink)


def _sink_table(sink, n_groups):
    s = sink.astype(F32).reshape(n_groups, 4, 2, 1)
    return jnp.broadcast_to(s, (n_groups, 4, 2, LANES))


def _cache_rows(cache):
    b, h, l, d = cache.shape
    return cache.astype(F32).transpose(0, 2, 1, 3).reshape(b, l, h * d)


def _outproj_kernel(*refs, n_in):
    a_refs = refs[:2 * n_in]
    w_refs = refs[2 * n_in:3 * n_in]
    xp_ref, xs_ref, mod_ref, g_ref, wr_ref, xn_ref, hfp_ref, hfs_ref, lg_ref = refs[3 * n_in:]
    i = pl.program_id(0)
    y = None
    for k, w_ref in enumerate(w_refs):
        a = _pick(a_refs[2 * k], a_refs[2 * k + 1])
        t = _dot(a.astype(BF16), w_ref[...])
        y = t if y is None else y + t
    xn = _pick(xp_ref, xs_ref) + mod_ref[2:3, :] * _rms(y, g_ref[1:2, :])
    hf = _rms(xn, g_ref[2:3, :]) * (1.0 + mod_ref[4:5, :]) + mod_ref[3:4, :]
    xn_ref[...] = xn
    lg_ref[...] = _dot_nt(wr_ref[...], hf.astype(BF16))

    @pl.when(i < NPT)
    def _():
        hfp_ref[...] = hf.astype(hfp_ref.dtype)

    @pl.when(i >= NPT)
    def _():
        hfs_ref[...] = hf.reshape(TM, D_MODEL // LANES, LANES)


def _outproj(parts, w_parts, x, mod, layer, gains, w_router_t):
    n_in = len(parts)
    in_specs = []
    args = []
    for a_p, a_s in parts:
        in_specs += _pair_specs(a_p.shape[1])
        args += [a_p, a_s]
    in_specs += [pl.BlockSpec(w.shape, lambda i: (0, 0)) for w in w_parts]
    in_specs += _pair_specs(D_MODEL) + [
        pl.BlockSpec((None, None, 6, D_MODEL), lambda i: (layer, _mod_index(i), 0, 0)),
        pl.BlockSpec((4, D_MODEL), lambda i: (0, 0)),
        pl.BlockSpec((N_EXPERTS, D_MODEL), lambda i: (0, 0)),
    ]
    row = pl.BlockSpec((TM, D_MODEL), lambda i: (i, 0))
    hf_p, _ = _pair_specs(D_MODEL)
    hf_s = pl.BlockSpec((TM, D_MODEL // LANES, LANES), lambda i: (jnp.maximum(i - NPT, 0), 0, 0))
    return pl.pallas_call(
        functools.partial(_outproj_kernel, n_in=n_in),
        grid=(T_ALL // TM,),
        in_specs=in_specs,
        out_specs=[row, hf_p, hf_s, pl.BlockSpec((N_EXPERTS, TM), lambda i: (0, i))],
        out_shape=[jax.ShapeDtypeStruct((T_ALL, D_MODEL), F32),
                   jax.ShapeDtypeStruct((T_PROMPT, D_MODEL), BF16),
                   jax.ShapeDtypeStruct((T_SAMPLE, D_MODEL // LANES, LANES), F32),
                   jax.ShapeDtypeStruct((N_EXPERTS, T_ALL), F32)],
        compiler_params=_cparams(("arbitrary",)),
        name="outproj",
    )(*args, *w_parts, x[0], x[1], mod, gains, w_router_t)


TILE_SUB = D_MODEL // LANES


def _gather_kernel(idx_ref, h_ref, o_ref, buf, *, eg, cap):
    b = pl.program_id(0)
    g = pl.program_id(1)
    for e in range(eg):
        base = (b * N_EXPERTS + g * eg + e) * cap

        def body(c, carry):
            buf[c] = h_ref[idx_ref[base + c]]
            return carry

        lax.fori_loop(0, cap, body, 0, unroll=8)
        o_ref[e] = buf[...].reshape(cap, D_MODEL).astype(BF16)


def _gather(h3, idx, n_seq, seq_len, cap, eg):
    return pl.pallas_call(
        functools.partial(_gather_kernel, eg=eg, cap=cap),
        grid_spec=pltpu.PrefetchScalarGridSpec(
            num_scalar_prefetch=1,
            grid=(n_seq, N_EXPERTS // eg),
            in_specs=[pl.BlockSpec((seq_len, TILE_SUB, LANES), lambda b, g, idx: (b, 0, 0))],
            out_specs=pl.BlockSpec((eg, cap, D_MODEL), lambda b, g, idx: (g, b, 0)),
            scratch_shapes=[pltpu.VMEM((cap, TILE_SUB, LANES), F32)],
        ),
        out_shape=jax.ShapeDtypeStruct((N_EXPERTS, n_seq * cap, D_MODEL), BF16),
        compiler_params=_cparams(("arbitrary", "arbitrary")),
        name="moe_gather",
    )(idx, h3)


def _moe_kernel(xp_ref, xs_ref, wg_ref, wu_ref, wd_ref, gwp_ref, gws_ref, yp_ref, ys_ref, hid_scr, wd_scr):
    f = pl.program_id(1)
    wg = wg_ref[...].astype(BF16)
    wu = wu_ref[...].astype(BF16)
    cols = pl.ds(pl.multiple_of(f * TF, TF), TF)
    wd_scr[cols, :] = wd_ref[...].astype(BF16)
    for h, x_ref in enumerate((xp_ref, xs_ref)):
        x = x_ref[...]
        hid_scr[h, :, cols] = (_silu(_dot(x, wg)) * _dot(x, wu)).astype(BF16)

    @pl.when(f == pl.num_programs(1) - 1)
    def _():
        for h, (gw_ref, y_ref) in enumerate(((gwp_ref, yp_ref), (gws_ref, ys_ref))):
            gw = jnp.broadcast_to(gw_ref[...], (LANES, hid_scr.shape[1])).T[:, 0:1]
            for r in range(0, hid_scr.shape[1], MOE_ROW_TILE):
                rows = slice(r, r + MOE_ROW_TILE)
                y_ref[rows, :] = _dot(hid_scr[h, rows, :], wd_scr[...]) * gw[rows, :]


MOE_ROW_TILE = 512


def _moe_ffn(xg_p, xg_s, w_gate, w_up, w_down, layer, gw_p, gw_s):
    assert ROWS_P == ROWS_S
    nf = EXPERT_FF // TF
    xspec_p = pl.BlockSpec((None, ROWS_P, D_MODEL), lambda e, f: (e, 0, 0))
    xspec_s = pl.BlockSpec((None, ROWS_S, D_MODEL), lambda e, f: (e, 0, 0))
    return pl.pallas_call(
        _moe_kernel,
        grid=(N_EXPERTS, nf),
        in_specs=[xspec_p, xspec_s,
                  pl.BlockSpec((None, None, D_MODEL, TF), lambda e, f: (layer, e, 0, f)),
                  pl.BlockSpec((None, None, D_MODEL, TF), lambda e, f: (layer, e, 0, f)),
                  pl.BlockSpec((None, None, TF, D_MODEL), lambda e, f: (layer, e, f, 0)),
                  pl.BlockSpec((None, 1, ROWS_P), lambda e, f: (e, 0, 0)),
                  pl.BlockSpec((None, 1, ROWS_S), lambda e, f: (e, 0, 0))],
        out_specs=[xspec_p, xspec_s],
        out_shape=[jax.ShapeDtypeStruct((N_EXPERTS, ROWS_P, D_MODEL), F32),
                   jax.ShapeDtypeStruct((N_EXPERTS, ROWS_S, D_MODEL), F32)],
        scratch_shapes=[pltpu.VMEM((2, ROWS_P, EXPERT_FF), BF16),
                        pltpu.VMEM((EXPERT_FF, D_MODEL), BF16)],
        compiler_params=_cparams(("arbitrary", "arbitrary")),
        name="moe_ffn",
    )(xg_p, xg_s, w_gate, w_up, w_down, gw_p, gw_s)


COMBINE_GROUP = 8


def _combine_kernel(idx_ref, y_ref, o_ref, y3, *, eg, cap):
    b = pl.program_id(0)
    g = pl.program_id(1)

    @pl.when(g == 0)
    def _():
        o_ref[...] = jnp.zeros_like(o_ref)

    for e in range(eg):
        base = (b * N_EXPERTS + g * eg + e) * cap
        y3[...] = y_ref[e].reshape(cap, TILE_SUB, LANES)

        def body(c, carry):
            c0 = c * COMBINE_GROUP
            toks = [idx_ref[base + c0 + k] for k in range(COMBINE_GROUP)]
            rows = [o_ref[t] + y3[c0 + k] for k, t in enumerate(toks)]
            for t, row in zip(toks, rows):
                o_ref[t] = row
            return carry

        lax.fori_loop(0, cap // COMBINE_GROUP, body, 0)


def _combine(y, idx, n_seq, seq_len, cap, eg):
    return pl.pallas_call(
        functools.partial(_combine_kernel, eg=eg, cap=cap),
        grid_spec=pltpu.PrefetchScalarGridSpec(
            num_scalar_prefetch=1,
            grid=(n_seq, N_EXPERTS // eg),
            in_specs=[pl.BlockSpec((eg, cap, D_MODEL), lambda b, g, idx: (g, b, 0))],
            out_specs=pl.BlockSpec((seq_len, TILE_SUB, LANES), lambda b, g, idx: (b, 0, 0)),
            scratch_shapes=[pltpu.VMEM((cap, TILE_SUB, LANES), F32)],
        ),
        out_shape=jax.ShapeDtypeStruct((n_seq * seq_len, TILE_SUB, LANES), F32),
        compiler_params=_cparams(("arbitrary", "arbitrary")),
        name="moe_combine",
    )(idx, y)


def _onehot_t(idx_row, n_tok):
    tok = lax.broadcasted_iota(jnp.int32, (n_tok, idx_row.shape[-1]), 0)
    return jnp.where(tok == idx_row, 1.0, 0.0)


def _gather_mm_kernel(idx_ref, h_ref, o_ref, *, cap, seq_len):
    for s in range(PROMPT_SEQS_PER_STEP):
        sel = _onehot_t(idx_ref[s], seq_len).T.astype(BF16)
        h = h_ref[s * seq_len:(s + 1) * seq_len, :].astype(BF16)
        rows = _dot(sel, h).astype(BF16)
        for e in range(N_EXPERTS):
            o_ref[e, s * cap:(s + 1) * cap, :] = rows[e * cap:(e + 1) * cap]


def _gather_mm(h, idx, n_seq, seq_len, cap):
    ns = PROMPT_SEQS_PER_STEP
    return pl.pallas_call(
        functools.partial(_gather_mm_kernel, cap=cap, seq_len=seq_len),
        grid=(n_seq // ns,),
        in_specs=[pl.BlockSpec((ns, 1, N_EXPERTS * cap), lambda b: (b, 0, 0)),
                  pl.BlockSpec((ns * seq_len, D_MODEL), lambda b: (b, 0))],
        out_specs=pl.BlockSpec((N_EXPERTS, ns * cap, D_MODEL), lambda b: (0, b, 0)),
        out_shape=jax.ShapeDtypeStruct((N_EXPERTS, n_seq * cap, D_MODEL), BF16),
        compiler_params=_cparams(("arbitrary",)),
        name="moe_gather_mm",
    )(idx, h)


def _combine_mm_kernel(idx_ref, y_ref, o_ref, *, cap, seq_len):
    for s in range(PROMPT_SEQS_PER_STEP):
        sel = _onehot_t(idx_ref[s], seq_len).astype(BF16)
        y = jnp.concatenate([y_ref[e, s * cap:(s + 1) * cap, :] for e in range(N_EXPERTS)], axis=0)
        hi = y.astype(BF16)
        lo = (y - hi.astype(F32)).astype(BF16)
        o_ref[s * seq_len:(s + 1) * seq_len, :] = _dot(sel, hi) + _dot(sel, lo)


def _combine_mm(y, idx, n_seq, seq_len, cap):
    ns = PROMPT_SEQS_PER_STEP
    return pl.pallas_call(
        functools.partial(_combine_mm_kernel, cap=cap, seq_len=seq_len),
        grid=(n_seq // ns,),
        in_specs=[pl.BlockSpec((ns, 1, N_EXPERTS * cap), lambda b: (b, 0, 0)),
                  pl.BlockSpec((N_EXPERTS, ns * cap, D_MODEL), lambda b: (0, b, 0))],
        out_specs=pl.BlockSpec((ns * seq_len, D_MODEL), lambda b: (b, 0)),
        out_shape=jax.ShapeDtypeStruct((n_seq * seq_len, D_MODEL), F32),
        compiler_params=_cparams(("arbitrary",)),
        name="moe_combine_mm",
    )(idx, y)


def _postnorm_kernel(x_ref, yp_ref, ys_ref, mod_ref, g_ref, op_ref, os_ref):
    i = pl.program_id(0)
    y = jnp.where(i < NPT, yp_ref[...], ys_ref[...].reshape(TM, D_MODEL))
    o = x_ref[...] + mod_ref[5:6, :] * _rms(y, g_ref[3:4, :])

    @pl.when(i < NPT)
    def _():
        op_ref[...] = o

    @pl.when(i >= NPT)
    def _():
        os_ref[...] = o


def _postnorm(x, y_p, y_s, mod, layer, gains):
    return pl.pallas_call(
        _postnorm_kernel,
        grid=(T_ALL // TM,),
        in_specs=[pl.BlockSpec((TM, D_MODEL), lambda i: (i, 0)), _pair_specs(D_MODEL)[0],
                  pl.BlockSpec((TM, TILE_SUB, LANES), lambda i: (jnp.maximum(i - NPT, 0), 0, 0)),
                  pl.BlockSpec((None, None, 6, D_MODEL), lambda i: (layer, _mod_index(i), 0, 0)),
            pl.BlockSpec((4, D_MODEL), lambda i: (0, 0))],
        out_specs=_pair_specs(D_MODEL),
        out_shape=[jax.ShapeDtypeStruct((T_PROMPT, D_MODEL), F32),
                   jax.ShapeDtypeStruct((T_SAMPLE, D_MODEL), F32)],
        compiler_params=_cparams(("arbitrary",)),
        name="postnorm",
    )(x, y_p, y_s, mod, gains)


def _excl_prefix(mask):
    r, n = mask.shape
    i0 = lax.broadcasted_iota(jnp.int32, (LANES, LANES), 0)
    i1 = lax.broadcasted_iota(jnp.int32, (LANES, LANES), 1)
    upper = jnp.where(i0 < i1, 1.0, 0.0).astype(BF16)
    ones = jnp.where(mask, 1.0, 0.0)
    carry = jnp.zeros((r, 1), F32)
    out = []
    for c in range(n // LANES):
        ch = ones[:, c * LANES:(c + 1) * LANES]
        out.append(_dot(ch.astype(BF16), upper) + carry)
        carry = carry + jnp.sum(ch, axis=1, keepdims=True)
    return jnp.concatenate(out, axis=1)


COMPACT_SLOTS = 256


def _compact_mxu(pos, aff, idx_ref, gw_ref, n, cap):
    sub = lax.broadcasted_iota(jnp.int32, (8, n), 0)
    tok = lax.broadcasted_iota(jnp.int32, (8, n), 1)
    base = jnp.where(sub == 0, tok >> 6, jnp.where(sub == 1, tok & 63, 0)).astype(F32)
    pad = jnp.zeros((LANES - 8, n), F32)
    slot0 = lax.broadcasted_iota(jnp.int32, (COMPACT_SLOTS, n), 0).astype(F32)
    for e in range(N_EXPERTS):
        a = aff[e:e + 1, :]
        a_hi = a.astype(BF16).astype(F32)
        rest = a - a_hi
        a_mid = rest.astype(BF16).astype(F32)
        a_lo = rest - a_mid
        pay = base + jnp.where(sub == 2, a_hi, 0.0) + jnp.where(sub == 3, a_mid, 0.0) + jnp.where(sub == 4, a_lo, 0.0)
        pay = jnp.concatenate([pay, pad], axis=0).astype(BF16)
        pos_e = pos[e:e + 1, :]
        for c in range(cap // COMPACT_SLOTS):
            onehot = jnp.where(pos_e == slot0 + float(c * COMPACT_SLOTS), 1.0, 0.0).astype(BF16)
            got = _dot_nt(onehot, pay).T
            cols = slice(c * COMPACT_SLOTS, (c + 1) * COMPACT_SLOTS)
            idx_ref[0, e:e + 1, cols] = (got[0:1] * 64.0 + got[1:2]).astype(jnp.int32)
            gw_ref[0, e:e + 1, cols] = got[2:3] + got[3:4] + got[4:5]


def _route_kernel(lg_ref, idx_ref, gw_ref, *, nb, n, cap):
    affs = []
    for s in range(nb):
        l = lg_ref[:, s * n:(s + 1) * n]
        e = jnp.exp(l - jnp.max(l, axis=0, keepdims=True))
        affs.append(e / jnp.sum(e, axis=0, keepdims=True))
    aff = affs[0] if nb == 1 else jnp.concatenate(affs, axis=0)
    rows = nb * N_EXPERTS
    bits = jnp.zeros((rows, 1), jnp.int32)
    for bit in range(30, -1, -1):
        cand = bits | (1 << bit)
        cnt = jnp.sum(jnp.where(aff >= pltpu.bitcast(cand, F32), 1.0, 0.0), axis=1, keepdims=True)
        bits = jnp.where(cnt >= cap, cand, bits)
    thr = pltpu.bitcast(bits, F32)
    gt = aff > thr
    eq = aff == thr
    need = cap - jnp.sum(jnp.where(gt, 1.0, 0.0), axis=1, keepdims=True)
    sel = gt | (eq & (_excl_prefix(eq) < need))
    pos = jnp.where(sel, _excl_prefix(sel), -1.0)
    if cap > LANES:
        _compact_mxu(pos, aff, idx_ref, gw_ref, n, cap)
        return
    tok = lax.broadcasted_iota(jnp.int32, (rows, n), 1).astype(F32)
    capw = max(cap, LANES)
    slot_lane = lax.broadcasted_iota(jnp.int32, (rows, capw), 1)

    def slot(s, carry):
        idx_acc, gw_acc = carry
        hit = pos == lax.convert_element_type(s, F32)
        ic = jnp.sum(jnp.where(hit, tok, 0.0), axis=1, keepdims=True)
        gc = jnp.sum(jnp.where(hit, aff, 0.0), axis=1, keepdims=True)
        here = slot_lane == s
        return jnp.where(here, ic, idx_acc), jnp.where(here, gc, gw_acc)

    zero = jnp.zeros((rows, capw), F32)
    idx_acc, gw_acc = lax.fori_loop(0, cap, slot, (zero, zero), unroll=8)
    for s in range(nb):
        r = slice(s * N_EXPERTS, (s + 1) * N_EXPERTS)
        idx_ref[s] = idx_acc[r, :cap].astype(jnp.int32)
        gw_ref[s] = gw_acc[r, :cap]


def _route(logits_t, n_seq, seq_len, cap, col0, nb):
    cb0 = col0 // (nb * seq_len)
    out = pl.BlockSpec((nb, N_EXPERTS, cap), lambda b: (b, 0, 0))
    return pl.pallas_call(
        functools.partial(_route_kernel, nb=nb, n=seq_len, cap=cap),
        grid=(n_seq // nb,),
        in_specs=[pl.BlockSpec((N_EXPERTS, nb * seq_len), lambda b: (0, cb0 + b))],
        out_specs=[out, out],
        out_shape=[jax.ShapeDtypeStruct((n_seq, N_EXPERTS, cap), jnp.int32),
                   jax.ShapeDtypeStruct((n_seq, N_EXPERTS, cap), F32)],
        compiler_params=_cparams(("arbitrary",)),
        name="route",
    )(logits_t)


def _expert_ffn(hf_p, hf_s, logits_t, w_gate, w_up, w_down, layer):
    idx_p, gw_p = _route(logits_t, BATCH, SEQ, CAP_P, 0, 8)
    idx_s, gw_s = _route(logits_t, DEC_BATCH, DEC_SEQ, CAP_S, T_PROMPT, 1)
    idx_p = idx_p.reshape(BATCH, 1, N_EXPERTS * CAP_P)
    idx_s = idx_s.reshape(DEC_BATCH * N_EXPERTS * CAP_S)
    xg_p = _gather_mm(hf_p, idx_p, BATCH, SEQ, CAP_P)
    xg_s = _gather(hf_s, idx_s, DEC_BATCH, DEC_SEQ, CAP_S, 4)
    gw_p = gw_p.transpose(1, 0, 2).reshape(N_EXPERTS, 1, ROWS_P)
    gw_s = gw_s.transpose(1, 0, 2).reshape(N_EXPERTS, 1, ROWS_S)
    y_p, y_s = _moe_ffn(xg_p, xg_s, w_gate, w_up, w_down, layer, gw_p, gw_s)
    o_p = _combine_mm(y_p, idx_p, BATCH, SEQ, CAP_P)
    o_s = _combine(y_s, idx_s, DEC_BATCH, DEC_SEQ, CAP_S, 2)
    return o_p, o_s


def _rope_tables():
    t = np.arange(DEC_SEQ)
    row = (t // GRID_W).astype(np.float32)
    col = (t % GRID_W).astype(np.float32)
    inv = (ROPE_THETA ** (-np.arange(0, AXIS_DIM, 2, dtype=np.float32) / AXIS_DIM)).astype(np.float32)
    ar = row[:, None] * inv[None]
    ac = col[:, None] * inv[None]
    ang = np.concatenate([ar, ar, ac, ac], axis=-1)
    ang = np.concatenate([ang, ang], axis=-1)
    cos = np.concatenate([np.ones((TM, LANES), np.float32), np.cos(ang)], axis=0)
    sin = np.concatenate([np.zeros((TM, LANES), np.float32), np.sin(ang)], axis=0)
    return jnp.asarray(cos, F32), jnp.asarray(sin, F32)


def kernel(x_prompt, x_sample, state_ret_fwd, state_ret_bwd, cache_win_k, cache_win_v, cache_attn_k, cache_attn_v, c, c_ctx, w_ada, b_ada, norm_gains, w_in_even, w_out_even, ret_decay_fwd, ret_decay_bwd, win_sink, w_in_odd, w_out_odd, q_norm, k_norm, w_router, w_gate, w_up, w_down):
    x = (x_prompt.reshape(T_PROMPT, D_MODEL), x_sample.reshape(T_SAMPLE, D_MODEL))
    cond =jnp.concatenate([c_ctx[None, :], c, jnp.zeros((8 - 1 - DEC_BATCH, D_MODEL), F32)], axis=0)
    mod = _modulation(cond, w_ada, b_ada).reshape(DEPTH, 8, 6, D_MODEL)
    cos_t, sin_t = _rope_tables()
    outs = {}
    for layer in range(DEPTH):
        j = layer // 2
        gains = norm_gains[layer]
        if layer % 2 == 0:
            p, outs["win_k"], outs["win_v"] = _inproj(x, mod, layer, gains[0], w_in_even[j].astype(BF16),
                                                      cos_t, sin_t)
            tabs = _retention_tables(ret_decay_fwd[j], ret_decay_bwd[j])
            ret_p, sf, sb = _retention(p, tabs, None, None, BATCH, SEQ, 0, 4)
            ret_s, _, _ = _retention(p, tabs, _pair_states(state_ret_fwd[:, j]),
                                     _pair_states(state_ret_bwd[:, j]), DEC_BATCH, DEC_SEQ, T_PROMPT, 2)
            sink = _sink_table(win_sink[j], 1)
            qcb = (4 * RET_QW) // (4 * LANES)
            kcb = (4 * RET_QW + WIN_QW) // LANES
            win_p = _full_attention(p, qcb, kcb, kcb + 1, 1, sink=sink)
            win_s = _window_attention(p, _cache_rows(cache_win_k[:, j]), _cache_rows(cache_win_v[:, j]),
                                      sink[0])
            w_out = w_out_even[j].astype(BF16)
            parts = [(ret_p, ret_s), (win_p, win_s)]
            w_parts = [w_out[:RET_QW], w_out[RET_QW:]]
            outs["ret_f"] = _unpair_states(sf)[:, None]
            outs["ret_b"] = _unpair_states(sb)[:, None]
        else:
            p, outs["att_k"], outs["att_v"] = _inproj(x, mod, layer, gains[0], w_in_odd[j].astype(BF16),
                                                      cos_t, sin_t, qn=q_norm[j], kn=k_norm[j])
            ngr = ATT_KV_HEADS // 2
            kcb = ATT_QW // LANES
            att_p = _full_attention(p, 0, kcb, kcb + ngr, ngr)
            att_s = _latent_attention(p, _cache_rows(cache_attn_k[:, j]), _cache_rows(cache_attn_v[:, j]))
            parts = [(att_p, att_s)]
            w_parts = [w_out_odd[j].astype(BF16)]
        xn, hf_p, hf_s, logits_t = _outproj(parts, w_parts, x, mod, layer, gains,
                                            w_router[layer].T.astype(BF16))
        o_p, o_s = _expert_ffn(hf_p, hf_s, logits_t, w_gate, w_up, w_down, layer)
        x = _postnorm(xn, o_p, o_s, mod, layer, gains)
    y_prompt = x[0].reshape(BATCH, SEQ, D_MODEL)
    y_sample = x[1].reshape(DEC_BATCH, DEC_SEQ, D_MODEL)
    return (y_prompt, y_sample, outs["ret_f"], outs["ret_b"], outs["win_k"], outs["win_v"],
            outs["att_k"], outs["att_v"])
```
